```python
import math
import jax
import jax.numpy as jnp
from jax import lax
import numpy as np

D_MODEL = 2048
BATCH = 4
SEQ = 2048
DEPTH = 1
DEC_BATCH = 8
DEC_SEQ = 4
PAST_LEN = 16384
PAGE_SIZE = 128

MIX_WIDTH = D_MODEL
MLSTM_WIDTH = MIX_WIDTH // 2
MLSTM_HEADS = 8
MLSTM_DH = MLSTM_WIDTH // MLSTM_HEADS
MLSTM_CHUNK = 64
CONV_W = 4
DIFF_WIDTH = MIX_WIDTH - MLSTM_WIDTH
DIFF_HEADS = 8
DIFF_DV = DIFF_WIDTH // DIFF_HEADS
DIFF_DQK = DIFF_DV // 2
DIFF_QK_WIDTH = DIFF_HEADS * 2 * DIFF_DQK
Q_BLOCK = 128
ROPE_THETA = 10000.0
N_GROUPS = 4
EXPERTS_PER_GROUP = 8
N_EXPERTS = N_GROUPS * EXPERTS_PER_GROUP
TOP_K = 2
D_FF_EXPERT = D_MODEL // 2
MOE_BLOCK = 128
EPS = 1e-6
SPLIT_QK_M = 2 * MLSTM_WIDTH
SPLIT_V_M = SPLIT_QK_M + MLSTM_WIDTH
SPLIT_O_M = SPLIT_V_M + MLSTM_WIDTH
SPLIT_I_M = SPLIT_O_M + MLSTM_HEADS
SPLIT_F_M = SPLIT_I_M + MLSTM_HEADS
SPLIT_Q_A = SPLIT_F_M + DIFF_QK_WIDTH
SPLIT_K_A = SPLIT_Q_A + DIFF_QK_WIDTH
IN_SPLITS = (SPLIT_QK_M, SPLIT_V_M, SPLIT_O_M, SPLIT_I_M, SPLIT_F_M, SPLIT_Q_A, SPLIT_K_A)
IN_WIDTH = SPLIT_K_A + DIFF_WIDTH

kernel_name = 'hymba_mlstm_diffattn_hmoe_step'


def rmsnorm(x, g):
    xf = x.astype(jnp.float32)
    y = xf * lax.rsqrt(jnp.mean(xf * xf, axis=-1, keepdims=True) + EPS)
    return (y * g).astype(x.dtype)


def rope(x, pos):
    half = x.shape[-1] // 2
    inv = ROPE_THETA ** (-jnp.arange(half, dtype=jnp.float32) / half)
    ang = pos.astype(jnp.float32)[:, None] * inv[None, :]
    bshape = (1, x.shape[1]) + (1,) * (x.ndim - 3) + (half,)
    cos = jnp.cos(ang).reshape(bshape)
    sin = jnp.sin(ang).reshape(bshape)
    xf = x.astype(jnp.float32)
    x1, x2 = xf[..., :half], xf[..., half:]
    return jnp.concatenate([x1 * cos - x2 * sin, x1 * sin + x2 * cos], axis=-1).astype(x.dtype)


def mlstm_chunkwise(q, k, v, i_pre, lf, C0, n0, m0):
    B, T, H, _ = q.shape
    L = MLSTM_CHUNK if T % MLSTM_CHUNK == 0 else T
    nc = T // L
    causal = jnp.tril(jnp.ones((L, L), dtype=bool))

    def to_chunks(a):
        return a.reshape((B, nc, L) + a.shape[2:]).swapaxes(0, 1)

    def step(carry, inp):
        C, n, m = carry
        qc, kc, vc, ic, fc = inp
        b = jnp.cumsum(fc, axis=1)
        D = b[:, :, None, :] - b[:, None, :, :] + ic[:, None, :, :]
        D = jnp.where(causal[None, :, :, None], D, -jnp.inf)
        inter = b + m[:, None, :]
        mt = jnp.maximum(inter, jnp.max(D, axis=2))
        w_intra = jnp.exp(D - mt[:, :, None, :])
        w_inter = jnp.exp(inter - mt)
        qk = jnp.einsum('bthd,bshd->btsh', qc, kc) * w_intra
        num = jnp.einsum('btsh,bshv->bthv', qk, vc) + w_inter[..., None] * jnp.einsum('bthd,bhdv->bthv', qc, C)
        den = jnp.sum(qk, axis=2) + w_inter * jnp.einsum('bthd,bhd->bth', qc, n)
        h = num / jnp.maximum(jnp.abs(den), jnp.exp(-mt))[..., None]
        mL = mt[:, -1]
        g = jnp.exp(b[:, -1:, :] - b + ic - mL[:, None, :])
        decay = jnp.exp(b[:, -1] + m - mL)
        C_new = decay[..., None, None] * C + jnp.einsum('bsh,bshd,bshv->bhdv', g, kc, vc)
        n_new = decay[..., None] * n + jnp.einsum('bsh,bshd->bhd', g, kc)
        return (C_new, n_new, mL), h

    xs = (to_chunks(q), to_chunks(k), to_chunks(v), to_chunks(i_pre), to_chunks(lf))
    (C, n, m), hs = lax.scan(step, (C0, n0, m0), xs)
    h = hs.swapaxes(0, 1).reshape(B, T, H, v.shape[-1])
    return h, C, n, m


def mlstm_group(qk_raw, v, o, ig, fg, C0, n0, m0, conv0, conv_w, conv_b, b_i, b_f, out_g):
    B, T, _ = qk_raw.shape
    f32 = jnp.float32
    xpad = jnp.concatenate([conv0.astype(qk_raw.dtype), qk_raw], axis=1)
    conv = conv_b + sum(xpad[:, j:j + T] * conv_w[j] for j in range(CONV_W))
    q, k = jnp.split(jax.nn.silu(conv), 2, axis=-1)
    q = q.reshape(B, T, MLSTM_HEADS, MLSTM_DH).astype(f32)
    k = k.reshape(B, T, MLSTM_HEADS, MLSTM_DH).astype(f32) * (MLSTM_DH ** -0.5)
    v = v.reshape(B, T, MLSTM_HEADS, MLSTM_DH).astype(f32)
    i_pre = ig.astype(f32) + b_i.astype(f32)
    lf = jax.nn.log_sigmoid(fg.astype(f32) + b_f.astype(f32))
    h, C, n, m = mlstm_chunkwise(q, k, v, i_pre, lf, C0.astype(f32), n0.astype(f32), m0.astype(f32))
    h = rmsnorm(h, out_g) * jax.nn.sigmoid(o.astype(f32)).reshape(B, T, MLSTM_HEADS, MLSTM_DH)
    return h.reshape(B, T, MLSTM_WIDTH).astype(qk_raw.dtype), C, n, m, xpad[:, -(CONV_W - 1):]


def diff_attend(qb, qidx, k_all, v_all, lam):
    s = jnp.einsum('bqhcd,bkhcd->bhcqk', qb, k_all, preferred_element_type=jnp.float32) * (DIFF_DQK ** -0.5)
    mask = jnp.arange(k_all.shape[1])[None, :] <= qidx[:, None]
    s = jnp.where(mask, s, -jnp.inf)
    a = jax.nn.softmax(s, axis=-1)
    a = a[:, :, 0] - lam * a[:, :, 1]
    return jnp.einsum('bhqk,bkhv->bqhv', a.astype(v_all.dtype), v_all)


def diff_attn_group(q, k, v, pos, k_past, v_past, q_norm_g, k_norm_g, lam_q1, lam_k1, lam_q2, lam_k2, out_g, lam_init):
    B, T, _ = q.shape
    q = rope(rmsnorm(q.reshape(B, T, DIFF_HEADS, 2, DIFF_DQK), q_norm_g), pos)
    k = rope(rmsnorm(k.reshape(B, T, DIFF_HEADS, 2, DIFF_DQK), k_norm_g), pos)
    v = v.reshape(B, T, DIFF_HEADS, DIFF_DV)
    if k_past is None:
        k_all, v_all, past = k, v, 0
    else:
        k_all = jnp.concatenate([k_past.astype(k.dtype), k], axis=1)
        v_all = jnp.concatenate([v_past.astype(v.dtype), v], axis=1)
        past = k_past.shape[1]
    lam = (jnp.exp(jnp.sum(lam_q1 * lam_k1).astype(jnp.float32))
           - jnp.exp(jnp.sum(lam_q2 * lam_k2).astype(jnp.float32)) + lam_init)
    qidx = past + jnp.arange(T, dtype=jnp.int32)
    if T % Q_BLOCK == 0 and T > Q_BLOCK:
        nb = T // Q_BLOCK
        qblocks = q.reshape(B, nb, Q_BLOCK, DIFF_HEADS, 2, DIFF_DQK).swapaxes(0, 1)
        iblocks = qidx.reshape(nb, Q_BLOCK)
        ob = lax.map(lambda a: diff_attend(a[0], a[1], k_all, v_all, lam), (qblocks, iblocks))
        o = ob.swapaxes(0, 1).reshape(B, T, DIFF_HEADS, DIFF_DV)
    else:
        o = diff_attend(q, qidx, k_all, v_all, lam)
    o = rmsnorm(o, out_g) * (1.0 - lam_init)
    return o.reshape(B, T, DIFF_WIDTH), k.reshape(B, T, DIFF_HEADS, 2 * DIFF_DQK), v


def hier_moe(x, w_grp, b_grp, w_rt, b_rt, w1, w3, w2):
    N, D = x.shape
    f32 = jnp.float32
    gl = jnp.einsum('nd,dg->ng', x, w_grp).astype(f32) + b_grp.astype(f32)
    gp = jax.nn.softmax(gl, axis=-1)
    g_idx = jnp.argmax(gl, axis=-1)
    g_p = jnp.take_along_axis(gp, g_idx[:, None], axis=-1)
    el = jnp.einsum('nd,gde->nge', x, w_rt).astype(f32) + b_rt.astype(f32)[None]
    el = jnp.take_along_axis(el, g_idx[:, None, None], axis=1)[:, 0]
    top_p, top_i = lax.top_k(jax.nn.softmax(el, axis=-1), TOP_K)
    gate = g_p * top_p / jnp.sum(top_p, axis=-1, keepdims=True)
    e_flat = (g_idx[:, None] * EXPERTS_PER_GROUP + top_i).reshape(-1).astype(jnp.int32)
    t_flat = jnp.repeat(jnp.arange(N, dtype=jnp.int32), TOP_K)
    w_flat = gate.reshape(-1)
    nk = N * TOP_K
    order = jnp.argsort(e_flat, stable=True)
    se, st, sw = e_flat[order], t_flat[order], w_flat[order]
    counts = jnp.bincount(e_flat, length=N_EXPERTS)
    padded = (counts + MOE_BLOCK - 1) // MOE_BLOCK * MOE_BLOCK
    pend = jnp.cumsum(padded)
    pstart = pend - padded
    cstart = jnp.cumsum(counts) - counts
    dest = pstart[se] + jnp.arange(nk, dtype=jnp.int32) - cstart[se]
    n_blocks = -(-(nk + N_EXPERTS * (MOE_BLOCK - 1)) // MOE_BLOCK)
    rows = n_blocks * MOE_BLOCK
    tok_buf = jnp.full((rows,), N, jnp.int32).at[dest].set(st)
    w_buf = jnp.zeros((rows,), f32).at[dest].set(sw)
    blk_expert = jnp.minimum(jnp.searchsorted(pend, jnp.arange(n_blocks, dtype=jnp.int32) * MOE_BLOCK, side='right'), N_EXPERTS - 1)
    x_pad = jnp.concatenate([x, jnp.zeros((1, D), x.dtype)], axis=0)

    def run_block(args):
        tok, e = args
        xb = x_pad[tok]
        hb = jax.nn.silu(xb @ w1[e]) * (xb @ w3[e])
        return hb @ w2[e]

    out = lax.map(run_block, (tok_buf.reshape(n_blocks, MOE_BLOCK), blk_expert)).reshape(rows, D)
    out = out * w_buf[:, None].astype(out.dtype)
    return jnp.zeros((N + 1, D), out.dtype).at[tok_buf].add(out)[:N]


def decoder_layer(x, c, pos, k_past, v_past, C0, n0, m0, conv0, p, lam_init):
    B, T, D = x.shape
    mod = (jax.nn.silu(c) @ p['w_ada'] + p['b_ada'])[:, None, :]
    sh1, sc1, g1, sh2, sc2, g2 = jnp.split(mod, 6, axis=-1)
    hn = rmsnorm(x, p['norm1_g']) * (1.0 + sc1) + sh1
    proj = hn @ p['w_in']
    qk_m, v_m, o_m, i_m, f_m, q_a, k_a, v_a = jnp.split(proj, IN_SPLITS, axis=-1)
    h_m, C, n, m, conv_new = mlstm_group(qk_m, v_m, o_m, i_m, f_m, C0, n0, m0, conv0, p['conv_w'], p['conv_b'],
                                         p['b_igate'], p['b_fgate'], p['mlstm_out_g'])
    h_a, k_new, v_new = diff_attn_group(q_a, k_a, v_a, pos, k_past, v_past, p['q_norm_g'], p['k_norm_g'],
                                        p['lam_q1'], p['lam_k1'], p['lam_q2'], p['lam_k2'], p['diff_out_g'], lam_init)
    mix = jnp.concatenate([h_m, h_a.astype(h_m.dtype)], axis=-1) @ p['w_out']
    x = x + g1 * mix
    hn2 = rmsnorm(x, p['norm2_g']) * (1.0 + sc2) + sh2
    ff = hier_moe(hn2.reshape(B * T, D), p['w_grp'], p['b_grp'], p['w_rt'], p['b_rt'], p['w1'], p['w3'], p['w2'])
    x = x + g2 * ff.reshape(B, T, D)
    return x, k_new, v_new, C, n, m, conv_new


def setup_inputs(seed: int = 0) -> dict:
    key = jax.random.key(seed)
    keys = jax.random.split(key, 48)
    f32 = jnp.float32

    def nrm(i, shape, s):
        return jax.random.normal(keys[i], shape, f32) * s

    n_pages = PAST_LEN // PAGE_SIZE
    n_used = DEC_BATCH * n_pages
    n_pool = (n_used * 5 + 3) // 4
    page_table = jax.random.permutation(keys[0], n_pool)[:n_used].reshape(DEC_BATCH, n_pages).astype(jnp.int32)
    D = D_MODEL
    return {
        'x_prompt': nrm(1, (BATCH, SEQ, D), 1.0),
        'x_sample': nrm(2, (DEC_BATCH, DEC_SEQ, D), 1.0),
        'cache_k': nrm(3, (DEPTH, n_pool, PAGE_SIZE, DIFF_HEADS, 2 * DIFF_DQK), 1.0),
        'cache_v': nrm(4, (DEPTH, n_pool, PAGE_SIZE, DIFF_HEADS, DIFF_DV), 1.0),
        'page_table': page_table,
        'state_mlstm_C': nrm(5, (DEPTH, DEC_BATCH, MLSTM_HEADS, MLSTM_DH, MLSTM_DH), 0.1),
        'state_mlstm_n': nrm(6, (DEPTH, DEC_BATCH, MLSTM_HEADS, MLSTM_DH), 0.1),
        'state_mlstm_m': nrm(7, (DEPTH, DEC_BATCH, MLSTM_HEADS), 0.5),
        'state_conv': nrm(8, (DEPTH, DEC_BATCH, CONV_W - 1, 2 * MLSTM_WIDTH), 1.0),
        'c_prompt': nrm(9, (BATCH, D), 1.0),
        'c_sample': nrm(10, (DEC_BATCH, D), 1.0),
        'w_ada': nrm(11, (DEPTH, D, 6 * D), 0.5 * D ** -0.5),
        'b_ada': nrm(12, (DEPTH, 6 * D), 0.01),
        'norm1_g': 1.0 + nrm(13, (DEPTH, D), 0.01),
        'norm2_g': 1.0 + nrm(14, (DEPTH, D), 0.01),
        'w_in': nrm(15, (DEPTH, D, IN_WIDTH), D ** -0.5),
        'conv_w': nrm(16, (DEPTH, CONV_W, 2 * MLSTM_WIDTH), CONV_W ** -0.5),
        'conv_b': nrm(17, (DEPTH, 2 * MLSTM_WIDTH), 0.01),
        'b_igate': nrm(18, (DEPTH, MLSTM_HEADS), 0.1),
        'b_fgate': jnp.linspace(3.0, 6.0, MLSTM_HEADS, dtype=f32)[None, :] + nrm(19, (DEPTH, MLSTM_HEADS), 0.1),
        'mlstm_out_g': 1.0 + nrm(20, (DEPTH, MLSTM_DH), 0.01),
        'q_norm_g': 1.0 + nrm(21, (DEPTH, DIFF_DQK), 0.01),
        'k_norm_g': 1.0 + nrm(22, (DEPTH, DIFF_DQK), 0.01),
        'lam_q1': nrm(23, (DEPTH, DIFF_DQK), 0.1),
        'lam_k1': nrm(24, (DEPTH, DIFF_DQK), 0.1),
        'lam_q2': nrm(25, (DEPTH, DIFF_DQK), 0.1),
        'lam_k2': nrm(26, (DEPTH, DIFF_DQK), 0.1),
        'diff_out_g': 1.0 + nrm(27, (DEPTH, DIFF_DV), 0.01),
        'w_out': nrm(28, (DEPTH, MIX_WIDTH, D), MIX_WIDTH ** -0.5),
        'w_grp': nrm(29, (DEPTH, D, N_GROUPS), D ** -0.5),
        'b_grp': nrm(30, (DEPTH, N_GROUPS), 0.01),
        'w_rt': nrm(31, (DEPTH, N_GROUPS, D, EXPERTS_PER_GROUP), D ** -0.5),
        'b_rt': nrm(32, (DEPTH, N_GROUPS, EXPERTS_PER_GROUP), 0.01),
        'w1': nrm(33, (DEPTH, N_EXPERTS, D, D_FF_EXPERT), D ** -0.5),
        'w3': nrm(34, (DEPTH, N_EXPERTS, D, D_FF_EXPERT), D ** -0.5),
        'w2': nrm(35, (DEPTH, N_EXPERTS, D_FF_EXPERT, D), D_FF_EXPERT ** -0.5),
    }


def reference(x_prompt, x_sample, cache_k, cache_v, page_table, state_mlstm_C, state_mlstm_n, state_mlstm_m, state_conv,
              c_prompt, c_sample, w_ada, b_ada, norm1_g, norm2_g, w_in, conv_w, conv_b, b_igate, b_fgate, mlstm_out_g,
              q_norm_g, k_norm_g, lam_q1, lam_k1, lam_q2, lam_k2, diff_out_g, w_out, w_grp, b_grp, w_rt, b_rt, w1, w3, w2):
    B, T, _ = x_prompt.shape
    DB, TS, _ = x_sample.shape
    past = page_table.shape[1] * cache_k.shape[2]
    f32 = jnp.float32
    pos_p = jnp.arange(T, dtype=jnp.int32)
    pos_s = past + jnp.arange(TS, dtype=jnp.int32)
    C0p = jnp.zeros((B, MLSTM_HEADS, MLSTM_DH, MLSTM_DH), f32)
    n0p = jnp.zeros((B, MLSTM_HEADS, MLSTM_DH), f32)
    m0p = jnp.zeros((B, MLSTM_HEADS), f32)
    conv0p = jnp.zeros((B, CONV_W - 1, 2 * MLSTM_WIDTH), x_prompt.dtype)
    yp, ys = x_prompt, x_sample
    new_p, new_s = [], []
    for layer in range(DEPTH):
        p = dict(w_ada=w_ada[layer], b_ada=b_ada[layer], norm1_g=norm1_g[layer], norm2_g=norm2_g[layer],
                 w_in=w_in[layer], conv_w=conv_w[layer], conv_b=conv_b[layer], b_igate=b_igate[layer],
                 b_fgate=b_fgate[layer], mlstm_out_g=mlstm_out_g[layer], q_norm_g=q_norm_g[layer],
                 k_norm_g=k_norm_g[layer], lam_q1=lam_q1[layer], lam_k1=lam_k1[layer], lam_q2=lam_q2[layer],
                 lam_k2=lam_k2[layer], diff_out_g=diff_out_g[layer], w_out=w_out[layer], w_grp=w_grp[layer],
                 b_grp=b_grp[layer], w_rt=w_rt[layer], b_rt=b_rt[layer], w1=w1[layer], w3=w3[layer], w2=w2[layer])
        lam_init = 0.8 - 0.6 * math.exp(-0.3 * layer)
        k_past = cache_k[layer][page_table].reshape(DB, past, DIFF_HEADS, 2, DIFF_DQK)
        v_past = cache_v[layer][page_table].reshape(DB, past, DIFF_HEADS, DIFF_DV)
        yp, *st_p = decoder_layer(yp, c_prompt, pos_p, None, None, C0p, n0p, m0p, conv0p, p, lam_init)
        ys, *st_s = decoder_layer(ys, c_sample, pos_s, k_past, v_past, state_mlstm_C[layer], state_mlstm_n[layer],
                                  state_mlstm_m[layer], state_conv[layer], p, lam_init)
        new_p.append(st_p)
        new_s.append(st_s)
    k_prompt, v_prompt, C_prompt, n_prompt, m_prompt, conv_prompt = [jnp.stack(a) for a in zip(*new_p)]
    k_sample, v_sample, C_sample, n_sample, m_sample, conv_sample = [jnp.stack(a) for a in zip(*new_s)]
    return (yp, ys, k_prompt, v_prompt, C_prompt, n_prompt, m_prompt, conv_prompt,
            k_sample, v_sample, C_sample, n_sample, m_sample, conv_sample)
```

```python
import functools
import math

import jax
import jax.numpy as jnp
from jax import lax
from jax.experimental import pallas as pl
from jax.experimental.pallas import tpu as pltpu

F32 = jnp.float32
BF16 = jnp.bfloat16
I32 = jnp.int32

EPS = 1e-6
ROPE_THETA = 10000.0
NEG = -1e30

LANES = 128
SUBLANES = 8
VMEM_LIMIT_CAP = 56 * 1024 * 1024

N_HEADS = 8
HEAD_W = 128
DQK = 64
N_GROUPS = 4
EXPERTS_PER_GROUP = 8
N_EXPERTS = N_GROUPS * EXPERTS_PER_GROUP
CONV_W = 4
MOE_ROWS = 256
SAMPLE_PAD = 8
MLSTM_L = 128
PAGES_PER_STEP = 4


def _cparams(sem, vmem_bytes):
    return pltpu.CompilerParams(dimension_semantics=sem,
                                vmem_limit_bytes=int(min(max(vmem_bytes, 16 * 2**20), VMEM_LIMIT_CAP)))


def _silu(x):
    return x * jax.nn.sigmoid(x)


def _dot(a, b):
    return jnp.dot(a, b, preferred_element_type=F32)


def _dot_nt(a, b):
    return lax.dot_general(a, b, (((1,), (1,)), ((), ())), preferred_element_type=F32)


def _dot_tn(a, b):
    return lax.dot_general(a, b, (((0,), (0,)), ((), ())), preferred_element_type=F32)


def _ada_kernel(c_ref, w_ref, b_ref, o_ref):
    s = _silu(c_ref[...]).astype(BF16)
    o_ref[...] = _dot(s, w_ref[...].astype(BF16)) + b_ref[...]


def _ada(c_all, w_ada, b_ada):
    rows, d = c_all.shape
    n = w_ada.shape[1]
    tn = 1024
    return pl.pallas_call(
        _ada_kernel,
        grid=(n // tn,),
        in_specs=[pl.BlockSpec((rows, d), lambda j: (0, 0)),
                  pl.BlockSpec((d, tn), lambda j: (0, j)),
                  pl.BlockSpec((1, tn), lambda j: (0, j))],
        out_specs=pl.BlockSpec((rows, tn), lambda j: (0, j)),
        out_shape=jax.ShapeDtypeStruct((rows, n), F32),
        compiler_params=_cparams(("arbitrary",), 2 * d * tn * 4 + 3 * d * tn * 2 + 2**22),
        name="ada_mod",
    )(c_all, w_ada, b_ada.reshape(1, n))


def _inproj_kernel(x_ref, sc_ref, sh_ref, g_ref, w_ref, wg_ref, o_ref, og_ref, hn_ref):
    @pl.when(pl.program_id(1) == 0)
    def _():
        x = x_ref[...]
        y = x * lax.rsqrt(jnp.mean(x * x, axis=-1, keepdims=True) + EPS) * g_ref[...]
        hb = (y * (1.0 + sc_ref[...]) + sh_ref[...]).astype(BF16)
        hn_ref[...] = hb
        og_ref[...] = _dot(hb, wg_ref[...])

    o_ref[...] = _dot(hn_ref[...], w_ref[...])


def _inproj(x, mod_specs, mods, g, w, wg, tm):
    m, d = x.shape
    n = w.shape[1]
    tn = 1024
    sc_spec, sh_spec = mod_specs
    return pl.pallas_call(
        _inproj_kernel,
        grid=(m // tm, n // tn),
        in_specs=[pl.BlockSpec((tm, d), lambda i, j: (i, 0)), sc_spec, sh_spec,
                  pl.BlockSpec((1, d), lambda i, j: (0, 0)),
                  pl.BlockSpec((d, tn), lambda i, j: (0, j)),
                  pl.BlockSpec((d, LANES), lambda i, j: (0, 0))],
        out_specs=[pl.BlockSpec((tm, tn), lambda i, j: (i, j)),
                   pl.BlockSpec((tm, LANES), lambda i, j: (i, 0))],
        out_shape=[jax.ShapeDtypeStruct((m, n), F32), jax.ShapeDtypeStruct((m, LANES), F32)],
        scratch_shapes=[pltpu.VMEM((tm, d), BF16)],
        compiler_params=_cparams(("arbitrary", "arbitrary"),
                                 2 * tm * d * 4 + 2 * d * tn * 2 + 2 * tm * tn * 4 + 3 * tm * d * 4 + 2**22),
        name="norm1_inproj",
    )(x, mods[0], mods[1], g.reshape(1, d), w, wg)


def _qkrope_kernel(q_ref, k_ref, v_ref, cos_ref, sin_ref, qg_ref, kg_ref,
                   qo_ref, ko_ref, kb_ref, vo_ref, vb_ref):
    cos = cos_ref[...]
    sin = sin_ref[...]
    lane = lax.broadcasted_iota(I32, (1, LANES), 1)
    first_map = lane < DQK
    first_half = (lane % DQK) < (DQK // 2)

    def norm_rope(x, g):
        x2 = x * x
        s1 = jnp.sum(jnp.where(first_map, x2, 0.0), axis=-1, keepdims=True)
        s2 = jnp.sum(jnp.where(first_map, 0.0, x2), axis=-1, keepdims=True)
        inv = jnp.where(first_map, lax.rsqrt(s1 / DQK + EPS), lax.rsqrt(s2 / DQK + EPS))
        y = x * inv * g
        partner = jnp.where(first_half, pltpu.roll(y, LANES - DQK // 2, 1), pltpu.roll(y, DQK // 2, 1))
        return y * cos + partner * sin

    for h in range(N_HEADS):
        sl = slice(h * HEAD_W, (h + 1) * HEAD_W)
        qh = norm_rope(q_ref[:, sl], qg_ref[...])
        kh = norm_rope(k_ref[:, sl], kg_ref[...])
        qo_ref[:, sl] = (qh * (DQK ** -0.5)).astype(BF16)
        ko_ref[:, sl] = kh
        kb_ref[:, sl] = kh.astype(BF16)
    v = v_ref[...]
    vo_ref[...] = v
    vb_ref[...] = v.astype(BF16)


def _qkrope(proj, col0, cos_t, sin_t, qg, kg, tm):
    m = proj.shape[0]
    w = N_HEADS * HEAD_W
    nt = cos_t.shape[0] // tm
    blk = lambda c: pl.BlockSpec((tm, w), lambda i, c=c: (i, c))
    tab = pl.BlockSpec((tm, LANES), lambda i: (i % nt, 0))
    gsp = pl.BlockSpec((1, LANES), lambda i: (0, 0))
    osp = pl.BlockSpec((tm, w), lambda i: (i, 0))
    g2 = lambda g: jnp.tile(g, 2).reshape(1, LANES)
    return pl.pallas_call(
        _qkrope_kernel,
        grid=(m // tm,),
        in_specs=[blk(col0), blk(col0 + 1), blk(col0 + 2), tab, tab, gsp, gsp],
        out_specs=[osp] * 5,
        out_shape=[jax.ShapeDtypeStruct((m, w), BF16), jax.ShapeDtypeStruct((m, w), F32),
                   jax.ShapeDtypeStruct((m, w), BF16), jax.ShapeDtypeStruct((m, w), F32),
                   jax.ShapeDtypeStruct((m, w), BF16)],
        compiler_params=_cparams(("arbitrary",), 2 * tm * w * (3 * 4 + 2 * 4 + 3 * 2) + 2**22),
        name="qknorm_rope",
    )(proj, proj, proj, cos_t, sin_t, g2(qg), g2(kg))


def _rope_tables(pos):
    half = DQK // 2
    inv = ROPE_THETA ** (-jnp.arange(half, dtype=F32) / half)
    ang = pos.astype(F32)[:, None] * inv[None, :]
    cos = jnp.tile(jnp.cos(ang), (1, LANES // half))
    sin = jnp.sin(ang)
    sin = jnp.tile(jnp.concatenate([-sin, sin], axis=1), (1, LANES // DQK))
    return cos, sin


def _lam_value(l1, l2, l3, l4, lam_init):
    a = jnp.sum(l1[...] * l2[...], axis=-1, keepdims=True)
    b = jnp.sum(l3[...] * l4[...], axis=-1, keepdims=True)
    return jnp.exp(a) - jnp.exp(b) + lam_init


def _diff_finish(acc1, l1, acc2, l2, lam, og, lam_init):
    o = acc1 / l1 - lam * (acc2 / l2)
    o = o * lax.rsqrt(jnp.mean(o * o, axis=-1, keepdims=True) + EPS) * og
    return o * (1.0 - lam_init)


def _dattn_kernel(q_ref, k_ref, v_ref, l1_ref, l2_ref, l3_ref, l4_ref, og_ref, o_ref, *, tq, lam_init):
    qi = pl.program_id(2)
    q = q_ref[...]
    lane = lax.broadcasted_iota(I32, (1, LANES), 1)
    zero = jnp.zeros_like(q)
    qa = jnp.where(lane < DQK, q, zero)
    qb = jnp.where(lane < DQK, zero, q)

    def step(k, v, carry, mask):
        out = []
        for qm, (m, l, acc) in zip((qa, qb), carry):
            s = _dot_nt(qm, k)
            if mask is not None:
                s = jnp.where(mask, s, NEG)
            m_new = jnp.maximum(m, jnp.max(s, axis=-1, keepdims=True))
            p = jnp.exp(s - m_new)
            alpha = jnp.exp(m - m_new)
            l_new = alpha * l + jnp.sum(p, axis=-1, keepdims=True)
            acc_new = alpha * acc + _dot(p.astype(BF16), v)
            out.append((m_new, l_new, acc_new))
        return tuple(out)

    def body(ki, carry):
        off = pl.multiple_of(ki * tq, tq)
        return step(k_ref[pl.ds(off, tq), :], v_ref[pl.ds(off, tq), :], carry, None)

    init1 = (jnp.full((tq, 1), NEG, F32), jnp.zeros((tq, 1), F32), jnp.zeros((tq, HEAD_W), F32))
    carry = lax.fori_loop(0, qi, body, (init1, init1))
    off = pl.multiple_of(qi * tq, tq)
    row = lax.broadcasted_iota(I32, (tq, tq), 0)
    col = lax.broadcasted_iota(I32, (tq, tq), 1)
    (_, l1, a1), (_, l2, a2) = step(k_ref[pl.ds(off, tq), :], v_ref[pl.ds(off, tq), :], carry, col <= row)
    lam = _lam_value(l1_ref, l2_ref, l3_ref, l4_ref, lam_init)
    o_ref[...] = _diff_finish(a1, l1, a2, l2, lam, og_ref[...], lam_init).astype(BF16)


def _dattn_prompt(q, k, v, lams, og, nb, t, lam_init):
    tq = 256
    nq = t // tq
    lsp = pl.BlockSpec((1, DQK), lambda b, h, i: (0, 0))
    return pl.pallas_call(
        functools.partial(_dattn_kernel, tq=tq, lam_init=lam_init),
        grid=(nb, N_HEADS, nq),
        in_specs=[pl.BlockSpec((tq, HEAD_W), lambda b, h, i: (b * nq + i, h)),
                  pl.BlockSpec((t, HEAD_W), lambda b, h, i: (b, h)),
                  pl.BlockSpec((t, HEAD_W), lambda b, h, i: (b, h)),
                  lsp, lsp, lsp, lsp,
                  pl.BlockSpec((1, HEAD_W), lambda b, h, i: (0, 0))],
        out_specs=pl.BlockSpec((tq, HEAD_W), lambda b, h, i: (b * nq + i, h)),
        out_shape=jax.ShapeDtypeStruct(q.shape, BF16),
        compiler_params=_cparams(("arbitrary",) * 3, 32 * 2**20),
        name="diff_attn_prompt",
    )(q, k, v, *[x.reshape(1, DQK) for x in lams], og.reshape(1, HEAD_W))


def _dattn_paged_kernel(pt_ref, q_ref, *refs, pps, n_new, lam_init):
    k_refs = refs[:pps]
    v_refs = refs[pps:2 * pps]
    kn_ref, vn_ref, l1_ref, l2_ref, l3_ref, l4_ref, og_ref, o_ref, m_sc, l_sc, acc_sc = refs[2 * pps:]
    step_id = pl.program_id(1)
    nsteps = pl.num_programs(1)
    rows = N_HEADS * SUBLANES
    q = q_ref[...]
    lane = lax.broadcasted_iota(I32, (SUBLANES, LANES), 1)
    lane_head = lane % N_HEADS

    @pl.when(step_id == 0)
    def _():
        m_sc[...] = jnp.full(m_sc.shape, NEG, F32)
        l_sc[...] = jnp.zeros(l_sc.shape, F32)
        acc_sc[...] = jnp.zeros(acc_sc.shape, F32)

    def lane_class_reduce(x, op):
        for sh in (8, 16, 32, 64):
            x = op(x, pltpu.roll(x, sh, 1))
        return x

    def head_rows(x):
        cols = [jnp.sum(jnp.where(lane == h, x, 0.0), axis=-1, keepdims=True) for h in range(N_HEADS)]
        return jnp.concatenate(cols, axis=0)

    def attend(ks, vs, valid):
        s_parts = []
        for kp in ks:
            sf = _dot_nt(q, kp.astype(BF16))
            npart = sf.shape[1] // LANES
            for c in range(npart):
                blk = sf[:, c * LANES:(c + 1) * LANES]
                s = jnp.zeros((SUBLANES, LANES), F32)
                for h in range(N_HEADS):
                    s = jnp.where(lane_head == h, blk[h * SUBLANES:(h + 1) * SUBLANES, :], s)
                s_parts.append(s)
        if valid is not None:
            s_parts = [jnp.where(vm, s, NEG) for s, vm in zip(s_parts, valid)]
        m_old = m_sc[...]
        m_cur = functools.reduce(jnp.maximum, s_parts)
        m_new = jnp.maximum(m_old, lane_class_reduce(m_cur, jnp.maximum))
        alpha = jnp.exp(m_old - m_new)
        p_parts = [jnp.exp(s - m_new) for s in s_parts]
        l_sc[...] = alpha * l_sc[...] + functools.reduce(jnp.add, p_parts)
        m_sc[...] = m_new
        pv = jnp.zeros((rows, HEAD_W), F32)
        per_page = len(p_parts) // len(vs)
        for pi, vp in enumerate(vs):
            pe = []
            for c in range(per_page):
                p = p_parts[pi * per_page + c]
                pe.append(jnp.concatenate(
                    [jnp.where(lane_head == h, p, 0.0) for h in range(N_HEADS)], axis=0))
            pexp = jnp.concatenate(pe, axis=1).astype(BF16)
            pv = pv + _dot(pexp, vp.astype(BF16))
        acc_sc[...] = head_rows(alpha) * acc_sc[...] + pv

    attend([r[...] for r in k_refs], [r[...] for r in v_refs], None)

    @pl.when(step_id == nsteps - 1)
    def _():
        qrow = lax.broadcasted_iota(I32, (SUBLANES, LANES), 0) % n_new
        valid = []
        for c in range(kn_ref.shape[0] // LANES):
            tok = (lane + c * LANES) // N_HEADS
            valid.append(tok <= qrow)
        attend([kn_ref[...]], [vn_ref[...]], valid)
        l_tot = head_rows(lane_class_reduce(l_sc[...], jnp.add))
        acc = acc_sc[...]
        lam = _lam_value(l1_ref, l2_ref, l3_ref, l4_ref, lam_init)
        outs = []
        for h in range(N_HEADS):
            r0 = h * SUBLANES
            a1, a2 = acc[r0:r0 + n_new], acc[r0 + n_new:r0 + 2 * n_new]
            s1, s2 = l_tot[r0:r0 + n_new], l_tot[r0 + n_new:r0 + 2 * n_new]
            o = _diff_finish(a1, s1, a2, s2, lam, og_ref[...], lam_init)
            outs.append(jnp.concatenate([o, jnp.zeros((SUBLANES - n_new, HEAD_W), F32)], axis=0))
        o_ref[...] = jnp.concatenate(outs, axis=1).astype(BF16)


def _dattn_paged(qall, cache_k, cache_v, page_table, k_new, v_new, lams, og, n_new, lam_init):
    nb, n_pages = page_table.shape
    pps = PAGES_PER_STEP
    prow = cache_k.shape[1]
    nsteps = n_pages // pps
    cspec = lambda j: pl.BlockSpec((None, prow, HEAD_W), lambda b, s, pt, j=j: (pt[b * n_pages + s * pps + j], 0, 0))
    lsp = pl.BlockSpec((1, DQK), lambda b, s, pt: (0, 0))
    nspec = pl.BlockSpec((None, k_new.shape[1], HEAD_W), lambda b, s, pt: (b, 0, 0))
    grid_spec = pltpu.PrefetchScalarGridSpec(
        num_scalar_prefetch=1,
        grid=(nb, nsteps),
        in_specs=[pl.BlockSpec((None, N_HEADS * SUBLANES, HEAD_W), lambda b, s, pt: (b, 0, 0))]
                 + [cspec(j) for j in range(pps)] + [cspec(j) for j in range(pps)]
                 + [nspec, nspec, lsp, lsp, lsp, lsp, pl.BlockSpec((1, HEAD_W), lambda b, s, pt: (0, 0))],
        out_specs=pl.BlockSpec((SUBLANES, N_HEADS * HEAD_W), lambda b, s, pt: (b, 0)),
        scratch_shapes=[pltpu.VMEM((SUBLANES, LANES), F32), pltpu.VMEM((SUBLANES, LANES), F32),
                        pltpu.VMEM((N_HEADS * SUBLANES, HEAD_W), F32)])
    return pl.pallas_call(
        functools.partial(_dattn_paged_kernel, pps=pps, n_new=n_new, lam_init=lam_init),
        grid_spec=grid_spec,
        out_shape=jax.ShapeDtypeStruct((nb * SUBLANES, N_HEADS * HEAD_W), BF16),
        compiler_params=_cparams(("arbitrary", "arbitrary"), 4 * pps * prow * HEAD_W * 4 + 24 * 2**20),
        name="diff_attn_paged",
    )(page_table.reshape(-1), qall, *([cache_k] * pps), *([cache_v] * pps), k_new, v_new,
      *[x.reshape(1, DQK) for x in lams], og.reshape(1, HEAD_W))


def _split3(a):
    hi = a.astype(BF16)
    r1 = a - hi.astype(F32)
    mid = r1.astype(BF16)
    lo = (r1 - mid.astype(F32)).astype(BF16)
    return hi, mid, lo


def _mlstm_kernel(qk_ref, v_ref, o_ref, gt_ref, cw_ref, cb_ref, gb_ref, og_ref,
                  C0_ref, n0_ref, m0_ref, cv0_ref,
                  h_ref, C_ref, n_ref, m_ref, cv_ref, xbuf, *, L, rows_in, n_valid):
    c = pl.program_id(1)
    width = N_HEADS * HEAD_W

    @pl.when(c == 0)
    def _():
        C_ref[...] = C0_ref[...]
        n_ref[...] = n0_ref[...]
        m_ref[...] = m0_ref[...]
        xbuf[0:SUBLANES, :] = cv0_ref[...]

    def padded(ref):
        x = ref[...]
        if rows_in < L:
            x = jnp.concatenate([x, jnp.zeros((L - rows_in, x.shape[1]), x.dtype)], axis=0)
        return x

    xbuf[SUBLANES:SUBLANES + L, :] = padded(qk_ref)
    conv = cb_ref[...]
    for j in range(CONV_W):
        lo = SUBLANES - (CONV_W - 1) + j
        conv = conv + xbuf[lo:lo + L, :] * cw_ref[j:j + 1, :]
    cv_ref[...] = xbuf[n_valid:n_valid + SUBLANES, :]
    xbuf[0:SUBLANES, :] = xbuf[L:L + SUBLANES, :]
    conv = _silu(conv)
    q_all = conv[:, :width].astype(BF16)
    k_all = conv[:, width:] * (HEAD_W ** -0.5)
    v_all = padded(v_ref).astype(BF16)
    o_all = padded(o_ref)

    g = padded(gt_ref) + gb_ref[...]
    lane = lax.broadcasted_iota(I32, (L, LANES), 1)
    rowi = lax.broadcasted_iota(I32, (L, LANES), 0)
    is_f = (lane >= N_HEADS) & (lane < 2 * N_HEADS)
    logsig = jnp.minimum(g, 0.0) - jnp.log1p(jnp.exp(-jnp.abs(g)))
    a = jnp.where(is_f, logsig, g)
    if n_valid < L:
        a = jnp.where(rowi < n_valid, a, jnp.where(is_f, 0.0, NEG))
    tri = (lax.broadcasted_iota(I32, (L, L), 1) <= lax.broadcasted_iota(I32, (L, L), 0))
    tri_b = tri.astype(BF16)
    cum = functools.reduce(jnp.add, [_dot(tri_b, part) for part in _split3(jnp.where(is_f, a, 0.0))])
    a_t = a.T
    cum_t = cum.T

    for h in range(N_HEADS):
        sl = slice(h * HEAD_W, (h + 1) * HEAD_W)
        b_col = cum[:, N_HEADS + h:N_HEADS + h + 1]
        b_row = cum_t[N_HEADS + h:N_HEADS + h + 1, :]
        i_col = a[:, h:h + 1]
        i_row = a_t[h:h + 1, :]
        m_prev = m_ref[h:h + 1, 0:1]
        d = jnp.where(tri, (b_col - b_row) + i_row, NEG)
        inter = b_col + m_prev
        mt = jnp.maximum(inter, jnp.max(d, axis=-1, keepdims=True))
        w_intra = jnp.exp(d - mt)
        w_inter = jnp.exp(inter - mt)
        qh, kh, vh = q_all[:, sl], k_all[:, sl], v_all[:, sl]
        C_h = C_ref[h]
        n_h = n_ref[h:h + 1, :]
        s = _dot_nt(qh, kh.astype(BF16)) * w_intra
        num = _dot(s.astype(BF16), vh) + w_inter * _dot(qh, C_h.astype(BF16))
        den = (jnp.sum(s, axis=-1, keepdims=True)
               + w_inter * jnp.sum(qh.astype(F32) * n_h, axis=-1, keepdims=True))
        hh = num / jnp.maximum(jnp.abs(den), jnp.exp(-mt))
        m_last = mt[L - 1:L, :]
        b_last = b_col[L - 1:L, :]
        gk = jnp.exp((b_last - b_col) + i_col - m_last) * kh
        decay = jnp.exp(b_last + m_prev - m_last)
        C_ref[h] = decay * C_h + _dot_tn(gk.astype(BF16), vh)
        n_ref[h:h + 1, :] = decay * n_h + jnp.sum(gk, axis=0, keepdims=True)
        m_ref[h:h + 1, :] = jnp.broadcast_to(m_last, (1, LANES))
        hn = hh * lax.rsqrt(jnp.mean(hh * hh, axis=-1, keepdims=True) + EPS) * og_ref[...]
        out = hn * jax.nn.sigmoid(o_all[:, sl])
        h_ref[:, sl] = out[:rows_in].astype(BF16)


def _mlstm(proj, gates, conv_w, conv_b, gate_b, og, C0, n0, m0, cv0, nb, rows_in, nchunks, n_valid):
    L = MLSTM_L
    w = N_HEADS * HEAD_W
    row = lambda cb: (lambda b, c: (b * nchunks + c, cb))
    st3 = lambda b, c: (b, 0, 0)
    kern = functools.partial(_mlstm_kernel, L=L, rows_in=rows_in, n_valid=n_valid)
    return pl.pallas_call(
        kern,
        grid=(nb, nchunks),
        in_specs=[pl.BlockSpec((rows_in, 2 * w), lambda b, c: (b * nchunks + c, 0)),
                  pl.BlockSpec((rows_in, w), row(2)), pl.BlockSpec((rows_in, w), row(3)),
                  pl.BlockSpec((rows_in, LANES), lambda b, c: (b * nchunks + c, 0)),
                  pl.BlockSpec((SUBLANES, 2 * w), lambda b, c: (0, 0)),
                  pl.BlockSpec((1, 2 * w), lambda b, c: (0, 0)),
                  pl.BlockSpec((1, LANES), lambda b, c: (0, 0)),
                  pl.BlockSpec((1, HEAD_W), lambda b, c: (0, 0)),
                  pl.BlockSpec((None, N_HEADS, HEAD_W, HEAD_W), lambda b, c: (b, 0, 0, 0)),
                  pl.BlockSpec((None, N_HEADS, HEAD_W), st3),
                  pl.BlockSpec((None, N_HEADS, LANES), st3),
                  pl.BlockSpec((None, SUBLANES, 2 * w), st3)],
        out_specs=[pl.BlockSpec((rows_in, w), lambda b, c: (b * nchunks + c, 0)),
                   pl.BlockSpec((None, N_HEADS, HEAD_W, HEAD_W), lambda b, c: (b, 0, 0, 0)),
                   pl.BlockSpec((None, N_HEADS, HEAD_W), st3),
                   pl.BlockSpec((None, N_HEADS, LANES), st3),
                   pl.BlockSpec((None, SUBLANES, 2 * w), st3)],
        out_shape=[jax.ShapeDtypeStruct((nb * nchunks * rows_in, w), BF16),
                   jax.ShapeDtypeStruct((nb, N_HEADS, HEAD_W, HEAD_W), F32),
                   jax.ShapeDtypeStruct((nb, N_HEADS, HEAD_W), F32),
                   jax.ShapeDtypeStruct((nb, N_HEADS, LANES), F32),
                   jax.ShapeDtypeStruct((nb, SUBLANES, 2 * w), F32)],
        scratch_shapes=[pltpu.VMEM((SUBLANES + L, 2 * w), F32)],
        compiler_params=_cparams(("arbitrary", "arbitrary"), 40 * 2**20),
        name="mlstm_chunkwise",
    )(proj, proj, proj, gates, conv_w, conv_b, gate_b, og, C0, n0, m0, cv0)


def _outproj_kernel(hm_ref, ha_ref, x_ref, g1_ref, sc_ref, sh_ref, ng_ref, wo_ref, wr_ref, br_ref,
                    x1_ref, hn_ref, ids_ref, gates_ref):
    half = hm_ref.shape[1]
    mix = _dot(hm_ref[...], wo_ref[0:half, :]) + _dot(ha_ref[...], wo_ref[half:2 * half, :])
    x1 = x_ref[...] + g1_ref[...] * mix
    x1_ref[...] = x1
    y = x1 * lax.rsqrt(jnp.mean(x1 * x1, axis=-1, keepdims=True) + EPS) * ng_ref[...]
    hn = y * (1.0 + sc_ref[...]) + sh_ref[...]
    hn_ref[...] = hn
    hi = hn.astype(BF16)
    lo = (hn - hi.astype(F32)).astype(BF16)
    p1 = _dot(hi, wr_ref[...])
    logits = p1[:, :LANES] + p1[:, LANES:] + _dot(lo, wr_ref[:, :LANES]) + br_ref[...]

    lane = lax.broadcasted_iota(I32, logits.shape, 1).astype(F32)
    big = 1000.0
    is_g = lane < N_GROUPS
    gl = jnp.where(is_g, logits, NEG)
    gmax = jnp.max(gl, axis=-1, keepdims=True)
    g_idx = jnp.min(jnp.where(gl == gmax, lane, big), axis=-1, keepdims=True)
    g_p = 1.0 / jnp.sum(jnp.where(is_g, jnp.exp(gl - gmax), 0.0), axis=-1, keepdims=True)
    e_lo = N_GROUPS + EXPERTS_PER_GROUP * g_idx
    in_grp = (lane >= e_lo) & (lane < e_lo + EXPERTS_PER_GROUP)
    el = jnp.where(in_grp, logits, NEG)
    ex = jnp.where(in_grp, jnp.exp(el - jnp.max(el, axis=-1, keepdims=True)), 0.0)
    p = jnp.where(in_grp, ex / jnp.sum(ex, axis=-1, keepdims=True), -1.0)
    top1 = jnp.max(p, axis=-1, keepdims=True)
    idx1 = jnp.min(jnp.where(p == top1, lane, big), axis=-1, keepdims=True)
    p2 = jnp.where(lane == idx1, -1.0, p)
    top2 = jnp.max(p2, axis=-1, keepdims=True)
    idx2 = jnp.min(jnp.where(p2 == top2, lane, big), axis=-1, keepdims=True)
    tsum = top1 + top2
    ids_ref[...] = jnp.where(lane == 0.0, idx1 - N_GROUPS,
                             jnp.where(lane == 1.0, idx2 - N_GROUPS, -1.0)).astype(I32)
    gates_ref[...] = jnp.where(lane == 0.0, g_p * top1 / tsum, jnp.where(lane == 1.0, g_p * top2 / tsum, 0.0))


def _outproj(hm, ha, x, mod_specs, mods, ng, wo, wr, br, tm):
    m, d = x.shape
    half = hm.shape[1]
    row = lambda wdt: pl.BlockSpec((tm, wdt), lambda i: (i, 0))
    const = lambda shp: pl.BlockSpec(shp, lambda i: (0, 0))
    return pl.pallas_call(
        _outproj_kernel,
        grid=(m // tm,),
        in_specs=[row(half), row(half), row(d), *mod_specs, const((1, d)), const((d, d)),
                  const((d, 2 * LANES)), const((1, LANES))],
        out_specs=[row(d), row(d), row(LANES), row(LANES)],
        out_shape=[jax.ShapeDtypeStruct((m, d), F32), jax.ShapeDtypeStruct((m, d), F32),
                   jax.ShapeDtypeStruct((m, LANES), I32), jax.ShapeDtypeStruct((m, LANES), F32)],
        compiler_params=_cparams(("arbitrary",), 2 * d * d * 2 + 8 * tm * d * 4 + 8 * 2**20),
        name="outproj_norm2_route",
    )(hm, ha, x, *mods, ng.reshape(1, d), wo, wr, br)


def _rank_kernel(ids_ref, rank_ref, cnt_ref, carry):
    @pl.when(pl.program_id(0) == 0)
    def _():
        carry[...] = jnp.zeros(carry.shape, F32)

    ids = ids_ref[...]
    tm = ids.shape[0]
    lane = lax.broadcasted_iota(I32, ids.shape, 1)
    o0 = lane == ids[:, 0:1]
    o1 = lane == ids[:, 1:2]
    onehot = jnp.where(o0 | o1, 1.0, 0.0)
    strict = lax.broadcasted_iota(I32, (tm, tm), 1) < lax.broadcasted_iota(I32, (tm, tm), 0)
    before = _dot(strict.astype(BF16), onehot.astype(BF16)) + carry[0:1, :]
    r0 = jnp.sum(jnp.where(o0, before, 0.0), axis=-1, keepdims=True)
    r1 = jnp.sum(jnp.where(o1, before, 0.0), axis=-1, keepdims=True)
    rank_ref[...] = jnp.where(lane == 0, r0, jnp.where(lane == 1, r1, 0.0)).astype(I32)
    carry[...] = carry[...] + jnp.sum(onehot, axis=0, keepdims=True)
    cnt_ref[...] = carry[...]


def _ranks(ids, tm):
    m = ids.shape[0]
    return pl.pallas_call(
        _rank_kernel,
        grid=(m // tm,),
        in_specs=[pl.BlockSpec((tm, LANES), lambda i: (i, 0))],
        out_specs=[pl.BlockSpec((tm, LANES), lambda i: (i, 0)), pl.BlockSpec((SUBLANES, LANES), lambda i: (0, 0))],
        out_shape=[jax.ShapeDtypeStruct((m, LANES), I32), jax.ShapeDtypeStruct((SUBLANES, LANES), F32)],
        scratch_shapes=[pltpu.VMEM((SUBLANES, LANES), F32)],
        compiler_params=_cparams(("arbitrary",), 16 * 2**20),
        name="moe_ranks",
    )(ids)


DISPATCH_CHUNK = 128


def _dispatch_kernel(dest_ref, hp_ref, hs_ref, xin_ref, xs_ref, sem, *, n_p, n_s):
    del xin_ref
    ch = DISPATCH_CHUNK

    def issue(src_ref, base, t0, cnt):
        def body(t, _):
            for k in range(2):
                d = dest_ref[2 * (base + t0 + t) + k]
                pltpu.make_async_copy(src_ref.at[pl.ds(t0 + t, 1)], xs_ref.at[pl.ds(d, 1)], sem).start()
            return 0
        lax.fori_loop(0, cnt, body, 0)

    def drain(cnt):
        pltpu.make_async_copy(hp_ref.at[pl.ds(0, 2 * cnt)], xs_ref.at[pl.ds(0, 2 * cnt)], sem).wait()

    def run(src_ref, base, n):
        nch = n // ch
        if nch > 0:
            issue(src_ref, base, 0, ch)

            def body(c, _):
                issue(src_ref, base, c * ch, ch)
                drain(ch)
                return 0
            lax.fori_loop(1, nch, body, 0)
            drain(ch)
        rem = n - nch * ch
        if rem:
            issue(src_ref, base, nch * ch, rem)
            drain(rem)

    run(hp_ref, 0, n_p)
    run(hs_ref, n_p, n_s)


def _dispatch(dest_flat, hn_p, hn_s, rows):
    d = hn_p.shape[1]
    xs0 = jnp.zeros((rows, d), F32)
    any_spec = pl.BlockSpec(memory_space=pl.ANY)
    grid_spec = pltpu.PrefetchScalarGridSpec(
        num_scalar_prefetch=1, grid=(1,),
        in_specs=[any_spec, any_spec, any_spec], out_specs=any_spec,
        scratch_shapes=[pltpu.SemaphoreType.DMA])
    return pl.pallas_call(
        functools.partial(_dispatch_kernel, n_p=hn_p.shape[0], n_s=hn_s.shape[0]),
        grid_spec=grid_spec,
        out_shape=jax.ShapeDtypeStruct((rows, d), F32),
        input_output_aliases={3: 0},
        compiler_params=_cparams(("arbitrary",), 16 * 2**20),
        name="moe_dispatch",
    )(dest_flat, hn_p, hn_s, xs0)


def _expert_up_kernel(be_ref, bf_ref, nv_ref, x_ref, w1_ref, w3_ref, h_ref, w1b, w3b):
    b = pl.program_id(0)

    @pl.when(bf_ref[b] == 1)
    def _():
        w1b[...] = w1_ref[...].astype(BF16)
        w3b[...] = w3_ref[...].astype(BF16)

    @pl.when(b < nv_ref[0])
    def _():
        x = x_ref[...].astype(BF16)
        h_ref[...] = (_silu(_dot(x, w1b[...])) * _dot(x, w3b[...])).astype(BF16)

    @pl.when(b >= nv_ref[0])
    def _():
        h_ref[...] = jnp.zeros(h_ref.shape, BF16)


def _expert_up(blk_e, blk_first, n_valid, xs, w1, w3):
    rows, d = xs.shape
    f = w1.shape[-1]
    nb = rows // MOE_ROWS
    wspec = pl.BlockSpec((None, None, d, f), lambda b, be, bf, nv: (0, be[b], 0, 0))
    grid_spec = pltpu.PrefetchScalarGridSpec(
        num_scalar_prefetch=3, grid=(nb,),
        in_specs=[pl.BlockSpec((MOE_ROWS, d), lambda b, be, bf, nv: (b, 0)), wspec, wspec],
        out_specs=pl.BlockSpec((MOE_ROWS, f), lambda b, be, bf, nv: (b, 0)),
        scratch_shapes=[pltpu.VMEM((d, f), BF16), pltpu.VMEM((d, f), BF16)])
    return pl.pallas_call(
        _expert_up_kernel,
        grid_spec=grid_spec,
        out_shape=jax.ShapeDtypeStruct((rows, f), BF16),
        compiler_params=_cparams(("arbitrary",), VMEM_LIMIT_CAP),
        name="moe_expert_up",
    )(blk_e, blk_first, n_valid, xs, w1, w3)


def _expert_down_kernel(be_ref, bf_ref, nv_ref, h_ref, w2_ref, y_ref, w2b):
    b = pl.program_id(0)

    @pl.when(bf_ref[b] == 1)
    def _():
        w2b[...] = w2_ref[...].astype(BF16)

    @pl.when(b < nv_ref[0])
    def _():
        y_ref[...] = _dot(h_ref[...], w2b[...])

    @pl.when(b >= nv_ref[0])
    def _():
        y_ref[...] = jnp.zeros(y_ref.shape, F32)


def _expert_down(blk_e, blk_first, n_valid, hs, w2):
    rows, f = hs.shape
    d = w2.shape[-1]
    nb = rows // MOE_ROWS
    grid_spec = pltpu.PrefetchScalarGridSpec(
        num_scalar_prefetch=3, grid=(nb,),
        in_specs=[pl.BlockSpec((MOE_ROWS, f), lambda b, be, bf, nv: (b, 0)),
                  pl.BlockSpec((None, None, f, d), lambda b, be, bf, nv: (0, be[b], 0, 0))],
        out_specs=pl.BlockSpec((MOE_ROWS, d), lambda b, be, bf, nv: (b, 0)),
        scratch_shapes=[pltpu.VMEM((f, d), BF16)])
    return pl.pallas_call(
        _expert_down_kernel,
        grid_spec=grid_spec,
        out_shape=jax.ShapeDtypeStruct((rows, d), F32),
        compiler_params=_cparams(("arbitrary",), 40 * 2**20),
        name="moe_expert_down",
    )(blk_e, blk_first, n_valid, hs, w2)


def _combine_kernel(dest_ref, x1_ref, g2_ref, gates_ref, y_ref, o_ref, ybuf, sem, *, tm, base):
    i = pl.program_id(0)
    t0 = base + i * tm

    def body(t, _):
        for k in range(2):
            d = dest_ref[2 * (t0 + t) + k]
            pltpu.make_async_copy(y_ref.at[pl.ds(d, 1)], ybuf.at[k, pl.ds(t, 1)], sem).start()
        return 0
    lax.fori_loop(0, tm, body, 0)
    for k in range(2):
        pltpu.make_async_copy(y_ref.at[pl.ds(0, tm)], ybuf.at[k], sem).wait()
    gates = gates_ref[...]
    ff = gates[:, 0:1] * ybuf[0] + gates[:, 1:2] * ybuf[1]
    o_ref[...] = x1_ref[...] + g2_ref[...] * ff


def _combine(dest_flat, x1, g2_spec, g2, gates, y, tm, base):
    m, d = x1.shape
    grid_spec = pltpu.PrefetchScalarGridSpec(
        num_scalar_prefetch=1, grid=(m // tm,),
        in_specs=[pl.BlockSpec((tm, d), lambda i, dr: (i, 0)), g2_spec,
                  pl.BlockSpec((tm, LANES), lambda i, dr: (i, 0)),
                  pl.BlockSpec(memory_space=pl.ANY)],
        out_specs=pl.BlockSpec((tm, d), lambda i, dr: (i, 0)),
        scratch_shapes=[pltpu.VMEM((2, tm, d), F32), pltpu.SemaphoreType.DMA])
    return pl.pallas_call(
        functools.partial(_combine_kernel, tm=tm, base=base),
        grid_spec=grid_spec,
        out_shape=jax.ShapeDtypeStruct((m, d), F32),
        compiler_params=_cparams(("arbitrary",), 6 * tm * d * 4 + 8 * 2**20),
        name="moe_combine",
    )(dest_flat, x1, g2, gates, y)


def _split_hi_lo(w):
    hi = w.astype(BF16)
    lo = (w - hi.astype(F32)).astype(BF16)
    return jnp.concatenate([hi, lo], axis=1)


def kernel(x_prompt, x_sample, cache_k, cache_v, page_table, state_mlstm_C, state_mlstm_n, state_mlstm_m, state_conv,
           c_prompt, c_sample, w_ada, b_ada, norm1_g, norm2_g, w_in, conv_w, conv_b, b_igate, b_fgate, mlstm_out_g,
           q_norm_g, k_norm_g, lam_q1, lam_k1, lam_q2, lam_k2, diff_out_g, w_out, w_grp, b_grp, w_rt, b_rt, w1, w3, w2):
    B, T, D = x_prompt.shape
    DB, TS, _ = x_sample.shape
    depth = w_ada.shape[0]
    assert depth == 1, "single-layer trunk"
    n_pages, page = page_table.shape[1], cache_k.shape[2]
    past = n_pages * page
    W = N_HEADS * HEAD_W
    lam_init = 0.8 - 0.6 * math.exp(-0.3 * 0)
    NP, NS = B * T, DB * TS
    SP = SAMPLE_PAD

    c_all = jnp.concatenate([c_prompt, c_sample, jnp.zeros((16 - B - DB, D), F32)], axis=0)
    wi = w_in[0]
    gate_lo = 4 * W
    w_main = jnp.concatenate([wi[:, :gate_lo], wi[:, gate_lo + 2 * N_HEADS:]], axis=1).astype(BF16)
    w_gate = jnp.pad(wi[:, gate_lo:gate_lo + 2 * N_HEADS], ((0, 0), (0, LANES - 2 * N_HEADS))).astype(BF16)
    gate_b = jnp.pad(jnp.concatenate([b_igate[0], b_fgate[0]]), (0, LANES - 2 * N_HEADS)).reshape(1, LANES)
    cw = jnp.pad(conv_w[0], ((0, SUBLANES - CONV_W), (0, 0)))
    cb = conv_b[0].reshape(1, 2 * W)
    og_m = mlstm_out_g[0].reshape(1, HEAD_W)
    wo = w_out[0].astype(BF16)
    w_route = jnp.concatenate([w_grp[0], w_rt[0].transpose(1, 0, 2).reshape(D, N_EXPERTS)], axis=1)
    w_route = _split_hi_lo(jnp.pad(w_route, ((0, 0), (0, LANES - N_GROUPS - N_EXPERTS))))
    b_route = jnp.pad(jnp.concatenate([b_grp[0], b_rt[0].reshape(-1)]), (0, LANES - N_GROUPS - N_EXPERTS)).reshape(1, LANES)
    lams = (lam_q1[0], lam_k1[0], lam_q2[0], lam_k2[0])

    mod = _ada(c_all, w_ada[0], b_ada[0])
    mod3 = mod.reshape(16, 1, 6 * D)
    mod_s = jnp.repeat(mod[B:B + DB], SP, axis=0)

    def p_mod(chunk, tm):
        return pl.BlockSpec((None, 1, D), lambda i, *_: ((i * tm) // T, 0, chunk))

    def s_mod(chunk, tm):
        return pl.BlockSpec((tm, D), lambda i, *_: (i, chunk))

    xp = x_prompt.reshape(NP, D)
    tm_in = 512
    proj_p, gates_p = _inproj(xp, (p_mod(1, tm_in), p_mod(0, tm_in)), (mod3, mod3), norm1_g[0], w_main, w_gate, tm_in)
    cos_p, sin_p = _rope_tables(jnp.arange(T, dtype=I32))
    q_p, knew_p, kb_p, vnew_p, vb_p = _qkrope(proj_p, 4, cos_p, sin_p, q_norm_g[0], k_norm_g[0], 256)
    ha_p = _dattn_prompt(q_p, kb_p, vb_p, lams, diff_out_g[0], B, T, lam_init)
    zeros = lambda *s: jnp.zeros(s, F32)
    hm_p, C_p, n_p, m_p, cv_p = _mlstm(
        proj_p, gates_p, cw, cb, gate_b, og_m,
        zeros(B, N_HEADS, HEAD_W, HEAD_W), zeros(B, N_HEADS, HEAD_W), zeros(B, N_HEADS, LANES),
        zeros(B, SUBLANES, 2 * W), B, MLSTM_L, T // MLSTM_L, MLSTM_L)
    tm_o = 256
    x1_p, hn2_p, ids_p, gts_p = _outproj(
        hm_p, ha_p, xp, (p_mod(2, tm_o), p_mod(4, tm_o), p_mod(3, tm_o)), (mod3, mod3, mod3),
        norm2_g[0], wo, w_route, b_route, tm_o)

    MS = DB * SP
    xs_pad = jnp.pad(x_sample, ((0, 0), (0, SP - TS), (0, 0))).reshape(MS, D)
    proj_s, gates_s = _inproj(xs_pad, (s_mod(1, MS), s_mod(0, MS)), (mod_s, mod_s), norm1_g[0], w_main, w_gate, MS)
    pos_s = past + (jnp.arange(MS, dtype=I32) % SP)
    cos_s, sin_s = _rope_tables(pos_s)
    q_s, knew_s, kb_s, vnew_s, vb_s = _qkrope(proj_s, 4, cos_s, sin_s, q_norm_g[0], k_norm_g[0], MS)
    qr = q_s.reshape(DB, SP, N_HEADS, 2, DQK)[:, :TS].transpose(0, 2, 3, 1, 4)
    zq = jnp.zeros_like(qr[:, :, 0])
    qall = jnp.stack([jnp.concatenate([qr[:, :, 0], zq], axis=-1),
                      jnp.concatenate([zq, qr[:, :, 1]], axis=-1)], axis=2).reshape(DB, N_HEADS * 2 * TS, HEAD_W)
    new_rows = page * N_HEADS
    pad_new = lambda a: jnp.pad(a.reshape(DB, SP * N_HEADS, HEAD_W), ((0, 0), (0, new_rows - SP * N_HEADS), (0, 0)))
    ha_s = _dattn_paged(qall, cache_k[0].reshape(-1, new_rows, HEAD_W), cache_v[0].reshape(-1, new_rows, HEAD_W),
                        page_table, pad_new(kb_s), pad_new(vb_s), lams, diff_out_g[0], TS, lam_init)
    m0_s = jnp.broadcast_to(state_mlstm_m[0][:, :, None], (DB, N_HEADS, LANES))
    cv0_s = jnp.pad(state_conv[0], ((0, 0), (SUBLANES - (CONV_W - 1), 0), (0, 0)))
    hm_s, C_s, n_s, m_s, cv_s = _mlstm(proj_s, gates_s, cw, cb, gate_b, og_m, state_mlstm_C[0], state_mlstm_n[0],
                                       m0_s, cv0_s, DB, SP, 1, TS)
    x1_s, hn2_s, ids_s, gts_s = _outproj(
        hm_s, ha_s, xs_pad, (s_mod(2, MS), s_mod(4, MS), s_mod(3, MS)), (mod_s, mod_s, mod_s),
        norm2_g[0], wo, w_route, b_route, MS)
    real = lambda a: a.reshape(DB, SP, -1)[:, :TS].reshape(NS, -1)
    x1_s, hn2_s, ids_s, gts_s = real(x1_s), real(hn2_s), real(ids_s), real(gts_s)

    NT = NP + NS
    tm_r = 256
    n_rank = -(-NT // tm_r) * tm_r
    ids_all = jnp.concatenate([ids_p, ids_s, jnp.full((n_rank - NT, LANES), -1, I32)], axis=0)
    rank, cnt = _ranks(ids_all, tm_r)
    counts = cnt[0, :N_EXPERTS].astype(I32)
    padded = (counts + MOE_ROWS - 1) // MOE_ROWS * MOE_ROWS
    pend = jnp.cumsum(padded)
    pstart = pend - padded
    e_tok = ids_all[:NT, :2]
    dest = (pstart[e_tok] + rank[:NT, :2]).reshape(-1).astype(I32)
    n_blocks = (2 * NT + N_EXPERTS * (MOE_ROWS - 1)) // MOE_ROWS
    n_valid = (pend[-1] // MOE_ROWS).astype(I32).reshape(1)
    blk_e = jnp.minimum(jnp.searchsorted(pend, jnp.arange(n_blocks, dtype=I32) * MOE_ROWS, side='right'),
                        N_EXPERTS - 1).astype(I32)
    last_e = blk_e[jnp.maximum(n_valid[0] - 1, 0)]
    blk_e = jnp.where(jnp.arange(n_blocks) < n_valid[0], blk_e, last_e)
    blk_first = jnp.concatenate([jnp.ones((1,), I32), (blk_e[1:] != blk_e[:-1]).astype(I32)])
    xs_sorted = _dispatch(dest, hn2_p, hn2_s, n_blocks * MOE_ROWS)
    h_sorted = _expert_up(blk_e, blk_first, n_valid, xs_sorted, w1, w3)
    y_sorted = _expert_down(blk_e, blk_first, n_valid, h_sorted, w2)
    tm_c = 128
    y_p = _combine(dest, x1_p, pl.BlockSpec((None, 1, D), lambda i, dr: ((i * tm_c) // T, 0, 5)), mod3, gts_p,
                   y_sorted, tm_c, 0)
    g2_s = jnp.repeat(mod[B:B + DB, 5 * D:], TS, axis=0)
    y_s = _combine(dest, x1_s, pl.BlockSpec((NS, D), lambda i, dr: (0, 0)), g2_s, gts_s, y_sorted, NS, NP)

    st = lambda a: a[None]
    k5 = lambda a, nb, t: a.reshape(1, nb, t, N_HEADS, HEAD_W)
    tok_s = lambda a: a.reshape(DB, SP, -1)[:, :TS]
    conv_out = lambda cv: cv[None, :, SUBLANES - (CONV_W - 1):, :]
    return (y_p.reshape(B, T, D), y_s.reshape(DB, TS, D),
            k5(knew_p, B, T), k5(vnew_p, B, T), st(C_p), st(n_p), st(m_p[:, :, 0]), conv_out(cv_p),
            k5(tok_s(knew_s), DB, TS), k5(tok_s(vnew_s), DB, TS), st(C_s), st(n_s), st(m_s[:, :, 0]), conv_out(cv_s))
```

```python
import functools
import math

import jax
import jax.numpy as jnp
from jax import lax
from jax.experimental import pallas as pl
from jax.experimental.pallas import tpu as pltpu

F32 = jnp.float32
BF16 = jnp.bfloat16
I32 = jnp.int32

EPS = 1e-6
ROPE_THETA = 10000.0
NEG = -1e30

LANES = 128
SUBLANES = 8
VMEM_LIMIT_CAP = 56 * 1024 * 1024

N_HEADS = 8
HEAD_W = 128
DQK = 64
N_GROUPS = 4
EXPERTS_PER_GROUP = 8
N_EXPERTS = N_GROUPS * EXPERTS_PER_GROUP
CONV_W = 4
MOE_ROWS = 256
SAMPLE_PAD = 8
MLSTM_L = 128
PAGES_PER_STEP = 4


def _cparams(sem, vmem_bytes):
    return pltpu.CompilerParams(dimension_semantics=sem,
                                vmem_limit_bytes=int(min(max(vmem_bytes, 16 * 2**20), VMEM_LIMIT_CAP)))


def _silu(x):
    return x * jax.nn.sigmoid(x)


def _dot(a, b):
    return jnp.dot(a, b, preferred_element_type=F32)


def _dot_nt(a, b):
    return lax.dot_general(a, b, (((1,), (1,)), ((), ())), preferred_element_type=F32)


def _dot_tn(a, b):
    return lax.dot_general(a, b, (((0,), (0,)), ((), ())), preferred_element_type=F32)


def _ada_kernel(c_ref, w_ref, b_ref, o_ref):
    s = _silu(c_ref[...]).astype(BF16)
    o_ref[...] = _dot(s, w_ref[...].astype(BF16)) + b_ref[...]


def _ada(c_all, w_ada, b_ada):
    rows, d = c_all.shape
    n = w_ada.shape[1]
    tn = 1024
    return pl.pallas_call(
        _ada_kernel,
        grid=(n // tn,),
        in_specs=[pl.BlockSpec((rows, d), lambda j: (0, 0)),
                  pl.BlockSpec((d, tn), lambda j: (0, j)),
                  pl.BlockSpec((1, tn), lambda j: (0, j))],
        out_specs=pl.BlockSpec((rows, tn), lambda j: (0, j)),
        out_shape=jax.ShapeDtypeStruct((rows, n), F32),
        compiler_params=_cparams(("arbitrary",), 2 * d * tn * 4 + 3 * d * tn * 2 + 2**22),
        name="ada_mod",
    )(c_all, w_ada, b_ada.reshape(1, n))


def _inproj_kernel(x_ref, sc_ref, sh_ref, g_ref, w_ref, wg_ref, o_ref, og_ref, hn_ref):
    @pl.when(pl.program_id(1) == 0)
    def _():
        x = x_ref[...]
        y = x * lax.rsqrt(jnp.mean(x * x, axis=-1, keepdims=True) + EPS) * g_ref[...]
        hb = (y * (1.0 + sc_ref[...]) + sh_ref[...]).astype(BF16)
        hn_ref[...] = hb
        og_ref[...] = _dot(hb, wg_ref[...])

    o_ref[...] = _dot(hn_ref[...], w_ref[...])


def _inproj(x, mod_specs, mods, g, w, wg, tm):
    m, d = x.shape
    n = w.shape[1]
    tn = 1024
    sc_spec, sh_spec = mod_specs
    return pl.pallas_call(
        _inproj_kernel,
        grid=(m // tm, n // tn),
        in_specs=[pl.BlockSpec((tm, d), lambda i, j: (i, 0)), sc_spec, sh_spec,
                  pl.BlockSpec((1, d), lambda i, j: (0, 0)),
                  pl.BlockSpec((d, tn), lambda i, j: (0, j)),
                  pl.BlockSpec((d, LANES), lambda i, j: (0, 0))],
        out_specs=[pl.BlockSpec((tm, tn), lambda i, j: (i, j)),
                   pl.BlockSpec((tm, LANES), lambda i, j: (i, 0))],
        out_shape=[jax.ShapeDtypeStruct((m, n), F32), jax.ShapeDtypeStruct((m, LANES), F32)],
        scratch_shapes=[pltpu.VMEM((tm, d), BF16)],
        compiler_params=_cparams(("arbitrary", "arbitrary"),
                                 2 * tm * d * 4 + 2 * d * tn * 2 + 2 * tm * tn * 4 + 3 * tm * d * 4 + 2**22),
        name="norm1_inproj",
    )(x, mods[0], mods[1], g.reshape(1, d), w, wg)


def _qkrope_kernel(q_ref, k_ref, v_ref, cos_ref, sin_ref, qg_ref, kg_ref,
                   qo_ref, ko_ref, kb_ref, vo_ref, vb_ref, *t_refs):
    cos = cos_ref[...]
    sin = sin_ref[...]
    lane = lax.broadcasted_iota(I32, (1, LANES), 1)
    first_map = lane < DQK
    first_half = (lane % DQK) < (DQK // 2)

    def norm_rope(x, g):
        x2 = x * x
        s1 = jnp.sum(jnp.where(first_map, x2, 0.0), axis=-1, keepdims=True)
        s2 = jnp.sum(jnp.where(first_map, 0.0, x2), axis=-1, keepdims=True)
        inv = jnp.where(first_map, lax.rsqrt(s1 / DQK + EPS), lax.rsqrt(s2 / DQK + EPS))
        y = x * inv * g
        partner = jnp.where(first_half, pltpu.roll(y, LANES - DQK // 2, 1), pltpu.roll(y, DQK // 2, 1))
        return y * cos + partner * sin

    for h in range(N_HEADS):
        sl = slice(h * HEAD_W, (h + 1) * HEAD_W)
        qh = norm_rope(q_ref[:, sl], qg_ref[...]) * (DQK ** -0.5)
        kh = norm_rope(k_ref[:, sl], kg_ref[...])
        vh = v_ref[:, sl]
        qo_ref[:, sl] = qh.astype(BF16)
        ko_ref[:, sl] = kh
        kb_ref[:, sl] = kh.astype(BF16)
        vo_ref[:, sl] = vh
        vb_ref[:, sl] = vh.astype(BF16)
        if t_refs:
            qt_ref, vt_ref = t_refs
            qt_ref[sl, :] = qh.T.astype(BF16)
            vt_ref[sl, :] = vh.T.astype(BF16)


def _qkrope(proj, col0, cos_t, sin_t, qg, kg, tm, transposed_batches=0):
    m = proj.shape[0]
    w = N_HEADS * HEAD_W
    nt = cos_t.shape[0] // tm
    blk = lambda c: pl.BlockSpec((tm, w), lambda i, c=c: (i, c))
    tab = pl.BlockSpec((tm, LANES), lambda i: (i % nt, 0))
    gsp = pl.BlockSpec((1, LANES), lambda i: (0, 0))
    osp = pl.BlockSpec((tm, w), lambda i: (i, 0))
    g2 = lambda g: jnp.tile(g, 2).reshape(1, LANES)
    out_specs = [osp] * 5
    out_shape = [jax.ShapeDtypeStruct((m, w), BF16), jax.ShapeDtypeStruct((m, w), F32),
                 jax.ShapeDtypeStruct((m, w), BF16), jax.ShapeDtypeStruct((m, w), F32),
                 jax.ShapeDtypeStruct((m, w), BF16)]
    if transposed_batches:
        tsp = pl.BlockSpec((None, None, w, tm), lambda i: (i // nt, i % nt, 0, 0))
        out_specs += [tsp, tsp]
        out_shape += [jax.ShapeDtypeStruct((transposed_batches, nt, w, tm), BF16)] * 2
    return pl.pallas_call(
        _qkrope_kernel,
        grid=(m // tm,),
        in_specs=[blk(col0), blk(col0 + 1), blk(col0 + 2), tab, tab, gsp, gsp],
        out_specs=out_specs,
        out_shape=out_shape,
        compiler_params=_cparams(("arbitrary",), 2 * tm * w * (3 * 4 + 2 * 4 + 5 * 2) + 2**22),
        name="qknorm_rope",
    )(proj, proj, proj, cos_t, sin_t, g2(qg), g2(kg))


def _rope_tables(pos):
    half = DQK // 2
    inv = ROPE_THETA ** (-jnp.arange(half, dtype=F32) / half)
    ang = pos.astype(F32)[:, None] * inv[None, :]
    cos = jnp.tile(jnp.cos(ang), (1, LANES // half))
    sin = jnp.sin(ang)
    sin = jnp.tile(jnp.concatenate([-sin, sin], axis=1), (1, LANES // DQK))
    return cos, sin


def _lam_value(l1, l2, l3, l4, lam_init):
    a = jnp.sum(l1[...] * l2[...], axis=-1, keepdims=True)
    b = jnp.sum(l3[...] * l4[...], axis=-1, keepdims=True)
    return jnp.exp(a) - jnp.exp(b) + lam_init


def _dattn_kernel(qt_ref, k_ref, vt_ref, l1_ref, l2_ref, l3_ref, l4_ref, og_ref, o_ref, *, tq, lam_init):
    qi = pl.program_id(2)
    qt = qt_ref[...]
    feat = lax.broadcasted_iota(I32, (HEAD_W, 1), 0)
    zero = jnp.zeros_like(qt)
    qcat = jnp.concatenate([jnp.where(feat < DQK, qt, zero), jnp.where(feat < DQK, zero, qt)], axis=1)
    key = lax.broadcasted_iota(I32, (tq, 2 * tq), 0)
    qry = lax.broadcasted_iota(I32, (tq, 2 * tq), 1) % tq
    diag = key <= qry

    def scores(j):
        off = pl.multiple_of(j * tq, tq)
        s = _dot(k_ref[pl.ds(off, tq), :], qcat)
        return jnp.where(jnp.logical_or(diag, j < qi), s, NEG)

    def softmax_pv(j, s, m, l, acc, p_prev):
        pv = _dot(vt_ref[jnp.maximum(j - 1, 0)], p_prev)
        m_new = jnp.maximum(m, jnp.max(s, axis=0, keepdims=True))
        alpha = jnp.exp(m - m_new)
        p = jnp.exp(s - m_new)
        l_new = alpha * l + jnp.sum(p, axis=0, keepdims=True)
        return m_new, l_new, alpha * (acc + pv), p.astype(BF16)

    def body(j, c):
        s, m, l, acc, p_prev = c
        s_next = scores(j + 1)
        return (s_next,) + softmax_pv(j, s, m, l, acc, p_prev)

    init = (scores(0), jnp.full((1, 2 * tq), NEG, F32), jnp.zeros((1, 2 * tq), F32),
            jnp.zeros((HEAD_W, 2 * tq), F32), jnp.zeros((tq, 2 * tq), BF16))
    s, m, l, acc, p_prev = lax.fori_loop(0, qi, body, init)
    m, l, acc, p = softmax_pv(qi, s, m, l, acc, p_prev)
    acc = (acc + _dot(vt_ref[qi], p)) / l
    lam = _lam_value(l1_ref, l2_ref, l3_ref, l4_ref, lam_init)
    o = (acc[:, :tq] - lam * acc[:, tq:]).T
    o = o * lax.rsqrt(jnp.mean(o * o, axis=-1, keepdims=True) + EPS) * og_ref[...]
    o_ref[...] = (o * (1.0 - lam_init)).astype(BF16)


def _dattn_prompt(qt, k, vt, lams, og, lam_init):
    nb, nq, _, tq = qt.shape
    t = nq * tq
    lsp = pl.BlockSpec((1, DQK), lambda b, h, i: (0, 0))
    return pl.pallas_call(
        functools.partial(_dattn_kernel, tq=tq, lam_init=lam_init),
        grid=(nb, N_HEADS, nq),
        in_specs=[pl.BlockSpec((None, None, HEAD_W, tq), lambda b, h, i: (b, i, h, 0)),
                  pl.BlockSpec((t, HEAD_W), lambda b, h, i: (b, h)),
                  pl.BlockSpec((None, nq, HEAD_W, tq), lambda b, h, i: (b, 0, h, 0)),
                  lsp, lsp, lsp, lsp,
                  pl.BlockSpec((1, HEAD_W), lambda b, h, i: (0, 0))],
        out_specs=pl.BlockSpec((tq, HEAD_W), lambda b, h, i: (b * nq + i, h)),
        out_shape=jax.ShapeDtypeStruct(k.shape, BF16),
        compiler_params=_cparams(("arbitrary",) * 3, 32 * 2**20),
        name="diff_attn_prompt",
    )(qt, k, vt, *[x.reshape(1, DQK) for x in lams], og.reshape(1, HEAD_W))


def _dattn_paged_kernel(pt_ref, q_ref, *refs, pps, n_new, n_pages, lam_init):
    k_refs = refs[:pps]
    v_refs = refs[pps:2 * pps]
    kn_ref, vn_ref, l1_ref, l2_ref, l3_ref, l4_ref, og_ref, o_ref, s_scr, m_sc, l_sc, acc_sc = refs[2 * pps:]
    phase = pl.program_id(1)
    step_id = pl.program_id(2)
    nsteps = pl.num_programs(2)
    prow = s_scr.shape[2]
    nchunk = prow // LANES
    lane = lax.broadcasted_iota(I32, (SUBLANES, LANES), 1)
    lane_head = lane % N_HEADS
    page_head = lax.broadcasted_iota(I32, (SUBLANES, prow), 1) % N_HEADS
    row = lax.broadcasted_iota(I32, (SUBLANES, prow), 0)
    tile = lambda x: jnp.concatenate([x] * nchunk, axis=1)

    def lane_class_reduce(x, op):
        for sh in (8, 16, 32, 64):
            x = op(x, pltpu.roll(x, sh, 1))
        return x

    def score_pages(ks, first, valid):
        q = q_ref[...]
        parts = []
        for pi, kp in enumerate(ks):
            sf = _dot_nt(q, kp.astype(BF16))
            page = []
            for c in range(nchunk):
                blk = sf[:, c * LANES:(c + 1) * LANES]
                s = jnp.zeros((SUBLANES, LANES), F32)
                for h in range(N_HEADS):
                    s = jnp.where(lane_head == h, blk[h * SUBLANES:(h + 1) * SUBLANES, :], s)
                page.append(s)
            s_page = jnp.concatenate(page, axis=1)
            if valid is not None:
                s_page = jnp.where(valid, s_page, NEG)
                page = [s_page[:, c * LANES:(c + 1) * LANES] for c in range(nchunk)]
            s_scr[first + pi] = s_page
            parts += page
        m_old = m_sc[...]
        m_new = jnp.maximum(m_old, lane_class_reduce(functools.reduce(jnp.maximum, parts), jnp.maximum))
        l_sc[...] = jnp.exp(m_old - m_new) * l_sc[...] + functools.reduce(jnp.add, [jnp.exp(s - m_new) for s in parts])
        m_sc[...] = m_new

    def value_pages(vs, first):
        lam = _lam_value(l1_ref, l2_ref, l3_ref, l4_ref, lam_init)
        m_full = tile(m_sc[...])
        l_full = tile(l_sc[...])
        pv = jnp.zeros(acc_sc.shape, F32)
        for pi, vp in enumerate(vs):
            e = jnp.exp(s_scr[first + pi] - m_full) / l_full
            a = jnp.where(row < n_new, e - lam * pltpu.roll(e, SUBLANES - n_new, 0), 0.0)
            pexp = jnp.concatenate([jnp.where(page_head == h, a, 0.0) for h in range(N_HEADS)], axis=0)
            pv = pv + _dot(pexp.astype(BF16), vp.astype(BF16))
        acc_sc[...] = acc_sc[...] + pv

    @pl.when(phase == 0)
    def _():
        @pl.when(step_id == 0)
        def _():
            m_sc[...] = jnp.full(m_sc.shape, NEG, F32)
            l_sc[...] = jnp.zeros(l_sc.shape, F32)

        score_pages([r[...] for r in k_refs], step_id * pps, None)

        @pl.when(step_id == nsteps - 1)
        def _():
            tok = lax.broadcasted_iota(I32, (SUBLANES, prow), 1) // N_HEADS
            score_pages([kn_ref[...]], n_pages, tok <= row % n_new)
            l_sc[...] = lane_class_reduce(l_sc[...], jnp.add)

    @pl.when(phase == 1)
    def _():
        @pl.when(step_id == 0)
        def _():
            acc_sc[...] = jnp.zeros(acc_sc.shape, F32)

        value_pages([r[...] for r in v_refs], step_id * pps)

        @pl.when(step_id == nsteps - 1)
        def _():
            value_pages([vn_ref[...]], n_pages)
            acc = acc_sc[...]
            outs = []
            for h in range(N_HEADS):
                o = acc[h * SUBLANES:(h + 1) * SUBLANES]
                o = o * lax.rsqrt(jnp.mean(o * o, axis=-1, keepdims=True) + EPS) * og_ref[...]
                outs.append(o * (1.0 - lam_init))
            o_ref[...] = jnp.concatenate(outs, axis=1).astype(BF16)


def _dattn_paged(qall, cache_k, cache_v, page_table, k_new, v_new, lams, og, n_new, lam_init):
    nb, n_pages = page_table.shape
    pps = PAGES_PER_STEP
    prow = cache_k.shape[1]
    nsteps = n_pages // pps
    kspec = lambda j: pl.BlockSpec(
        (None, prow, HEAD_W),
        lambda b, ph, s, pt, j=j: (pt[b * n_pages + (s * (1 - ph) + (nsteps - 1) * ph) * pps + j], 0, 0))
    vspec = lambda j: pl.BlockSpec(
        (None, prow, HEAD_W), lambda b, ph, s, pt, j=j: (pt[b * n_pages + s * ph * pps + j], 0, 0))
    lsp = pl.BlockSpec((1, DQK), lambda b, ph, s, pt: (0, 0))
    nspec = pl.BlockSpec((None, k_new.shape[1], HEAD_W), lambda b, ph, s, pt: (b, 0, 0))
    grid_spec = pltpu.PrefetchScalarGridSpec(
        num_scalar_prefetch=1,
        grid=(nb, 2, nsteps),
        in_specs=[pl.BlockSpec((None, N_HEADS * SUBLANES, HEAD_W), lambda b, ph, s, pt: (b, 0, 0))]
                 + [kspec(j) for j in range(pps)] + [vspec(j) for j in range(pps)]
                 + [nspec, nspec, lsp, lsp, lsp, lsp, pl.BlockSpec((1, HEAD_W), lambda b, ph, s, pt: (0, 0))],
        out_specs=pl.BlockSpec((SUBLANES, N_HEADS * HEAD_W), lambda b, ph, s, pt: (b, 0)),
        scratch_shapes=[pltpu.VMEM((n_pages + 1, SUBLANES, prow), F32),
                        pltpu.VMEM((SUBLANES, LANES), F32), pltpu.VMEM((SUBLANES, LANES), F32),
                        pltpu.VMEM((N_HEADS * SUBLANES, HEAD_W), F32)])
    return pl.pallas_call(
        functools.partial(_dattn_paged_kernel, pps=pps, n_new=n_new, n_pages=n_pages, lam_init=lam_init),
        grid_spec=grid_spec,
        out_shape=jax.ShapeDtypeStruct((nb * SUBLANES, N_HEADS * HEAD_W), BF16),
        compiler_params=_cparams(("arbitrary",) * 3,
                                 4 * pps * prow * HEAD_W * 4 + (n_pages + 1) * SUBLANES * prow * 4 + 24 * 2**20),
        name="diff_attn_paged",
    )(page_table.reshape(-1), qall, *([cache_k] * pps), *([cache_v] * pps), k_new, v_new,
      *[x.reshape(1, DQK) for x in lams], og.reshape(1, HEAD_W))


def _split3(a):
    hi = a.astype(BF16)
    r1 = a - hi.astype(F32)
    mid = r1.astype(BF16)
    lo = (r1 - mid.astype(F32)).astype(BF16)
    return hi, mid, lo


def _mlstm_kernel(qk_ref, v_ref, o_ref, gt_ref, cw_ref, cb_ref, gb_ref, og_ref,
                  C0_ref, n0_ref, m0_ref, cv0_ref,
                  h_ref, C_ref, n_ref, m_ref, cv_ref, xbuf, *, L, rows_in, n_valid):
    c = pl.program_id(1)
    width = N_HEADS * HEAD_W

    @pl.when(c == 0)
    def _():
        C_ref[...] = C0_ref[...]
        n_ref[...] = n0_ref[...]
        m_ref[...] = m0_ref[...]
        xbuf[0:SUBLANES, :] = cv0_ref[...]

    def padded(ref):
        x = ref[...]
        if rows_in < L:
            x = jnp.concatenate([x, jnp.zeros((L - rows_in, x.shape[1]), x.dtype)], axis=0)
        return x

    xbuf[SUBLANES:SUBLANES + L, :] = padded(qk_ref)
    taps = [xbuf[SUBLANES - (CONV_W - 1) + j:SUBLANES - (CONV_W - 1) + j + L, :] * cw_ref[j:j + 1, :]
            for j in range(CONV_W)]
    conv = cb_ref[...] + functools.reduce(jnp.add, taps)
    cv_ref[...] = xbuf[n_valid:n_valid + SUBLANES, :]
    xbuf[0:SUBLANES, :] = xbuf[L:L + SUBLANES, :]
    conv = _silu(conv)
    q_all = conv[:, :width].astype(BF16)
    k_all = conv[:, width:] * (HEAD_W ** -0.5)
    v_all = padded(v_ref).astype(BF16)
    o_all = padded(o_ref)

    g = padded(gt_ref) + gb_ref[...]
    lane = lax.broadcasted_iota(I32, (L, LANES), 1)
    rowi = lax.broadcasted_iota(I32, (L, LANES), 0)
    is_f = (lane >= N_HEADS) & (lane < 2 * N_HEADS)
    logsig = jnp.minimum(g, 0.0) - jnp.log1p(jnp.exp(-jnp.abs(g)))
    a = jnp.where(is_f, logsig, g)
    if n_valid < L:
        a = jnp.where(rowi < n_valid, a, jnp.where(is_f, 0.0, NEG))
    tri = (lax.broadcasted_iota(I32, (L, L), 1) <= lax.broadcasted_iota(I32, (L, L), 0))
    tri_b = tri.astype(BF16)
    cum = functools.reduce(jnp.add, [_dot(tri_b, part) for part in _split3(jnp.where(is_f, a, 0.0))])
    a_t = a.T
    cum_t = cum.T

    for h in range(N_HEADS):
        sl = slice(h * HEAD_W, (h + 1) * HEAD_W)
        b_col = cum[:, N_HEADS + h:N_HEADS + h + 1]
        b_row = cum_t[N_HEADS + h:N_HEADS + h + 1, :]
        i_col = a[:, h:h + 1]
        i_row = a_t[h:h + 1, :]
        m_prev = m_ref[h:h + 1, 0:1]
        d = jnp.where(tri, (b_col - b_row) + i_row, NEG)
        inter = b_col + m_prev
        mt = jnp.maximum(inter, jnp.max(d, axis=-1, keepdims=True))
        w_intra = jnp.exp(d - mt)
        w_inter = jnp.exp(inter - mt)
        qh, kh, vh = q_all[:, sl], k_all[:, sl], v_all[:, sl]
        C_h = C_ref[h]
        n_h = n_ref[h:h + 1, :]
        s = _dot_nt(qh, kh.astype(BF16)) * w_intra
        num = _dot(s.astype(BF16), vh) + w_inter * _dot(qh, C_h.astype(BF16))
        den = (jnp.sum(s, axis=-1, keepdims=True)
               + w_inter * jnp.sum(qh.astype(F32) * n_h.astype(BF16).astype(F32), axis=-1, keepdims=True))
        hh = num / jnp.maximum(jnp.abs(den), jnp.exp(-mt))
        m_last = mt[L - 1:L, :]
        b_last = b_col[L - 1:L, :]
        gk = jnp.exp((b_last - b_col) + i_col - m_last) * kh
        decay = jnp.exp(b_last + m_prev - m_last)
        C_ref[h] = decay * C_h + _dot_tn(gk.astype(BF16), vh)
        n_ref[h:h + 1, :] = decay * n_h + jnp.sum(gk, axis=0, keepdims=True)
        m_ref[h:h + 1, :] = jnp.broadcast_to(m_last, (1, LANES))
        hn = hh * lax.rsqrt(jnp.mean(hh * hh, axis=-1, keepdims=True) + EPS) * og_ref[...]
        out = hn * jax.nn.sigmoid(o_all[:, sl])
        h_ref[:, sl] = out[:rows_in].astype(BF16)


def _mlstm(proj, gates, conv_w, conv_b, gate_b, og, C0, n0, m0, cv0, nb, rows_in, nchunks, n_valid):
    L = MLSTM_L
    w = N_HEADS * HEAD_W
    row = lambda cb: (lambda b, c: (b * nchunks + c, cb))
    st3 = lambda b, c: (b, 0, 0)
    kern = functools.partial(_mlstm_kernel, L=L, rows_in=rows_in, n_valid=n_valid)
    return pl.pallas_call(
        kern,
        grid=(nb, nchunks),
        in_specs=[pl.BlockSpec((rows_in, 2 * w), lambda b, c: (b * nchunks + c, 0)),
                  pl.BlockSpec((rows_in, w), row(2)), pl.BlockSpec((rows_in, w), row(3)),
                  pl.BlockSpec((rows_in, LANES), lambda b, c: (b * nchunks + c, 0)),
                  pl.BlockSpec((SUBLANES, 2 * w), lambda b, c: (0, 0)),
                  pl.BlockSpec((1, 2 * w), lambda b, c: (0, 0)),
                  pl.BlockSpec((1, LANES), lambda b, c: (0, 0)),
                  pl.BlockSpec((1, HEAD_W), lambda b, c: (0, 0)),
                  pl.BlockSpec((None, N_HEADS, HEAD_W, HEAD_W), lambda b, c: (b, 0, 0, 0)),
                  pl.BlockSpec((None, N_HEADS, HEAD_W), st3),
                  pl.BlockSpec((None, N_HEADS, LANES), st3),
                  pl.BlockSpec((None, SUBLANES, 2 * w), st3)],
        out_specs=[pl.BlockSpec((rows_in, w), lambda b, c: (b * nchunks + c, 0)),
                   pl.BlockSpec((None, N_HEADS, HEAD_W, HEAD_W), lambda b, c: (b, 0, 0, 0)),
                   pl.BlockSpec((None, N_HEADS, HEAD_W), st3),
                   pl.BlockSpec((None, N_HEADS, LANES), st3),
                   pl.BlockSpec((None, SUBLANES, 2 * w), st3)],
        out_shape=[jax.ShapeDtypeStruct((nb * nchunks * rows_in, w), BF16),
                   jax.ShapeDtypeStruct((nb, N_HEADS, HEAD_W, HEAD_W), F32),
                   jax.ShapeDtypeStruct((nb, N_HEADS, HEAD_W), F32),
                   jax.ShapeDtypeStruct((nb, N_HEADS, LANES), F32),
                   jax.ShapeDtypeStruct((nb, SUBLANES, 2 * w), F32)],
        scratch_shapes=[pltpu.VMEM((SUBLANES + L, 2 * w), F32)],
        compiler_params=_cparams(("arbitrary", "arbitrary"), 40 * 2**20),
        name="mlstm_chunkwise",
    )(proj, proj, proj, gates, conv_w, conv_b, gate_b, og, C0, n0, m0, cv0)


def _outproj_kernel(hm_ref, ha_ref, x_ref, g1_ref, sc_ref, sh_ref, ng_ref, wo_ref, wr_ref, br_ref,
                    x1_ref, hn_ref, ids_ref, gates_ref):
    half = hm_ref.shape[1]
    mix = _dot(hm_ref[...], wo_ref[0:half, :]) + _dot(ha_ref[...], wo_ref[half:2 * half, :])
    x1 = x_ref[...] + g1_ref[...] * mix
    x1_ref[...] = x1
    y = x1 * lax.rsqrt(jnp.mean(x1 * x1, axis=-1, keepdims=True) + EPS) * ng_ref[...]
    hn = y * (1.0 + sc_ref[...]) + sh_ref[...]
    hn_ref[...] = hn
    logits = _dot(hn.astype(BF16), wr_ref[...]) + br_ref[...]

    lane = lax.broadcasted_iota(I32, logits.shape, 1).astype(F32)
    big = 1000.0
    is_g = lane < N_GROUPS
    gl = jnp.where(is_g, logits, NEG)
    gmax = jnp.max(gl, axis=-1, keepdims=True)
    g_idx = jnp.min(jnp.where(gl == gmax, lane, big), axis=-1, keepdims=True)
    g_p = 1.0 / jnp.sum(jnp.where(is_g, jnp.exp(gl - gmax), 0.0), axis=-1, keepdims=True)
    e_lo = N_GROUPS + EXPERTS_PER_GROUP * g_idx
    in_grp = (lane >= e_lo) & (lane < e_lo + EXPERTS_PER_GROUP)
    el = jnp.where(in_grp, logits, NEG)
    ex = jnp.where(in_grp, jnp.exp(el - jnp.max(el, axis=-1, keepdims=True)), 0.0)
    p = jnp.where(in_grp, ex / jnp.sum(ex, axis=-1, keepdims=True), -1.0)
    top1 = jnp.max(p, axis=-1, keepdims=True)
    idx1 = jnp.min(jnp.where(p == top1, lane, big), axis=-1, keepdims=True)
    p2 = jnp.where(lane == idx1, -1.0, p)
    top2 = jnp.max(p2, axis=-1, keepdims=True)
    idx2 = jnp.min(jnp.where(p2 == top2, lane, big), axis=-1, keepdims=True)
    tsum = top1 + top2
    ids_ref[...] = jnp.where(lane == 0.0, idx1 - N_GROUPS,
                             jnp.where(lane == 1.0, idx2 - N_GROUPS, -1.0)).astype(I32)
    gates_ref[...] = jnp.where(lane == 0.0, g_p * top1 / tsum, jnp.where(lane == 1.0, g_p * top2 / tsum, 0.0))


def _outproj(hm, ha, x, mod_specs, mods, ng, wo, wr, br, tm):
    m, d = x.shape
    half = hm.shape[1]
    row = lambda wdt: pl.BlockSpec((tm, wdt), lambda i: (i, 0))
    const = lambda shp: pl.BlockSpec(shp, lambda i: (0, 0))
    return pl.pallas_call(
        _outproj_kernel,
        grid=(m // tm,),
        in_specs=[row(half), row(half), row(d), *mod_specs, const((1, d)), const((d, d)),
                  const((d, LANES)), const((1, LANES))],
        out_specs=[row(d), row(d), row(LANES), row(LANES)],
        out_shape=[jax.ShapeDtypeStruct((m, d), F32), jax.ShapeDtypeStruct((m, d), F32),
                   jax.ShapeDtypeStruct((m, LANES), I32), jax.ShapeDtypeStruct((m, LANES), F32)],
        compiler_params=_cparams(("arbitrary",), 2 * d * d * 2 + 8 * tm * d * 4 + 8 * 2**20),
        name="outproj_norm2_route",
    )(hm, ha, x, *mods, ng.reshape(1, d), wo, wr, br)


def _rank_kernel(ids_ref, dest_ref, cnt_ref, carry, pstart):
    pss = pl.program_id(0)
    i = pl.program_id(1)
    ids = ids_ref[...]
    tm = ids.shape[0]
    lane = lax.broadcasted_iota(I32, ids.shape, 1)
    o0 = lane == ids[:, 0:1]
    o1 = lane == ids[:, 1:2]
    onehot = jnp.where(o0 | o1, 1.0, 0.0)
    col_counts = jnp.sum(onehot, axis=0, keepdims=True)

    @pl.when((pss == 0) & (i == 0))
    def _():
        carry[...] = jnp.zeros(carry.shape, F32)

    @pl.when(pss == 0)
    def _():
        carry[...] = carry[...] + col_counts

    @pl.when((pss == 1) & (i == 0))
    def _():
        cnt = carry[...]
        cnt_ref[...] = cnt
        blocks = jnp.floor((cnt + (MOE_ROWS - 1)) * (1.0 / MOE_ROWS))
        earlier = lax.broadcasted_iota(I32, (LANES, LANES), 0) < lax.broadcasted_iota(I32, (LANES, LANES), 1)
        pstart[...] = _dot(blocks.astype(BF16), earlier.astype(BF16)) * MOE_ROWS
        carry[...] = jnp.zeros(carry.shape, F32)

    @pl.when(pss == 1)
    def _():
        strict = lax.broadcasted_iota(I32, (tm, tm), 1) < lax.broadcasted_iota(I32, (tm, tm), 0)
        before = _dot(strict.astype(BF16), onehot.astype(BF16)) + carry[0:1, :] + pstart[0:1, :]
        r0 = jnp.sum(jnp.where(o0, before, 0.0), axis=-1, keepdims=True)
        r1 = jnp.sum(jnp.where(o1, before, 0.0), axis=-1, keepdims=True)
        dest_ref[...] = jnp.where(lane == 0, r0, jnp.where(lane == 1, r1, 0.0)).astype(I32)
        carry[...] = carry[...] + col_counts


def _ranks(ids, tm):
    m = ids.shape[0]
    return pl.pallas_call(
        _rank_kernel,
        grid=(2, m // tm),
        in_specs=[pl.BlockSpec((tm, LANES), lambda p, i: (i, 0))],
        out_specs=[pl.BlockSpec((tm, LANES), lambda p, i: (i * p, 0)),
                   pl.BlockSpec((SUBLANES, LANES), lambda p, i: (0, 0))],
        out_shape=[jax.ShapeDtypeStruct((m, LANES), I32), jax.ShapeDtypeStruct((SUBLANES, LANES), F32)],
        scratch_shapes=[pltpu.VMEM((SUBLANES, LANES), F32), pltpu.VMEM((SUBLANES, LANES), F32)],
        compiler_params=_cparams(("arbitrary", "arbitrary"), 16 * 2**20),
        name="moe_ranks",
    )(ids)


def _row_copies(idx_ref, t0, tm, copy):
    def body(t, _):
        for k in range(2):
            copy(t, k, idx_ref[2 * (t0 + t) + k]).start()
        return 0
    lax.fori_loop(0, tm, body, 0)


def _dispatch_kernel(dest_ref, h_ref, xin_ref, xs_ref, sem, *, tm, base):
    del xin_ref
    t0 = base + pl.program_id(0) * tm
    _row_copies(dest_ref, t0, tm,
                lambda t, k, d: pltpu.make_async_copy(h_ref.at[pl.ds(t, 1)], xs_ref.at[pl.ds(d, 1)], sem))
    for _ in range(2):
        pltpu.make_async_copy(h_ref, xs_ref.at[pl.ds(0, tm)], sem).wait()


def _dispatch(dest_flat, hn, xs, tm, base):
    m, d = hn.shape
    any_spec = pl.BlockSpec(memory_space=pl.ANY)
    grid_spec = pltpu.PrefetchScalarGridSpec(
        num_scalar_prefetch=1, grid=(m // tm,),
        in_specs=[pl.BlockSpec((tm, d), lambda i, dr: (i, 0)), any_spec], out_specs=any_spec,
        scratch_shapes=[pltpu.SemaphoreType.DMA])
    return pl.pallas_call(
        functools.partial(_dispatch_kernel, tm=tm, base=base),
        grid_spec=grid_spec,
        out_shape=jax.ShapeDtypeStruct(xs.shape, xs.dtype),
        input_output_aliases={2: 0},
        compiler_params=_cparams(("arbitrary",), 4 * tm * d * 4 + 8 * 2**20),
        name="moe_dispatch",
    )(dest_flat, hn, xs)


def _expert_up_kernel(be_ref, bf_ref, nv_ref, x_ref, w1_ref, w3_ref, h_ref, w1b, w3b):
    b = pl.program_id(0)

    @pl.when(bf_ref[b] == 1)
    def _():
        w1b[...] = w1_ref[...].astype(BF16)
        w3b[...] = w3_ref[...].astype(BF16)

    @pl.when(b < nv_ref[0])
    def _():
        x = x_ref[...].astype(BF16)
        h_ref[...] = (_silu(_dot(x, w1b[...])) * _dot(x, w3b[...])).astype(BF16)

    @pl.when(b >= nv_ref[0])
    def _():
        h_ref[...] = jnp.zeros(h_ref.shape, BF16)


def _expert_up(blk_e, blk_first, n_valid, xs, w1, w3):
    rows, d = xs.shape
    f = w1.shape[-1]
    nb = rows // MOE_ROWS
    wspec = pl.BlockSpec((None, None, d, f), lambda b, be, bf, nv: (0, be[b], 0, 0))
    grid_spec = pltpu.PrefetchScalarGridSpec(
        num_scalar_prefetch=3, grid=(nb,),
        in_specs=[pl.BlockSpec((MOE_ROWS, d), lambda b, be, bf, nv: (b, 0)), wspec, wspec],
        out_specs=pl.BlockSpec((MOE_ROWS, f), lambda b, be, bf, nv: (b, 0)),
        scratch_shapes=[pltpu.VMEM((d, f), BF16), pltpu.VMEM((d, f), BF16)])
    return pl.pallas_call(
        _expert_up_kernel,
        grid_spec=grid_spec,
        out_shape=jax.ShapeDtypeStruct((rows, f), BF16),
        compiler_params=_cparams(("arbitrary",), VMEM_LIMIT_CAP),
        name="moe_expert_up",
    )(blk_e, blk_first, n_valid, xs, w1, w3)


def _expert_down_kernel(be_ref, bf_ref, nv_ref, h_ref, w2_ref, y_ref, w2b):
    b = pl.program_id(0)

    @pl.when(bf_ref[b] == 1)
    def _():
        w2b[...] = w2_ref[...].astype(BF16)

    @pl.when(b < nv_ref[0])
    def _():
        y_ref[...] = _dot(h_ref[...], w2b[...])

    @pl.when(b >= nv_ref[0])
    def _():
        y_ref[...] = jnp.zeros(y_ref.shape, F32)


def _expert_down(blk_e, blk_first, n_valid, hs, w2):
    rows, f = hs.shape
    d = w2.shape[-1]
    nb = rows // MOE_ROWS
    grid_spec = pltpu.PrefetchScalarGridSpec(
        num_scalar_prefetch=3, grid=(nb,),
        in_specs=[pl.BlockSpec((MOE_ROWS, f), lambda b, be, bf, nv: (b, 0)),
                  pl.BlockSpec((None, None, f, d), lambda b, be, bf, nv: (0, be[b], 0, 0))],
        out_specs=pl.BlockSpec((MOE_ROWS, d), lambda b, be, bf, nv: (b, 0)),
        scratch_shapes=[pltpu.VMEM((f, d), BF16)])
    return pl.pallas_call(
        _expert_down_kernel,
        grid_spec=grid_spec,
        out_shape=jax.ShapeDtypeStruct((rows, d), F32),
        compiler_params=_cparams(("arbitrary",), 40 * 2**20),
        name="moe_expert_down",
    )(blk_e, blk_first, n_valid, hs, w2)


def _combine_kernel(dest_ref, x1_ref, g2_ref, gates_ref, y_ref, o_ref, ybuf, sem, *, tm, base):
    t0 = base + pl.program_id(0) * tm
    _row_copies(dest_ref, t0, tm,
                lambda t, k, d: pltpu.make_async_copy(y_ref.at[pl.ds(d, 1)], ybuf.at[k, pl.ds(t, 1)], sem))
    for k in range(2):
        pltpu.make_async_copy(y_ref.at[pl.ds(0, tm)], ybuf.at[k], sem).wait()
    gates = gates_ref[...]
    ff = gates[:, 0:1] * ybuf[0] + gates[:, 1:2] * ybuf[1]
    o_ref[...] = x1_ref[...] + g2_ref[...] * ff


def _combine(dest_flat, x1, g2_spec, g2, gates, y, tm, base):
    m, d = x1.shape
    grid_spec = pltpu.PrefetchScalarGridSpec(
        num_scalar_prefetch=1, grid=(m // tm,),
        in_specs=[pl.BlockSpec((tm, d), lambda i, dr: (i, 0)), g2_spec,
                  pl.BlockSpec((tm, LANES), lambda i, dr: (i, 0)),
                  pl.BlockSpec(memory_space=pl.ANY)],
        out_specs=pl.BlockSpec((tm, d), lambda i, dr: (i, 0)),
        scratch_shapes=[pltpu.VMEM((2, tm, d), F32), pltpu.SemaphoreType.DMA])
    return pl.pallas_call(
        functools.partial(_combine_kernel, tm=tm, base=base),
        grid_spec=grid_spec,
        out_shape=jax.ShapeDtypeStruct((m, d), F32),
        compiler_params=_cparams(("arbitrary",), 6 * tm * d * 4 + 8 * 2**20),
        name="moe_combine",
    )(dest_flat, x1, g2, gates, y)


def kernel(x_prompt, x_sample, cache_k, cache_v, page_table, state_mlstm_C, state_mlstm_n, state_mlstm_m, state_conv,
           c_prompt, c_sample, w_ada, b_ada, norm1_g, norm2_g, w_in, conv_w, conv_b, b_igate, b_fgate, mlstm_out_g,
           q_norm_g, k_norm_g, lam_q1, lam_k1, lam_q2, lam_k2, diff_out_g, w_out, w_grp, b_grp, w_rt, b_rt, w1, w3, w2):
    B, T, D = x_prompt.shape
    DB, TS, _ = x_sample.shape
    depth = w_ada.shape[0]
    assert depth == 1, "single-layer trunk"
    n_pages, page = page_table.shape[1], cache_k.shape[2]
    past = n_pages * page
    W = N_HEADS * HEAD_W
    lam_init = 0.8 - 0.6 * math.exp(-0.3 * 0)
    NP, NS = B * T, DB * TS
    SP = SAMPLE_PAD

    c_all = jnp.concatenate([c_prompt, c_sample, jnp.zeros((16 - B - DB, D), F32)], axis=0)
    wi = w_in[0]
    gate_lo = 4 * W
    w_main = jnp.concatenate([wi[:, :gate_lo].astype(BF16), wi[:, gate_lo + 2 * N_HEADS:].astype(BF16)], axis=1)
    w_gate = jnp.pad(wi[:, gate_lo:gate_lo + 2 * N_HEADS], ((0, 0), (0, LANES - 2 * N_HEADS))).astype(BF16)
    gate_b = jnp.pad(jnp.concatenate([b_igate[0], b_fgate[0]]), (0, LANES - 2 * N_HEADS)).reshape(1, LANES)
    cw = jnp.pad(conv_w[0], ((0, SUBLANES - CONV_W), (0, 0)))
    cb = conv_b[0].reshape(1, 2 * W)
    og_m = mlstm_out_g[0].reshape(1, HEAD_W)
    wo = w_out[0].astype(BF16)
    w_route = jnp.concatenate([w_grp[0], w_rt[0].transpose(1, 0, 2).reshape(D, N_EXPERTS)], axis=1)
    w_route = jnp.pad(w_route, ((0, 0), (0, LANES - N_GROUPS - N_EXPERTS))).astype(BF16)
    b_route = jnp.pad(jnp.concatenate([b_grp[0], b_rt[0].reshape(-1)]), (0, LANES - N_GROUPS - N_EXPERTS)).reshape(1, LANES)
    lams = (lam_q1[0], lam_k1[0], lam_q2[0], lam_k2[0])

    mod = _ada(c_all, w_ada[0], b_ada[0])
    mod3 = mod.reshape(16, 1, 6 * D)
    mod_s = jnp.repeat(mod[B:B + DB], SP, axis=0)

    def p_mod(chunk, tm):
        return pl.BlockSpec((None, 1, D), lambda i, *_: ((i * tm) // T, 0, chunk))

    def s_mod(chunk, tm):
        return pl.BlockSpec((tm, D), lambda i, *_: (i, chunk))

    xp = x_prompt.reshape(NP, D)
    tm_in = 512
    proj_p, gates_p = _inproj(xp, (p_mod(1, tm_in), p_mod(0, tm_in)), (mod3, mod3), norm1_g[0], w_main, w_gate, tm_in)
    cos_p, sin_p = _rope_tables(jnp.arange(T, dtype=I32))
    _, knew_p, kb_p, vnew_p, _, qt_p, vt_p = _qkrope(proj_p, 4, cos_p, sin_p, q_norm_g[0], k_norm_g[0], 256, B)
    ha_p = _dattn_prompt(qt_p, kb_p, vt_p, lams, diff_out_g[0], lam_init)
    zeros = lambda *s: jnp.zeros(s, F32)
    hm_p, C_p, n_p, m_p, cv_p = _mlstm(
        proj_p, gates_p, cw, cb, gate_b, og_m,
        zeros(B, N_HEADS, HEAD_W, HEAD_W), zeros(B, N_HEADS, HEAD_W), zeros(B, N_HEADS, LANES),
        zeros(B, SUBLANES, 2 * W), B, MLSTM_L, T // MLSTM_L, MLSTM_L)
    tm_o = 256
    x1_p, hn2_p, ids_p, gts_p = _outproj(
        hm_p, ha_p, xp, (p_mod(2, tm_o), p_mod(4, tm_o), p_mod(3, tm_o)), (mod3, mod3, mod3),
        norm2_g[0], wo, w_route, b_route, tm_o)

    MS = DB * SP
    xs_pad = jnp.pad(x_sample, ((0, 0), (0, SP - TS), (0, 0))).reshape(MS, D)
    proj_s, gates_s = _inproj(xs_pad, (s_mod(1, MS), s_mod(0, MS)), (mod_s, mod_s), norm1_g[0], w_main, w_gate, MS)
    pos_s = past + (jnp.arange(MS, dtype=I32) % SP)
    cos_s, sin_s = _rope_tables(pos_s)
    q_s, knew_s, kb_s, vnew_s, vb_s = _qkrope(proj_s, 4, cos_s, sin_s, q_norm_g[0], k_norm_g[0], MS)
    qr = q_s.reshape(DB, SP, N_HEADS, 2, DQK)[:, :TS].transpose(0, 2, 3, 1, 4)
    zq = jnp.zeros_like(qr[:, :, 0])
    qall = jnp.stack([jnp.concatenate([qr[:, :, 0], zq], axis=-1),
                      jnp.concatenate([zq, qr[:, :, 1]], axis=-1)], axis=2).reshape(DB, N_HEADS * 2 * TS, HEAD_W)
    new_rows = page * N_HEADS
    pad_new = lambda a: jnp.pad(a.reshape(DB, SP * N_HEADS, HEAD_W), ((0, 0), (0, new_rows - SP * N_HEADS), (0, 0)))
    ha_s = _dattn_paged(qall, cache_k[0].reshape(-1, new_rows, HEAD_W), cache_v[0].reshape(-1, new_rows, HEAD_W),
                        page_table, pad_new(kb_s), pad_new(vb_s), lams, diff_out_g[0], TS, lam_init)
    m0_s = jnp.broadcast_to(state_mlstm_m[0][:, :, None], (DB, N_HEADS, LANES))
    cv0_s = jnp.pad(state_conv[0], ((0, 0), (SUBLANES - (CONV_W - 1), 0), (0, 0)))
    hm_s, C_s, n_s, m_s, cv_s = _mlstm(proj_s, gates_s, cw, cb, gate_b, og_m, state_mlstm_C[0], state_mlstm_n[0],
                                       m0_s, cv0_s, DB, SP, 1, TS)
    x1_s, hn2_s, ids_s, gts_s = _outproj(
        hm_s, ha_s, xs_pad, (s_mod(2, MS), s_mod(4, MS), s_mod(3, MS)), (mod_s, mod_s, mod_s),
        norm2_g[0], wo, w_route, b_route, MS)
    real = lambda a: a.reshape(DB, SP, -1)[:, :TS].reshape(NS, -1)
    x1_s, hn2_s, ids_s, gts_s = real(x1_s), real(hn2_s), real(ids_s), real(gts_s)

    NT = NP + NS
    tm_r = 256
    n_rank = -(-NT // tm_r) * tm_r
    ids_all = jnp.concatenate([ids_p, ids_s, jnp.full((n_rank - NT, LANES), -1, I32)], axis=0)
    dest2, cnt = _ranks(ids_all, tm_r)
    dest = dest2[:NT, :2].reshape(-1)
    counts = cnt[0, :N_EXPERTS].astype(I32)
    pend = jnp.cumsum((counts + MOE_ROWS - 1) // MOE_ROWS)
    n_blocks = (2 * NT + N_EXPERTS * (MOE_ROWS - 1)) // MOE_ROWS
    n_valid = pend[-1:]
    blk = jnp.minimum(jnp.arange(n_blocks, dtype=I32), n_valid[0] - 1)
    blk_e = jnp.sum((blk[:, None] >= pend[None, :]).astype(I32), axis=1)
    blk_first = jnp.concatenate([jnp.ones((1,), I32), (blk_e[1:] != blk_e[:-1]).astype(I32)])
    xs_sorted = _dispatch(dest, hn2_p, jnp.zeros((n_blocks * MOE_ROWS, D), F32), 256, 0)
    xs_sorted = _dispatch(dest, hn2_s, xs_sorted, NS, NP)
    h_sorted = _expert_up(blk_e, blk_first, n_valid, xs_sorted, w1, w3)
    y_sorted = _expert_down(blk_e, blk_first, n_valid, h_sorted, w2)
    tm_c = 128
    y_p = _combine(dest, x1_p, pl.BlockSpec((None, 1, D), lambda i, dr: ((i * tm_c) // T, 0, 5)), mod3, gts_p,
                   y_sorted, tm_c, 0)
    g2_s = jnp.repeat(mod[B:B + DB, 5 * D:], TS, axis=0)
    y_s = _combine(dest, x1_s, pl.BlockSpec((NS, D), lambda i, dr: (0, 0)), g2_s, gts_s, y_sorted, NS, NP)

    st = lambda a: a[None]
    k5 = lambda a, nb, t: a.reshape(1, nb, t, N_HEADS, HEAD_W)
    tok_s = lambda a: a.reshape(DB, SP, -1)[:, :TS]
    conv_out = lambda cv: cv[None, :, SUBLANES - (CONV_W - 1):, :]
    return (y_p.reshape(B, T, D), y_s.reshape(DB, TS, D),
            k5(knew_p, B, T), k5(vnew_p, B, T), st(C_p), st(n_p), st(m_p[:, :, 0]), conv_out(cv_p),
            k5(tok_s(knew_s), DB, TS), k5(tok_s(vnew_s), DB, TS), st(C_s), st(n_s), st(m_s[:, :, 0]), conv_out(cv_s))
```

```python
import functools
import math

import jax
import jax.numpy as jnp
from jax import lax
from jax.experimental import pallas as pl
from jax.experimental.pallas import tpu as pltpu

F32 = jnp.float32
BF16 = jnp.bfloat16
I32 = jnp.int32

EPS = 1e-6
ROPE_THETA = 10000.0
NEG = -1e30

LANES = 128
SUBLANES = 8
VMEM_LIMIT_CAP = 56 * 1024 * 1024

N_HEADS = 8
HEAD_W = 128
DQK = 64
N_GROUPS = 4
EXPERTS_PER_GROUP = 8
N_EXPERTS = N_GROUPS * EXPERTS_PER_GROUP
CONV_W = 4
MOE_ROWS = 256
SAMPLE_PAD = 8
MLSTM_L = 128
PAGES_PER_STEP = 8


def _cparams(sem, vmem_bytes):
    return pltpu.CompilerParams(dimension_semantics=sem,
                                vmem_limit_bytes=int(min(max(vmem_bytes, 16 * 2**20), VMEM_LIMIT_CAP)))


def _silu(x):
    return x * jax.nn.sigmoid(x)


def _dot(a, b):
    return jnp.dot(a, b, preferred_element_type=F32)


def _dot_nt(a, b):
    return lax.dot_general(a, b, (((1,), (1,)), ((), ())), preferred_element_type=F32)


def _dot_tn(a, b):
    return lax.dot_general(a, b, (((0,), (0,)), ((), ())), preferred_element_type=F32)


def _ada_kernel(c_ref, w_ref, b_ref, o_ref):
    s = _silu(c_ref[...]).astype(BF16)
    o_ref[...] = _dot(s, w_ref[...].astype(BF16)) + b_ref[...]


def _ada(c_all, w_ada, b_ada):
    rows, d = c_all.shape
    n = w_ada.shape[1]
    tn = 1024
    return pl.pallas_call(
        _ada_kernel,
        grid=(n // tn,),
        in_specs=[pl.BlockSpec((rows, d), lambda j: (0, 0)),
                  pl.BlockSpec((d, tn), lambda j: (0, j)),
                  pl.BlockSpec((1, tn), lambda j: (0, j))],
        out_specs=pl.BlockSpec((rows, tn), lambda j: (0, j)),
        out_shape=jax.ShapeDtypeStruct((rows, n), F32),
        compiler_params=_cparams(("arbitrary",), 2 * d * tn * 4 + 3 * d * tn * 2 + 2**22),
        name="ada_mod",
    )(c_all, w_ada, b_ada.reshape(1, n))


def _inproj_kernel(x_ref, sc_ref, sh_ref, g_ref, w_ref, wg_ref, o_ref, og_ref, hn_ref):
    @pl.when(pl.program_id(1) == 0)
    def _():
        x = x_ref[...]
        y = x * lax.rsqrt(jnp.mean(x * x, axis=-1, keepdims=True) + EPS) * g_ref[...]
        hb = (y * (1.0 + sc_ref[...]) + sh_ref[...]).astype(BF16)
        hn_ref[...] = hb
        og_ref[...] = _dot(hb, wg_ref[...])

    o_ref[...] = _dot(hn_ref[...], w_ref[...])


def _inproj(x, mod_specs, mods, g, w, wg, tm):
    m, d = x.shape
    n = w.shape[1]
    tn = 1024
    sc_spec, sh_spec = mod_specs
    return pl.pallas_call(
        _inproj_kernel,
        grid=(m // tm, n // tn),
        in_specs=[pl.BlockSpec((tm, d), lambda i, j: (i, 0)), sc_spec, sh_spec,
                  pl.BlockSpec((1, d), lambda i, j: (0, 0)),
                  pl.BlockSpec((d, tn), lambda i, j: (0, j)),
                  pl.BlockSpec((d, LANES), lambda i, j: (0, 0))],
        out_specs=[pl.BlockSpec((tm, tn), lambda i, j: (i, j)),
                   pl.BlockSpec((tm, LANES), lambda i, j: (i, 0))],
        out_shape=[jax.ShapeDtypeStruct((m, n), F32), jax.ShapeDtypeStruct((m, LANES), F32)],
        scratch_shapes=[pltpu.VMEM((tm, d), BF16)],
        compiler_params=_cparams(("arbitrary", "arbitrary"),
                                 2 * tm * d * 4 + 2 * d * tn * 2 + 2 * tm * tn * 4 + 3 * tm * d * 4 + 2**22),
        name="norm1_inproj",
    )(x, mods[0], mods[1], g.reshape(1, d), w, wg)


def _qkrope_kernel(q_ref, k_ref, v_ref, cos_ref, sin_ref, qg_ref, kg_ref,
                   qo_ref, ko_ref, kb_ref, vo_ref, vb_ref, *t_refs):
    cos = cos_ref[...]
    sin = sin_ref[...]
    lane = lax.broadcasted_iota(I32, (1, LANES), 1)
    first_map = lane < DQK
    first_half = (lane % DQK) < (DQK // 2)

    def norm_rope(x, g):
        x2 = x * x
        s1 = jnp.sum(jnp.where(first_map, x2, 0.0), axis=-1, keepdims=True)
        s2 = jnp.sum(jnp.where(first_map, 0.0, x2), axis=-1, keepdims=True)
        inv = jnp.where(first_map, lax.rsqrt(s1 / DQK + EPS), lax.rsqrt(s2 / DQK + EPS))
        y = x * inv * g
        partner = jnp.where(first_half, pltpu.roll(y, LANES - DQK // 2, 1), pltpu.roll(y, DQK // 2, 1))
        return y * cos + partner * sin

    for h in range(N_HEADS):
        sl = slice(h * HEAD_W, (h + 1) * HEAD_W)
        qh = norm_rope(q_ref[:, sl], qg_ref[...]) * (DQK ** -0.5)
        kh = norm_rope(k_ref[:, sl], kg_ref[...])
        vh = v_ref[:, sl]
        qo_ref[:, sl] = qh.astype(BF16)
        ko_ref[:, sl] = kh
        kb_ref[:, sl] = kh.astype(BF16)
        vo_ref[:, sl] = vh
        vb_ref[:, sl] = vh.astype(BF16)
        if t_refs:
            qt_ref, vt_ref = t_refs
            qt_ref[sl, :] = qh.T.astype(BF16)
            vt_ref[sl, :] = vh.T.astype(BF16)


def _qkrope(proj, col0, cos_t, sin_t, qg, kg, tm, transposed_batches=0):
    m = proj.shape[0]
    w = N_HEADS * HEAD_W
    nt = cos_t.shape[0] // tm
    blk = lambda c: pl.BlockSpec((tm, w), lambda i, c=c: (i, c))
    tab = pl.BlockSpec((tm, LANES), lambda i: (i % nt, 0))
    gsp = pl.BlockSpec((1, LANES), lambda i: (0, 0))
    osp = pl.BlockSpec((tm, w), lambda i: (i, 0))
    g2 = lambda g: jnp.tile(g, 2).reshape(1, LANES)
    out_specs = [osp] * 5
    out_shape = [jax.ShapeDtypeStruct((m, w), BF16), jax.ShapeDtypeStruct((m, w), F32),
                 jax.ShapeDtypeStruct((m, w), BF16), jax.ShapeDtypeStruct((m, w), F32),
                 jax.ShapeDtypeStruct((m, w), BF16)]
    if transposed_batches:
        tsp = pl.BlockSpec((None, None, w, tm), lambda i: (i // nt, i % nt, 0, 0))
        out_specs += [tsp, tsp]
        out_shape += [jax.ShapeDtypeStruct((transposed_batches, nt, w, tm), BF16)] * 2
    return pl.pallas_call(
        _qkrope_kernel,
        grid=(m // tm,),
        in_specs=[blk(col0), blk(col0 + 1), blk(col0 + 2), tab, tab, gsp, gsp],
        out_specs=out_specs,
        out_shape=out_shape,
        compiler_params=_cparams(("arbitrary",), 2 * tm * w * (3 * 4 + 2 * 4 + 5 * 2) + 2**22),
        name="qknorm_rope",
    )(proj, proj, proj, cos_t, sin_t, g2(qg), g2(kg))


def _rope_tables(pos):
    half = DQK // 2
    inv = ROPE_THETA ** (-jnp.arange(half, dtype=F32) / half)
    ang = pos.astype(F32)[:, None] * inv[None, :]
    cos = jnp.tile(jnp.cos(ang), (1, LANES // half))
    sin = jnp.sin(ang)
    sin = jnp.tile(jnp.concatenate([-sin, sin], axis=1), (1, LANES // DQK))
    return cos, sin


def _lam_value(l1, l2, l3, l4, lam_init):
    a = jnp.sum(l1[...] * l2[...], axis=-1, keepdims=True)
    b = jnp.sum(l3[...] * l4[...], axis=-1, keepdims=True)
    return jnp.exp(a) - jnp.exp(b) + lam_init


def _dattn_kernel(qt_ref, k_ref, vt_ref, l1_ref, l2_ref, l3_ref, l4_ref, og_ref, o_ref,
                  s_a, s_b, p_a, p_b, cm_ref, stat_ref, acc_ref, *, tq, lam_init):
    qi = pl.program_id(2)
    tk = tq // 2
    s_bufs, p_bufs = (s_a, s_b), (p_a, p_b)
    qt = jnp.concatenate([qt_ref[0], qt_ref[1]], axis=1)
    feat = lax.broadcasted_iota(I32, (HEAD_W, 1), 0)
    zero = jnp.zeros_like(qt)
    qcat = jnp.concatenate([jnp.where(feat < DQK, qt, zero), jnp.where(feat < DQK, zero, qt)], axis=1)
    key = lax.broadcasted_iota(I32, (tk, 2 * tq), 0)
    qry = lax.broadcasted_iota(I32, (tk, 2 * tq), 1) % tq

    def set_scores(slot, s, mask):
        if mask is not None:
            s = jnp.where(mask, s, NEG)
        s_bufs[slot][...] = s
        cm_ref[slot:slot + 1, :] = jnp.max(s, axis=0, keepdims=True)

    def scores_into(j, slot, mask=None):
        off = pl.multiple_of(j * tk, tk)
        set_scores(slot, _dot(k_ref[pl.ds(off, tk), :], qcat), mask)

    def softmax_pv(j, slot):
        pv = _dot(vt_ref[jnp.maximum(j - 1, 0)], p_bufs[1 - slot][...])
        m = stat_ref[0:1, :]
        m_new = jnp.maximum(m, cm_ref[slot:slot + 1, :])
        alpha = jnp.exp(m - m_new)
        p = jnp.exp(s_bufs[slot][...] - m_new)
        stat_ref[0:1, :] = m_new
        stat_ref[1:2, :] = alpha * stat_ref[1:2, :] + jnp.sum(p, axis=0, keepdims=True)
        p_bufs[slot][...] = p.astype(BF16)
        acc_ref[...] = alpha * (acc_ref[...] + pv)

    stat_ref[0:1, :] = jnp.full((1, 2 * tq), NEG, F32)
    stat_ref[1:2, :] = jnp.zeros((1, 2 * tq), F32)
    acc_ref[...] = jnp.zeros(acc_ref.shape, F32)
    p_b[...] = jnp.zeros(p_b.shape, BF16)
    scores_into(0, 0)

    def body(t, _):
        scores_into(2 * t + 1, 1)
        softmax_pv(2 * t, 0)
        scores_into(2 * t + 2, 0)
        softmax_pv(2 * t + 1, 1)
        return 0

    lax.fori_loop(0, qi, body, 0)
    set_scores(0, s_a[...], key <= qry)
    scores_into(2 * qi + 1, 1, key + tk <= qry)
    softmax_pv(2 * qi, 0)
    softmax_pv(2 * qi + 1, 1)
    acc = (acc_ref[...] + _dot(vt_ref[2 * qi + 1], p_b[...])) / stat_ref[1:2, :]
    lam = _lam_value(l1_ref, l2_ref, l3_ref, l4_ref, lam_init)
    o = (acc[:, :tq] - lam * acc[:, tq:]).T
    o = o * lax.rsqrt(jnp.mean(o * o, axis=-1, keepdims=True) + EPS) * og_ref[...]
    o_ref[...] = (o * (1.0 - lam_init)).astype(BF16)


def _dattn_prompt(qt, k, vt, lams, og, lam_init):
    nb, nk, _, tk = qt.shape
    t = nk * tk
    tq = 2 * tk
    nq = t // tq
    lsp = pl.BlockSpec((1, DQK), lambda b, h, i: (0, 0))
    return pl.pallas_call(
        functools.partial(_dattn_kernel, tq=tq, lam_init=lam_init),
        grid=(nb, N_HEADS, nq),
        in_specs=[pl.BlockSpec((None, 2, HEAD_W, tk), lambda b, h, i: (b, i, h, 0)),
                  pl.BlockSpec((t, HEAD_W), lambda b, h, i: (b, h)),
                  pl.BlockSpec((None, nk, HEAD_W, tk), lambda b, h, i: (b, 0, h, 0)),
                  lsp, lsp, lsp, lsp,
                  pl.BlockSpec((1, HEAD_W), lambda b, h, i: (0, 0))],
        out_specs=pl.BlockSpec((tq, HEAD_W), lambda b, h, i: (b * nq + i, h)),
        out_shape=jax.ShapeDtypeStruct(k.shape, BF16),
        scratch_shapes=[pltpu.VMEM((tk, 2 * tq), F32), pltpu.VMEM((tk, 2 * tq), F32),
                        pltpu.VMEM((tk, 2 * tq), BF16), pltpu.VMEM((tk, 2 * tq), BF16),
                        pltpu.VMEM((SUBLANES, 2 * tq), F32), pltpu.VMEM((SUBLANES, 2 * tq), F32),
                        pltpu.VMEM((HEAD_W, 2 * tq), F32)],
        compiler_params=_cparams(("arbitrary",) * 3, 32 * 2**20),
        name="diff_attn_prompt",
    )(qt, k, vt, *[x.reshape(1, DQK) for x in lams], og.reshape(1, HEAD_W))


def _dattn_paged_kernel(pt_ref, q_ref, *refs, pps, n_new, n_pages, lam_init):
    k_refs = refs[:pps]
    v_refs = refs[pps:2 * pps]
    kn_ref, vn_ref, l1_ref, l2_ref, l3_ref, l4_ref, og_ref, o_ref, s_scr, m_sc, l_sc, acc_sc = refs[2 * pps:]
    phase = pl.program_id(1)
    step_id = pl.program_id(2)
    nsteps = pl.num_programs(2)
    prow = s_scr.shape[2]
    nchunk = prow // LANES
    lane = lax.broadcasted_iota(I32, (SUBLANES, LANES), 1)
    lane_head = lane % N_HEADS
    page_head = lax.broadcasted_iota(I32, (SUBLANES, prow), 1) % N_HEADS
    row = lax.broadcasted_iota(I32, (SUBLANES, prow), 0)
    tile = lambda x: jnp.concatenate([x] * nchunk, axis=1)

    def lane_class_reduce(x, op):
        for sh in (8, 16, 32, 64):
            x = op(x, pltpu.roll(x, sh, 1))
        return x

    def score_pages(ks, first, valid):
        q = q_ref[...]
        parts = []
        for pi, kp in enumerate(ks):
            sf = _dot_nt(q, kp.astype(BF16))
            page = []
            for c in range(nchunk):
                blk = sf[:, c * LANES:(c + 1) * LANES]
                s = jnp.zeros((SUBLANES, LANES), F32)
                for h in range(N_HEADS):
                    s = jnp.where(lane_head == h, blk[h * SUBLANES:(h + 1) * SUBLANES, :], s)
                page.append(s)
            s_page = jnp.concatenate(page, axis=1)
            if valid is not None:
                s_page = jnp.where(valid, s_page, NEG)
                page = [s_page[:, c * LANES:(c + 1) * LANES] for c in range(nchunk)]
            s_scr[first + pi] = s_page
            parts += page
        m_old = m_sc[...]
        m_new = jnp.maximum(m_old, functools.reduce(jnp.maximum, parts))
        l_sc[...] = jnp.exp(m_old - m_new) * l_sc[...] + functools.reduce(jnp.add, [jnp.exp(s - m_new) for s in parts])
        m_sc[...] = m_new

    def value_pages(vs, first):
        lam = _lam_value(l1_ref, l2_ref, l3_ref, l4_ref, lam_init)
        m_full = tile(m_sc[...])
        l_full = tile(l_sc[...])
        pv = jnp.zeros(acc_sc.shape, F32)
        for pi, vp in enumerate(vs):
            e = jnp.exp(s_scr[first + pi] - m_full) / l_full
            a = jnp.where(row < n_new, e - lam * pltpu.roll(e, SUBLANES - n_new, 0), 0.0)
            pexp = jnp.concatenate([jnp.where(page_head == h, a, 0.0) for h in range(N_HEADS)], axis=0)
            pv = pv + _dot(pexp.astype(BF16), vp.astype(BF16))
        acc_sc[...] = acc_sc[...] + pv

    @pl.when(phase == 0)
    def _():
        @pl.when(step_id == 0)
        def _():
            m_sc[...] = jnp.full(m_sc.shape, NEG, F32)
            l_sc[...] = jnp.zeros(l_sc.shape, F32)

        score_pages([r[...] for r in k_refs], step_id * pps, None)

        @pl.when(step_id == nsteps - 1)
        def _():
            tok = lax.broadcasted_iota(I32, (SUBLANES, prow), 1) // N_HEADS
            score_pages([kn_ref[...]], n_pages, tok <= row % n_new)
            m_lane = m_sc[...]
            m_head = lane_class_reduce(m_lane, jnp.maximum)
            l_sc[...] = lane_class_reduce(l_sc[...] * jnp.exp(m_lane - m_head), jnp.add)
            m_sc[...] = m_head

    @pl.when(phase == 1)
    def _():
        @pl.when(step_id == 0)
        def _():
            acc_sc[...] = jnp.zeros(acc_sc.shape, F32)

        value_pages([r[...] for r in v_refs], step_id * pps)

        @pl.when(step_id == nsteps - 1)
        def _():
            value_pages([vn_ref[...]], n_pages)
            acc = acc_sc[...]
            outs = []
            for h in range(N_HEADS):
                o = acc[h * SUBLANES:(h + 1) * SUBLANES]
                o = o * lax.rsqrt(jnp.mean(o * o, axis=-1, keepdims=True) + EPS) * og_ref[...]
                outs.append(o * (1.0 - lam_init))
            o_ref[...] = jnp.concatenate(outs, axis=1).astype(BF16)


def _dattn_paged(qall, cache_k, cache_v, page_table, k_new, v_new, lams, og, n_new, lam_init):
    nb, n_pages = page_table.shape
    pps = PAGES_PER_STEP
    prow = cache_k.shape[1]
    nsteps = n_pages // pps
    kspec = lambda j: pl.BlockSpec(
        (None, prow, HEAD_W),
        lambda b, ph, s, pt, j=j: (pt[b * n_pages + (s * (1 - ph) + (nsteps - 1) * ph) * pps + j], 0, 0))
    vspec = lambda j: pl.BlockSpec(
        (None, prow, HEAD_W), lambda b, ph, s, pt, j=j: (pt[b * n_pages + s * ph * pps + j], 0, 0))
    lsp = pl.BlockSpec((1, DQK), lambda b, ph, s, pt: (0, 0))
    nspec = pl.BlockSpec((None, k_new.shape[1], HEAD_W), lambda b, ph, s, pt: (b, 0, 0))
    grid_spec = pltpu.PrefetchScalarGridSpec(
        num_scalar_prefetch=1,
        grid=(nb, 2, nsteps),
        in_specs=[pl.BlockSpec((None, N_HEADS * SUBLANES, HEAD_W), lambda b, ph, s, pt: (b, 0, 0))]
                 + [kspec(j) for j in range(pps)] + [vspec(j) for j in range(pps)]
                 + [nspec, nspec, lsp, lsp, lsp, lsp, pl.BlockSpec((1, HEAD_W), lambda b, ph, s, pt: (0, 0))],
        out_specs=pl.BlockSpec((SUBLANES, N_HEADS * HEAD_W), lambda b, ph, s, pt: (b, 0)),
        scratch_shapes=[pltpu.VMEM((n_pages + 1, SUBLANES, prow), F32),
                        pltpu.VMEM((SUBLANES, LANES), F32), pltpu.VMEM((SUBLANES, LANES), F32),
                        pltpu.VMEM((N_HEADS * SUBLANES, HEAD_W), F32)])
    return pl.pallas_call(
        functools.partial(_dattn_paged_kernel, pps=pps, n_new=n_new, n_pages=n_pages, lam_init=lam_init),
        grid_spec=grid_spec,
        out_shape=jax.ShapeDtypeStruct((nb * SUBLANES, N_HEADS * HEAD_W), BF16),
        compiler_params=_cparams(("arbitrary",) * 3,
                                 4 * pps * prow * HEAD_W * 4 + (n_pages + 1) * SUBLANES * prow * 4 + 24 * 2**20),
        name="diff_attn_paged",
    )(page_table.reshape(-1), qall, *([cache_k] * pps), *([cache_v] * pps), k_new, v_new,
      *[x.reshape(1, DQK) for x in lams], og.reshape(1, HEAD_W))


def _split3(a):
    hi = a.astype(BF16)
    r1 = a - hi.astype(F32)
    mid = r1.astype(BF16)
    lo = (r1 - mid.astype(F32)).astype(BF16)
    return hi, mid, lo


def _mlstm_kernel(qk_ref, v_ref, o_ref, gt_ref, cw_ref, cb_ref, gb_ref, og_ref,
                  C0_ref, n0_ref, m0_ref, cv0_ref,
                  h_ref, C_ref, n_ref, m_ref, cv_ref, xbuf, *, L, rows_in, n_valid):
    c = pl.program_id(1)
    width = N_HEADS * HEAD_W

    @pl.when(c == 0)
    def _():
        C_ref[...] = C0_ref[...]
        n_ref[...] = n0_ref[...]
        m_ref[...] = m0_ref[...]
        xbuf[0:SUBLANES, :] = cv0_ref[...]

    def padded(ref):
        x = ref[...]
        if rows_in < L:
            x = jnp.concatenate([x, jnp.zeros((L - rows_in, x.shape[1]), x.dtype)], axis=0)
        return x

    xbuf[SUBLANES:SUBLANES + L, :] = padded(qk_ref)
    taps = [xbuf[SUBLANES - (CONV_W - 1) + j:SUBLANES - (CONV_W - 1) + j + L, :] * cw_ref[j:j + 1, :]
            for j in range(CONV_W)]
    conv = cb_ref[...] + functools.reduce(jnp.add, taps)
    cv_ref[...] = xbuf[n_valid:n_valid + SUBLANES, :]
    xbuf[0:SUBLANES, :] = xbuf[L:L + SUBLANES, :]
    conv = _silu(conv)
    q_all = conv[:, :width].astype(BF16)
    k_all = conv[:, width:] * (HEAD_W ** -0.5)
    v_all = padded(v_ref).astype(BF16)
    o_all = padded(o_ref)

    g = padded(gt_ref) + gb_ref[...]
    lane = lax.broadcasted_iota(I32, (L, LANES), 1)
    rowi = lax.broadcasted_iota(I32, (L, LANES), 0)
    is_f = (lane >= N_HEADS) & (lane < 2 * N_HEADS)
    logsig = jnp.minimum(g, 0.0) - jnp.log1p(jnp.exp(-jnp.abs(g)))
    a = jnp.where(is_f, logsig, g)
    if n_valid < L:
        a = jnp.where(rowi < n_valid, a, jnp.where(is_f, 0.0, NEG))
    tri = (lax.broadcasted_iota(I32, (L, L), 1) <= lax.broadcasted_iota(I32, (L, L), 0))
    tri_b = tri.astype(BF16)
    cum = functools.reduce(jnp.add, [_dot(tri_b, part) for part in _split3(jnp.where(is_f, a, 0.0))])
    a_t = a.T
    cum_t = cum.T

    for h in range(N_HEADS):
        sl = slice(h * HEAD_W, (h + 1) * HEAD_W)
        b_col = cum[:, N_HEADS + h:N_HEADS + h + 1]
        b_row = cum_t[N_HEADS + h:N_HEADS + h + 1, :]
        i_col = a[:, h:h + 1]
        i_row = a_t[h:h + 1, :]
        m_prev = m_ref[h:h + 1, 0:1]
        d = jnp.where(tri, (b_col - b_row) + i_row, NEG)
        inter = b_col + m_prev
        mt = jnp.maximum(inter, jnp.max(d, axis=-1, keepdims=True))
        w_intra = jnp.exp(d - mt)
        w_inter = jnp.exp(inter - mt)
        qh, kh, vh = q_all[:, sl], k_all[:, sl], v_all[:, sl]
        C_h = C_ref[h]
        n_h = n_ref[h:h + 1, :]
        s = _dot_nt(qh, kh.astype(BF16)) * w_intra
        num = _dot(s.astype(BF16), vh) + w_inter * _dot(qh, C_h.astype(BF16))
        den = (jnp.sum(s, axis=-1, keepdims=True)
               + w_inter * jnp.sum(qh.astype(F32) * n_h.astype(BF16).astype(F32), axis=-1, keepdims=True))
        hh = num / jnp.maximum(jnp.abs(den), jnp.exp(-mt))
        m_last = mt[L - 1:L, :]
        b_last = b_col[L - 1:L, :]
        gk = jnp.exp((b_last - b_col) + i_col - m_last) * kh
        decay = jnp.exp(b_last + m_prev - m_last)
        C_ref[h] = decay * C_h + _dot_tn(gk.astype(BF16), vh)
        n_ref[h:h + 1, :] = decay * n_h + jnp.sum(gk, axis=0, keepdims=True)
        m_ref[h:h + 1, :] = jnp.broadcast_to(m_last, (1, LANES))
        hn = hh * lax.rsqrt(jnp.mean(hh * hh, axis=-1, keepdims=True) + EPS) * og_ref[...]
        out = hn * jax.nn.sigmoid(o_all[:, sl])
        h_ref[:, sl] = out[:rows_in].astype(BF16)


def _mlstm(proj, gates, conv_w, conv_b, gate_b, og, C0, n0, m0, cv0, nb, rows_in, nchunks, n_valid):
    L = MLSTM_L
    w = N_HEADS * HEAD_W
    row = lambda cb: (lambda b, c: (b * nchunks + c, cb))
    st3 = lambda b, c: (b, 0, 0)
    kern = functools.partial(_mlstm_kernel, L=L, rows_in=rows_in, n_valid=n_valid)
    return pl.pallas_call(
        kern,
        grid=(nb, nchunks),
        in_specs=[pl.BlockSpec((rows_in, 2 * w), lambda b, c: (b * nchunks + c, 0)),
                  pl.BlockSpec((rows_in, w), row(2)), pl.BlockSpec((rows_in, w), row(3)),
                  pl.BlockSpec((rows_in, LANES), lambda b, c: (b * nchunks + c, 0)),
                  pl.BlockSpec((SUBLANES, 2 * w), lambda b, c: (0, 0)),
                  pl.BlockSpec((1, 2 * w), lambda b, c: (0, 0)),
                  pl.BlockSpec((1, LANES), lambda b, c: (0, 0)),
                  pl.BlockSpec((1, HEAD_W), lambda b, c: (0, 0)),
                  pl.BlockSpec((None, N_HEADS, HEAD_W, HEAD_W), lambda b, c: (b, 0, 0, 0)),
                  pl.BlockSpec((None, N_HEADS, HEAD_W), st3),
                  pl.BlockSpec((None, N_HEADS, LANES), st3),
                  pl.BlockSpec((None, SUBLANES, 2 * w), st3)],
        out_specs=[pl.BlockSpec((rows_in, w), lambda b, c: (b * nchunks + c, 0)),
                   pl.BlockSpec((None, N_HEADS, HEAD_W, HEAD_W), lambda b, c: (b, 0, 0, 0)),
                   pl.BlockSpec((None, N_HEADS, HEAD_W), st3),
                   pl.BlockSpec((None, N_HEADS, LANES), st3),
                   pl.BlockSpec((None, SUBLANES, 2 * w), st3)],
        out_shape=[jax.ShapeDtypeStruct((nb * nchunks * rows_in, w), BF16),
                   jax.ShapeDtypeStruct((nb, N_HEADS, HEAD_W, HEAD_W), F32),
                   jax.ShapeDtypeStruct((nb, N_HEADS, HEAD_W), F32),
                   jax.ShapeDtypeStruct((nb, N_HEADS, LANES), F32),
                   jax.ShapeDtypeStruct((nb, SUBLANES, 2 * w), F32)],
        scratch_shapes=[pltpu.VMEM((SUBLANES + L, 2 * w), F32)],
        compiler_params=_cparams(("arbitrary", "arbitrary"), 40 * 2**20),
        name="mlstm_chunkwise",
    )(proj, proj, proj, gates, conv_w, conv_b, gate_b, og, C0, n0, m0, cv0)


def _outproj_kernel(hm_ref, ha_ref, x_ref, g1_ref, sc_ref, sh_ref, ng_ref, wo_ref, wr_ref, br_ref,
                    x1_ref, hn_ref, ids_ref, gates_ref):
    half = hm_ref.shape[1]
    mix = _dot(hm_ref[...], wo_ref[0:half, :]) + _dot(ha_ref[...], wo_ref[half:2 * half, :])
    x1 = x_ref[...] + g1_ref[...] * mix
    x1_ref[...] = x1
    y = x1 * lax.rsqrt(jnp.mean(x1 * x1, axis=-1, keepdims=True) + EPS) * ng_ref[...]
    hn = y * (1.0 + sc_ref[...]) + sh_ref[...]
    hn_ref[...] = hn
    logits = _dot(hn.astype(BF16), wr_ref[...]) + br_ref[...]

    lane = lax.broadcasted_iota(I32, logits.shape, 1).astype(F32)
    big = 1000.0
    is_g = lane < N_GROUPS
    gl = jnp.where(is_g, logits, NEG)
    gmax = jnp.max(gl, axis=-1, keepdims=True)
    g_idx = jnp.min(jnp.where(gl == gmax, lane, big), axis=-1, keepdims=True)
    g_p = 1.0 / jnp.sum(jnp.where(is_g, jnp.exp(gl - gmax), 0.0), axis=-1, keepdims=True)
    e_lo = N_GROUPS + EXPERTS_PER_GROUP * g_idx
    in_grp = (lane >= e_lo) & (lane < e_lo + EXPERTS_PER_GROUP)
    el = jnp.where(in_grp, logits, NEG)
    ex = jnp.where(in_grp, jnp.exp(el - jnp.max(el, axis=-1, keepdims=True)), 0.0)
    p = jnp.where(in_grp, ex / jnp.sum(ex, axis=-1, keepdims=True), -1.0)
    top1 = jnp.max(p, axis=-1, keepdims=True)
    idx1 = jnp.min(jnp.where(p == top1, lane, big), axis=-1, keepdims=True)
    p2 = jnp.where(lane == idx1, -1.0, p)
    top2 = jnp.max(p2, axis=-1, keepdims=True)
    idx2 = jnp.min(jnp.where(p2 == top2, lane, big), axis=-1, keepdims=True)
    tsum = top1 + top2
    ids_ref[...] = jnp.where(lane == 0.0, idx1 - N_GROUPS,
                             jnp.where(lane == 1.0, idx2 - N_GROUPS, -1.0)).astype(I32)
    gates_ref[...] = jnp.where(lane == 0.0, g_p * top1 / tsum, jnp.where(lane == 1.0, g_p * top2 / tsum, 0.0))


def _outproj(hm, ha, x, mod_specs, mods, ng, wo, wr, br, tm):
    m, d = x.shape
    half = hm.shape[1]
    row = lambda wdt: pl.BlockSpec((tm, wdt), lambda i: (i, 0))
    const = lambda shp: pl.BlockSpec(shp, lambda i: (0, 0))
    return pl.pallas_call(
        _outproj_kernel,
        grid=(m // tm,),
        in_specs=[row(half), row(half), row(d), *mod_specs, const((1, d)), const((d, d)),
                  const((d, LANES)), const((1, LANES))],
        out_specs=[row(d), row(d), row(LANES), row(LANES)],
        out_shape=[jax.ShapeDtypeStruct((m, d), F32), jax.ShapeDtypeStruct((m, d), F32),
                   jax.ShapeDtypeStruct((m, LANES), I32), jax.ShapeDtypeStruct((m, LANES), F32)],
        compiler_params=_cparams(("arbitrary",), 2 * d * d * 2 + 8 * tm * d * 4 + 8 * 2**20),
        name="outproj_norm2_route",
    )(hm, ha, x, *mods, ng.reshape(1, d), wo, wr, br)


def _rank_kernel(ids_ref, dest_ref, cnt_ref, carry, pstart):
    pss = pl.program_id(0)
    i = pl.program_id(1)
    ids = ids_ref[...]
    tm = ids.shape[0]
    lane = lax.broadcasted_iota(I32, ids.shape, 1)
    o0 = lane == ids[:, 0:1]
    o1 = lane == ids[:, 1:2]
    onehot = jnp.where(o0 | o1, 1.0, 0.0)
    col_counts = jnp.sum(onehot, axis=0, keepdims=True)

    @pl.when((pss == 0) & (i == 0))
    def _():
        carry[...] = jnp.zeros(carry.shape, F32)

    @pl.when(pss == 0)
    def _():
        carry[...] = carry[...] + col_counts

    @pl.when((pss == 1) & (i == 0))
    def _():
        cnt = carry[...]
        cnt_ref[...] = cnt
        blocks = jnp.floor((cnt + (MOE_ROWS - 1)) * (1.0 / MOE_ROWS))
        earlier = lax.broadcasted_iota(I32, (LANES, LANES), 0) < lax.broadcasted_iota(I32, (LANES, LANES), 1)
        pstart[...] = _dot(blocks.astype(BF16), earlier.astype(BF16)) * MOE_ROWS
        carry[...] = jnp.zeros(carry.shape, F32)

    @pl.when(pss == 1)
    def _():
        strict = lax.broadcasted_iota(I32, (tm, tm), 1) < lax.broadcasted_iota(I32, (tm, tm), 0)
        before = _dot(strict.astype(BF16), onehot.astype(BF16)) + carry[0:1, :] + pstart[0:1, :]
        r0 = jnp.sum(jnp.where(o0, before, 0.0), axis=-1, keepdims=True)
        r1 = jnp.sum(jnp.where(o1, before, 0.0), axis=-1, keepdims=True)
        dest_ref[...] = jnp.where(lane == 0, r0, jnp.where(lane == 1, r1, 0.0)).astype(I32)
        carry[...] = carry[...] + col_counts


def _ranks(ids, tm):
    m = ids.shape[0]
    return pl.pallas_call(
        _rank_kernel,
        grid=(2, m // tm),
        in_specs=[pl.BlockSpec((tm, LANES), lambda p, i: (i, 0))],
        out_specs=[pl.BlockSpec((tm, LANES), lambda p, i: (i * p, 0)),
                   pl.BlockSpec((SUBLANES, LANES), lambda p, i: (0, 0))],
        out_shape=[jax.ShapeDtypeStruct((m, LANES), I32), jax.ShapeDtypeStruct((SUBLANES, LANES), F32)],
        scratch_shapes=[pltpu.VMEM((SUBLANES, LANES), F32), pltpu.VMEM((SUBLANES, LANES), F32)],
        compiler_params=_cparams(("arbitrary", "arbitrary"), 16 * 2**20),
        name="moe_ranks",
    )(ids)


def _row_copies(idx_ref, t0, tm, copy):
    def body(t, _):
        for k in range(2):
            copy(t, k, idx_ref[2 * (t0 + t) + k]).start()
        return 0
    lax.fori_loop(0, tm, body, 0)


def _dispatch_kernel(dest_ref, h_ref, xin_ref, xs_ref, sem, *, tm, base):
    del xin_ref
    t0 = base + pl.program_id(0) * tm
    _row_copies(dest_ref, t0, tm,
                lambda t, k, d: pltpu.make_async_copy(h_ref.at[pl.ds(t, 1)], xs_ref.at[pl.ds(d, 1)], sem))
    for _ in range(2):
        pltpu.make_async_copy(h_ref, xs_ref.at[pl.ds(0, tm)], sem).wait()


def _dispatch(dest_flat, hn, xs, tm, base):
    m, d = hn.shape
    any_spec = pl.BlockSpec(memory_space=pl.ANY)
    grid_spec = pltpu.PrefetchScalarGridSpec(
        num_scalar_prefetch=1, grid=(m // tm,),
        in_specs=[pl.BlockSpec((tm, d), lambda i, dr: (i, 0)), any_spec], out_specs=any_spec,
        scratch_shapes=[pltpu.SemaphoreType.DMA])
    return pl.pallas_call(
        functools.partial(_dispatch_kernel, tm=tm, base=base),
        grid_spec=grid_spec,
        out_shape=jax.ShapeDtypeStruct(xs.shape, xs.dtype),
        input_output_aliases={2: 0},
        compiler_params=_cparams(("arbitrary",), 4 * tm * d * 4 + 8 * 2**20),
        name="moe_dispatch",
    )(dest_flat, hn, xs)


def _expert_weights(b, be_ref, bf_ref, nx_ref, w_hbm, w_stage, w_bf16, sem):
    def copies(e):
        return [pltpu.make_async_copy(src.at[0, e], dst, sem.at[i])
                for i, (src, dst) in enumerate(zip(w_hbm, w_stage))]

    @pl.when(b == 0)
    def _():
        for c in copies(be_ref[0]):
            c.start()

    @pl.when(bf_ref[b] == 1)
    def _():
        for c in copies(be_ref[b]):
            c.wait()
        for stage, wb in zip(w_stage, w_bf16):
            wb[...] = stage[...].astype(BF16)

        @pl.when(nx_ref[b] >= 0)
        def _():
            for c in copies(nx_ref[b]):
                c.start()


def _expert_up_kernel(be_ref, bf_ref, nx_ref, nv_ref, x_ref, w1_hbm, w3_hbm, h_ref, w1f, w3f, w1b, w3b, sem):
    b = pl.program_id(0)
    _expert_weights(b, be_ref, bf_ref, nx_ref, (w1_hbm, w3_hbm), (w1f, w3f), (w1b, w3b), sem)

    @pl.when(b < nv_ref[0])
    def _():
        x = x_ref[...].astype(BF16)
        h_ref[...] = (_silu(_dot(x, w1b[...])) * _dot(x, w3b[...])).astype(BF16)

    @pl.when(b >= nv_ref[0])
    def _():
        h_ref[...] = jnp.zeros(h_ref.shape, BF16)


def _expert_up(tabs, xs, w1, w3):
    rows, d = xs.shape
    f = w1.shape[-1]
    nb = rows // MOE_ROWS
    any_spec = pl.BlockSpec(memory_space=pl.ANY)
    grid_spec = pltpu.PrefetchScalarGridSpec(
        num_scalar_prefetch=4, grid=(nb,),
        in_specs=[pl.BlockSpec((MOE_ROWS, d), lambda b, *_: (b, 0)), any_spec, any_spec],
        out_specs=pl.BlockSpec((MOE_ROWS, f), lambda b, *_: (b, 0)),
        scratch_shapes=[pltpu.VMEM((d, f), F32), pltpu.VMEM((d, f), F32),
                        pltpu.VMEM((d, f), BF16), pltpu.VMEM((d, f), BF16), pltpu.SemaphoreType.DMA((2,))])
    return pl.pallas_call(
        _expert_up_kernel,
        grid_spec=grid_spec,
        out_shape=jax.ShapeDtypeStruct((rows, f), BF16),
        compiler_params=_cparams(("arbitrary",), 2 * d * f * 6 + 4 * MOE_ROWS * d * 4 + 8 * 2**20),
        name="moe_expert_up",
    )(*tabs, xs, w1, w3)


def _expert_down_kernel(be_ref, bf_ref, nx_ref, nv_ref, h_ref, w2_hbm, y_ref, w2f, w2b, sem):
    b = pl.program_id(0)
    _expert_weights(b, be_ref, bf_ref, nx_ref, (w2_hbm,), (w2f,), (w2b,), sem)

    @pl.when(b < nv_ref[0])
    def _():
        y_ref[...] = _dot(h_ref[...], w2b[...])

    @pl.when(b >= nv_ref[0])
    def _():
        y_ref[...] = jnp.zeros(y_ref.shape, F32)


def _expert_down(tabs, hs, w2):
    rows, f = hs.shape
    d = w2.shape[-1]
    nb = rows // MOE_ROWS
    grid_spec = pltpu.PrefetchScalarGridSpec(
        num_scalar_prefetch=4, grid=(nb,),
        in_specs=[pl.BlockSpec((MOE_ROWS, f), lambda b, *_: (b, 0)), pl.BlockSpec(memory_space=pl.ANY)],
        out_specs=pl.BlockSpec((MOE_ROWS, d), lambda b, *_: (b, 0)),
        scratch_shapes=[pltpu.VMEM((f, d), F32), pltpu.VMEM((f, d), BF16), pltpu.SemaphoreType.DMA((1,))])
    return pl.pallas_call(
        _expert_down_kernel,
        grid_spec=grid_spec,
        out_shape=jax.ShapeDtypeStruct((rows, d), F32),
        compiler_params=_cparams(("arbitrary",), f * d * 6 + 6 * MOE_ROWS * d * 4 + 8 * 2**20),
        name="moe_expert_down",
    )(*tabs, hs, w2)


def _combine_kernel(dest_ref, x1_ref, g2_ref, gates_ref, y_ref, o_ref, ybuf, sem, *, tm, base):
    t0 = base + pl.program_id(0) * tm
    _row_copies(dest_ref, t0, tm,
                lambda t, k, d: pltpu.make_async_copy(y_ref.at[pl.ds(d, 1)], ybuf.at[k, pl.ds(t, 1)], sem))
    for k in range(2):
        pltpu.make_async_copy(y_ref.at[pl.ds(0, tm)], ybuf.at[k], sem).wait()
    gates = gates_ref[...]
    ff = gates[:, 0:1] * ybuf[0] + gates[:, 1:2] * ybuf[1]
    o_ref[...] = x1_ref[...] + g2_ref[...] * ff


def _combine(dest_flat, x1, g2_spec, g2, gates, y, tm, base):
    m, d = x1.shape
    grid_spec = pltpu.PrefetchScalarGridSpec(
        num_scalar_prefetch=1, grid=(m // tm,),
        in_specs=[pl.BlockSpec((tm, d), lambda i, dr: (i, 0)), g2_spec,
                  pl.BlockSpec((tm, LANES), lambda i, dr: (i, 0)),
                  pl.BlockSpec(memory_space=pl.ANY)],
        out_specs=pl.BlockSpec((tm, d), lambda i, dr: (i, 0)),
        scratch_shapes=[pltpu.VMEM((2, tm, d), F32), pltpu.SemaphoreType.DMA])
    return pl.pallas_call(
        functools.partial(_combine_kernel, tm=tm, base=base),
        grid_spec=grid_spec,
        out_shape=jax.ShapeDtypeStruct((m, d), F32),
        compiler_params=_cparams(("arbitrary",), 6 * tm * d * 4 + 8 * 2**20),
        name="moe_combine",
    )(dest_flat, x1, g2, gates, y)


def kernel(x_prompt, x_sample, cache_k, cache_v, page_table, state_mlstm_C, state_mlstm_n, state_mlstm_m, state_conv,
           c_prompt, c_sample, w_ada, b_ada, norm1_g, norm2_g, w_in, conv_w, conv_b, b_igate, b_fgate, mlstm_out_g,
           q_norm_g, k_norm_g, lam_q1, lam_k1, lam_q2, lam_k2, diff_out_g, w_out, w_grp, b_grp, w_rt, b_rt, w1, w3, w2):
    B, T, D = x_prompt.shape
    DB, TS, _ = x_sample.shape
    depth = w_ada.shape[0]
    assert depth == 1, "single-layer trunk"
    n_pages, page = page_table.shape[1], cache_k.shape[2]
    past = n_pages * page
    W = N_HEADS * HEAD_W
    lam_init = 0.8 - 0.6 * math.exp(-0.3 * 0)
    NP, NS = B * T, DB * TS
    SP = SAMPLE_PAD

    c_all = jnp.concatenate([c_prompt, c_sample, jnp.zeros((16 - B - DB, D), F32)], axis=0)
    wi = w_in[0]
    gate_lo = 4 * W
    w_main = jnp.concatenate([wi[:, :gate_lo].astype(BF16), wi[:, gate_lo + 2 * N_HEADS:].astype(BF16)], axis=1)
    w_gate = jnp.pad(wi[:, gate_lo:gate_lo + 2 * N_HEADS], ((0, 0), (0, LANES - 2 * N_HEADS))).astype(BF16)
    gate_b = jnp.pad(jnp.concatenate([b_igate[0], b_fgate[0]]), (0, LANES - 2 * N_HEADS)).reshape(1, LANES)
    cw = jnp.pad(conv_w[0], ((0, SUBLANES - CONV_W), (0, 0)))
    cb = conv_b[0].reshape(1, 2 * W)
    og_m = mlstm_out_g[0].reshape(1, HEAD_W)
    wo = w_out[0].astype(BF16)
    w_route = jnp.concatenate([w_grp[0], w_rt[0].transpose(1, 0, 2).reshape(D, N_EXPERTS)], axis=1)
    w_route = jnp.pad(w_route, ((0, 0), (0, LANES - N_GROUPS - N_EXPERTS))).astype(BF16)
    b_route = jnp.pad(jnp.concatenate([b_grp[0], b_rt[0].reshape(-1)]), (0, LANES - N_GROUPS - N_EXPERTS)).reshape(1, LANES)
    lams = (lam_q1[0], lam_k1[0], lam_q2[0], lam_k2[0])

    mod = _ada(c_all, w_ada[0], b_ada[0])
    mod3 = mod.reshape(16, 1, 6 * D)
    mod_s = jnp.repeat(mod[B:B + DB], SP, axis=0)

    def p_mod(chunk, tm):
        return pl.BlockSpec((None, 1, D), lambda i, *_: ((i * tm) // T, 0, chunk))

    def s_mod(chunk, tm):
        return pl.BlockSpec((tm, D), lambda i, *_: (i, chunk))

    xp = x_prompt.reshape(NP, D)
    tm_in = 512
    proj_p, gates_p = _inproj(xp, (p_mod(1, tm_in), p_mod(0, tm_in)), (mod3, mod3), norm1_g[0], w_main, w_gate, tm_in)
    cos_p, sin_p = _rope_tables(jnp.arange(T, dtype=I32))
    _, knew_p, kb_p, vnew_p, _, qt_p, vt_p = _qkrope(proj_p, 4, cos_p, sin_p, q_norm_g[0], k_norm_g[0], 256, B)
    ha_p = _dattn_prompt(qt_p, kb_p, vt_p, lams, diff_out_g[0], lam_init)
    zeros = lambda *s: jnp.zeros(s, F32)
    hm_p, C_p, n_p, m_p, cv_p = _mlstm(
        proj_p, gates_p, cw, cb, gate_b, og_m,
        zeros(B, N_HEADS, HEAD_W, HEAD_W), zeros(B, N_HEADS, HEAD_W), zeros(B, N_HEADS, LANES),
        zeros(B, SUBLANES, 2 * W), B, MLSTM_L, T // MLSTM_L, MLSTM_L)
    tm_o = 256
    x1_p, hn2_p, ids_p, gts_p = _outproj(
        hm_p, ha_p, xp, (p_mod(2, tm_o), p_mod(4, tm_o), p_mod(3, tm_o)), (mod3, mod3, mod3),
        norm2_g[0], wo, w_route, b_route, tm_o)

    MS = DB * SP
    xs_pad = jnp.pad(x_sample, ((0, 0), (0, SP - TS), (0, 0))).reshape(MS, D)
    proj_s, gates_s = _inproj(xs_pad, (s_mod(1, MS), s_mod(0, MS)), (mod_s, mod_s), norm1_g[0], w_main, w_gate, MS)
    pos_s = past + (jnp.arange(MS, dtype=I32) % SP)
    cos_s, sin_s = _rope_tables(pos_s)
    q_s, knew_s, kb_s, vnew_s, vb_s = _qkrope(proj_s, 4, cos_s, sin_s, q_norm_g[0], k_norm_g[0], MS)
    qr = q_s.reshape(DB, SP, N_HEADS, 2, DQK)[:, :TS].transpose(0, 2, 3, 1, 4)
    zq = jnp.zeros_like(qr[:, :, 0])
    qall = jnp.stack([jnp.concatenate([qr[:, :, 0], zq], axis=-1),
                      jnp.concatenate([zq, qr[:, :, 1]], axis=-1)], axis=2).reshape(DB, N_HEADS * 2 * TS, HEAD_W)
    new_rows = page * N_HEADS
    pad_new = lambda a: jnp.pad(a.reshape(DB, SP * N_HEADS, HEAD_W), ((0, 0), (0, new_rows - SP * N_HEADS), (0, 0)))
    ha_s = _dattn_paged(qall, cache_k[0].reshape(-1, new_rows, HEAD_W), cache_v[0].reshape(-1, new_rows, HEAD_W),
                        page_table, pad_new(kb_s), pad_new(vb_s), lams, diff_out_g[0], TS, lam_init)
    m0_s = jnp.broadcast_to(state_mlstm_m[0][:, :, None], (DB, N_HEADS, LANES))
    cv0_s = jnp.pad(state_conv[0], ((0, 0), (SUBLANES - (CONV_W - 1), 0), (0, 0)))
    hm_s, C_s, n_s, m_s, cv_s = _mlstm(proj_s, gates_s, cw, cb, gate_b, og_m, state_mlstm_C[0], state_mlstm_n[0],
                                       m0_s, cv0_s, DB, SP, 1, TS)
    x1_s, hn2_s, ids_s, gts_s = _outproj(
        hm_s, ha_s, xs_pad, (s_mod(2, MS), s_mod(4, MS), s_mod(3, MS)), (mod_s, mod_s, mod_s),
        norm2_g[0], wo, w_route, b_route, MS)
    real = lambda a: a.reshape(DB, SP, -1)[:, :TS].reshape(NS, -1)
    x1_s, hn2_s, ids_s, gts_s = real(x1_s), real(hn2_s), real(ids_s), real(gts_s)

    NT = NP + NS
    tm_r = 256
    n_rank = -(-NT // tm_r) * tm_r
    ids_all = jnp.concatenate([ids_p, ids_s, jnp.full((n_rank - NT, LANES), -1, I32)], axis=0)
    dest2, cnt = _ranks(ids_all, tm_r)
    dest = dest2[:NT, :2].reshape(-1)
    counts = cnt[0, :N_EXPERTS].astype(I32)
    pend = jnp.cumsum((counts + MOE_ROWS - 1) // MOE_ROWS)
    n_blocks = (2 * NT + N_EXPERTS * (MOE_ROWS - 1)) // MOE_ROWS
    n_valid = pend[-1:]
    blk = jnp.minimum(jnp.arange(n_blocks, dtype=I32), n_valid[0] - 1)
    blk_e = jnp.sum((blk[:, None] >= pend[None, :]).astype(I32), axis=1)
    blk_first = jnp.concatenate([jnp.ones((1,), I32), (blk_e[1:] != blk_e[:-1]).astype(I32)])
    first_pos = jnp.where(blk_first == 1, jnp.arange(n_blocks, dtype=I32), n_blocks)
    next_first = jnp.concatenate([lax.cummin(first_pos, reverse=True)[1:], jnp.full((1,), n_blocks, I32)])
    blk_next = jnp.where(next_first < n_blocks, blk_e[jnp.minimum(next_first, n_blocks - 1)], -1).astype(I32)
    tabs = (blk_e, blk_first, blk_next, n_valid)
    xs_sorted = _dispatch(dest, hn2_p, jnp.zeros((n_blocks * MOE_ROWS, D), F32), 256, 0)
    xs_sorted = _dispatch(dest, hn2_s, xs_sorted, NS, NP)
    h_sorted = _expert_up(tabs, xs_sorted, w1, w3)
    y_sorted = _expert_down(tabs, h_sorted, w2)
    tm_c = 128
    y_p = _combine(dest, x1_p, pl.BlockSpec((None, 1, D), lambda i, dr: ((i * tm_c) // T, 0, 5)), mod3, gts_p,
                   y_sorted, tm_c, 0)
    g2_s = jnp.repeat(mod[B:B + DB, 5 * D:], TS, axis=0)
    y_s = _combine(dest, x1_s, pl.BlockSpec((NS, D), lambda i, dr: (0, 0)), g2_s, gts_s, y_sorted, NS, NP)

    st = lambda a: a[None]
    k5 = lambda a, nb, t: a.reshape(1, nb, t, N_HEADS, HEAD_W)
    tok_s = lambda a: a.reshape(DB, SP, -1)[:, :TS]
    conv_out = lambda cv: cv[None, :, SUBLANES - (CONV_W - 1):, :]
    return (y_p.reshape(B, T, D), y_s.reshape(DB, TS, D),
            k5(knew_p, B, T), k5(vnew_p, B, T), st(C_p), st(n_p), st(m_p[:, :, 0]), conv_out(cv_p),
            k5(tok_s(knew_s), DB, TS), k5(tok_s(vnew_s), DB, TS), st(C_s), st(n_s), st(m_s[:, :, 0]), conv_out(cv_s))
```

```python
import functools
import math

import jax
import jax.numpy as jnp
from jax import lax
from jax.experimental import pallas as pl
from jax.experimental.pallas import tpu as pltpu

F32 = jnp.float32
BF16 = jnp.bfloat16
I32 = jnp.int32

EPS = 1e-6
ROPE_THETA = 10000.0
NEG = -1e30

LANES = 128
SUBLANES = 8
VMEM_LIMIT_CAP = 56 * 1024 * 1024

N_HEADS = 8
HEAD_W = 128
DQK = 64
N_GROUPS = 4
EXPERTS_PER_GROUP = 8
N_EXPERTS = N_GROUPS * EXPERTS_PER_GROUP
CONV_W = 4
MOE_ROWS = 256
SAMPLE_PAD = 8
MLSTM_L = 128
PAGES_PER_STEP = 16


def _cparams(sem, vmem_bytes):
    return pltpu.CompilerParams(dimension_semantics=sem,
                                vmem_limit_bytes=int(min(max(vmem_bytes, 16 * 2**20), VMEM_LIMIT_CAP)))


def _silu(x):
    return x * jax.nn.sigmoid(x)


def _dot(a, b):
    return jnp.dot(a, b, preferred_element_type=F32)


def _dot_nt(a, b):
    return lax.dot_general(a, b, (((1,), (1,)), ((), ())), preferred_element_type=F32)


def _dot_tn(a, b):
    return lax.dot_general(a, b, (((0,), (0,)), ((), ())), preferred_element_type=F32)


def _ada_kernel(c_ref, w_ref, b_ref, o_ref):
    s = _silu(c_ref[...]).astype(BF16)
    o_ref[...] = _dot(s, w_ref[...].astype(BF16)) + b_ref[...]


def _ada(c_all, w_ada, b_ada):
    rows, d = c_all.shape
    n = w_ada.shape[1]
    tn = 1024
    return pl.pallas_call(
        _ada_kernel,
        grid=(n // tn,),
        in_specs=[pl.BlockSpec((rows, d), lambda j: (0, 0)),
                  pl.BlockSpec((d, tn), lambda j: (0, j)),
                  pl.BlockSpec((1, tn), lambda j: (0, j))],
        out_specs=pl.BlockSpec((rows, tn), lambda j: (0, j)),
        out_shape=jax.ShapeDtypeStruct((rows, n), F32),
        compiler_params=_cparams(("arbitrary",), 2 * d * tn * 4 + 3 * d * tn * 2 + 2**22),
        name="ada_mod",
    )(c_all, w_ada, b_ada.reshape(1, n))


def _regroup_kernel(w_ref, main_ref, gate_ref, *, lo, ng):
    x = w_ref[...]
    main_ref[...] = jnp.concatenate([x[:, :lo], x[:, lo + ng:]], axis=1).astype(BF16)
    gate = jnp.concatenate([x[:, lo:lo + ng], jnp.zeros((x.shape[0], LANES - ng), F32)], axis=1)
    gate_ref[...] = gate.astype(BF16)


def _regroup_w_in(w, lo, ng):
    d, n = w.shape
    tr = 256
    return pl.pallas_call(
        functools.partial(_regroup_kernel, lo=lo, ng=ng),
        grid=(d // tr,),
        in_specs=[pl.BlockSpec((tr, n), lambda i: (i, 0))],
        out_specs=[pl.BlockSpec((tr, n - ng), lambda i: (i, 0)), pl.BlockSpec((tr, LANES), lambda i: (i, 0))],
        out_shape=[jax.ShapeDtypeStruct((d, n - ng), BF16), jax.ShapeDtypeStruct((d, LANES), BF16)],
        compiler_params=_cparams(("arbitrary",), 6 * tr * n * 4 + 2**22),
        name="w_in_layout",
    )(w)


def _inproj_kernel(x_ref, sc_ref, sh_ref, g_ref, w_ref, wg_ref, o_ref, og_ref, hn_ref):
    @pl.when(pl.program_id(1) == 0)
    def _():
        x = x_ref[...]
        y = x * lax.rsqrt(jnp.mean(x * x, axis=-1, keepdims=True) + EPS) * g_ref[...]
        hb = (y * (1.0 + sc_ref[...]) + sh_ref[...]).astype(BF16)
        hn_ref[...] = hb
        og_ref[...] = _dot(hb, wg_ref[...])

    o_ref[...] = _dot(hn_ref[...], w_ref[...])


def _inproj(x, mod_specs, mods, g, w, wg, tm):
    m, d = x.shape
    n = w.shape[1]
    tn = 1024
    sc_spec, sh_spec = mod_specs
    return pl.pallas_call(
        _inproj_kernel,
        grid=(m // tm, n // tn),
        in_specs=[pl.BlockSpec((tm, d), lambda i, j: (i, 0)), sc_spec, sh_spec,
                  pl.BlockSpec((1, d), lambda i, j: (0, 0)),
                  pl.BlockSpec((d, tn), lambda i, j: (0, j)),
                  pl.BlockSpec((d, LANES), lambda i, j: (0, 0))],
        out_specs=[pl.BlockSpec((tm, tn), lambda i, j: (i, j)),
                   pl.BlockSpec((tm, LANES), lambda i, j: (i, 0))],
        out_shape=[jax.ShapeDtypeStruct((m, n), F32), jax.ShapeDtypeStruct((m, LANES), F32)],
        scratch_shapes=[pltpu.VMEM((tm, d), BF16)],
        compiler_params=_cparams(("arbitrary", "arbitrary"),
                                 2 * tm * d * 4 + 2 * d * tn * 2 + 2 * tm * tn * 4 + 3 * tm * d * 4 + 2**22),
        name="norm1_inproj",
    )(x, mods[0], mods[1], g.reshape(1, d), w, wg)


def _qkrope_kernel(q_ref, k_ref, v_ref, cos_ref, sin_ref, qg_ref, kg_ref, ko_ref, kb_ref, vo_ref, qx_ref, vx_ref,
                   *, transposed):
    cos = cos_ref[...]
    sin = sin_ref[...]
    lane = lax.broadcasted_iota(I32, (1, LANES), 1)
    first_map = lane < DQK
    first_half = (lane % DQK) < (DQK // 2)

    def norm_rope(x, g):
        x2 = x * x
        s1 = jnp.sum(jnp.where(first_map, x2, 0.0), axis=-1, keepdims=True)
        s2 = jnp.sum(jnp.where(first_map, 0.0, x2), axis=-1, keepdims=True)
        inv = jnp.where(first_map, lax.rsqrt(s1 / DQK + EPS), lax.rsqrt(s2 / DQK + EPS))
        y = x * inv * g
        partner = jnp.where(first_half, pltpu.roll(y, LANES - DQK // 2, 1), pltpu.roll(y, DQK // 2, 1))
        return y * cos + partner * sin

    for h in range(N_HEADS):
        sl = slice(h * HEAD_W, (h + 1) * HEAD_W)
        qh = norm_rope(q_ref[:, sl], qg_ref[...]) * (DQK ** -0.5)
        kh = norm_rope(k_ref[:, sl], kg_ref[...])
        vh = v_ref[:, sl]
        ko_ref[:, sl] = kh
        kb_ref[:, sl] = kh.astype(BF16)
        vo_ref[:, sl] = vh
        if transposed:
            qx_ref[sl, :] = qh.T.astype(BF16)
            vx_ref[sl, :] = vh.T.astype(BF16)
        else:
            qx_ref[:, sl] = qh.astype(BF16)
            vx_ref[:, sl] = vh.astype(BF16)


def _qkrope(proj, col0, cos_t, sin_t, qg, kg, tm, transposed_batches=0):
    m = proj.shape[0]
    w = N_HEADS * HEAD_W
    nt = cos_t.shape[0] // tm
    blk = lambda c: pl.BlockSpec((tm, w), lambda i, c=c: (i, c))
    tab = pl.BlockSpec((tm, LANES), lambda i: (i % nt, 0))
    gsp = pl.BlockSpec((1, LANES), lambda i: (0, 0))
    osp = pl.BlockSpec((tm, w), lambda i: (i, 0))
    g2 = lambda g: jnp.tile(g, 2).reshape(1, LANES)
    out_specs = [osp] * 3
    out_shape = [jax.ShapeDtypeStruct((m, w), F32), jax.ShapeDtypeStruct((m, w), BF16),
                 jax.ShapeDtypeStruct((m, w), F32)]
    if transposed_batches:
        out_specs += [pl.BlockSpec((None, None, w, tm), lambda i: (i // nt, i % nt, 0, 0))] * 2
        out_shape += [jax.ShapeDtypeStruct((transposed_batches, nt, w, tm), BF16)] * 2
    else:
        out_specs += [osp] * 2
        out_shape += [jax.ShapeDtypeStruct((m, w), BF16)] * 2
    return pl.pallas_call(
        functools.partial(_qkrope_kernel, transposed=bool(transposed_batches)),
        grid=(m // tm,),
        in_specs=[blk(col0), blk(col0 + 1), blk(col0 + 2), tab, tab, gsp, gsp],
        out_specs=out_specs,
        out_shape=out_shape,
        compiler_params=_cparams(("arbitrary",), 2 * tm * w * (3 * 4 + 2 * 4 + 5 * 2) + 2**22),
        name="qknorm_rope",
    )(proj, proj, proj, cos_t, sin_t, g2(qg), g2(kg))


def _rope_tables(pos):
    half = DQK // 2
    inv = ROPE_THETA ** (-jnp.arange(half, dtype=F32) / half)
    ang = pos.astype(F32)[:, None] * inv[None, :]
    cos = jnp.tile(jnp.cos(ang), (1, LANES // half))
    sin = jnp.sin(ang)
    sin = jnp.tile(jnp.concatenate([-sin, sin], axis=1), (1, LANES // DQK))
    return cos, sin


def _lam_value(l1, l2, l3, l4, lam_init):
    a = jnp.sum(l1[...] * l2[...], axis=-1, keepdims=True)
    b = jnp.sum(l3[...] * l4[...], axis=-1, keepdims=True)
    return jnp.exp(a) - jnp.exp(b) + lam_init


def _dattn_kernel(qt_ref, k_ref, vt_ref, l1_ref, l2_ref, l3_ref, l4_ref, og_ref, o_ref,
                  s_a, s_b, p_a, p_b, cm_ref, stat_ref, acc_ref, *, tq, lam_init):
    qi = pl.program_id(2)
    tk = tq // 2
    s_bufs, p_bufs = (s_a, s_b), (p_a, p_b)
    qt = jnp.concatenate([qt_ref[0], qt_ref[1]], axis=1)
    feat = lax.broadcasted_iota(I32, (HEAD_W, 1), 0)
    zero = jnp.zeros_like(qt)
    qcat = jnp.concatenate([jnp.where(feat < DQK, qt, zero), jnp.where(feat < DQK, zero, qt)], axis=1)
    key = lax.broadcasted_iota(I32, (tk, 2 * tq), 0)
    qry = lax.broadcasted_iota(I32, (tk, 2 * tq), 1) % tq

    def set_scores(slot, s, mask):
        if mask is not None:
            s = jnp.where(mask, s, NEG)
        s_bufs[slot][...] = s
        cm_ref[slot:slot + 1, :] = jnp.max(s, axis=0, keepdims=True)

    def scores_into(j, slot, mask=None):
        off = pl.multiple_of(j * tk, tk)
        set_scores(slot, _dot(k_ref[pl.ds(off, tk), :], qcat), mask)

    def softmax_pv(j, slot):
        pv = _dot(vt_ref[jnp.maximum(j - 1, 0)], p_bufs[1 - slot][...])
        m = stat_ref[0:1, :]
        m_new = jnp.maximum(m, cm_ref[slot:slot + 1, :])
        alpha = jnp.exp(m - m_new)
        p = jnp.exp(s_bufs[slot][...] - m_new)
        stat_ref[0:1, :] = m_new
        stat_ref[1:2, :] = alpha * stat_ref[1:2, :] + jnp.sum(p, axis=0, keepdims=True)
        p_bufs[slot][...] = p.astype(BF16)
        acc_ref[...] = alpha * (acc_ref[...] + pv)

    stat_ref[0:1, :] = jnp.full((1, 2 * tq), NEG, F32)
    stat_ref[1:2, :] = jnp.zeros((1, 2 * tq), F32)
    acc_ref[...] = jnp.zeros(acc_ref.shape, F32)
    p_b[...] = jnp.zeros(p_b.shape, BF16)
    scores_into(0, 0)

    def body(t, _):
        scores_into(2 * t + 1, 1)
        softmax_pv(2 * t, 0)
        scores_into(2 * t + 2, 0)
        softmax_pv(2 * t + 1, 1)
        return 0

    lax.fori_loop(0, qi, body, 0)
    set_scores(0, s_a[...], key <= qry)
    scores_into(2 * qi + 1, 1, key + tk <= qry)
    softmax_pv(2 * qi, 0)
    softmax_pv(2 * qi + 1, 1)
    acc = (acc_ref[...] + _dot(vt_ref[2 * qi + 1], p_b[...])) / stat_ref[1:2, :]
    lam = _lam_value(l1_ref, l2_ref, l3_ref, l4_ref, lam_init)
    o = (acc[:, :tq] - lam * acc[:, tq:]).T
    o = o * lax.rsqrt(jnp.mean(o * o, axis=-1, keepdims=True) + EPS) * og_ref[...]
    o_ref[...] = (o * (1.0 - lam_init)).astype(BF16)


def _dattn_prompt(qt, k, vt, lams, og, lam_init):
    nb, nk, _, tk = qt.shape
    t = nk * tk
    tq = 2 * tk
    nq = t // tq
    lsp = pl.BlockSpec((1, DQK), lambda b, h, i: (0, 0))
    return pl.pallas_call(
        functools.partial(_dattn_kernel, tq=tq, lam_init=lam_init),
        grid=(nb, N_HEADS, nq),
        in_specs=[pl.BlockSpec((None, 2, HEAD_W, tk), lambda b, h, i: (b, i, h, 0)),
                  pl.BlockSpec((t, HEAD_W), lambda b, h, i: (b, h)),
                  pl.BlockSpec((None, nk, HEAD_W, tk), lambda b, h, i: (b, 0, h, 0)),
                  lsp, lsp, lsp, lsp,
                  pl.BlockSpec((1, HEAD_W), lambda b, h, i: (0, 0))],
        out_specs=pl.BlockSpec((tq, HEAD_W), lambda b, h, i: (b * nq + i, h)),
        out_shape=jax.ShapeDtypeStruct(k.shape, BF16),
        scratch_shapes=[pltpu.VMEM((tk, 2 * tq), F32), pltpu.VMEM((tk, 2 * tq), F32),
                        pltpu.VMEM((tk, 2 * tq), BF16), pltpu.VMEM((tk, 2 * tq), BF16),
                        pltpu.VMEM((SUBLANES, 2 * tq), F32), pltpu.VMEM((SUBLANES, 2 * tq), F32),
                        pltpu.VMEM((HEAD_W, 2 * tq), F32)],
        compiler_params=_cparams(("arbitrary",) * 3, 32 * 2**20),
        name="diff_attn_prompt",
    )(qt, k, vt, *[x.reshape(1, DQK) for x in lams], og.reshape(1, HEAD_W))


def _dattn_paged_kernel(pt_ref, q_ref, *refs, pps, n_new, n_pages, lam_init):
    k_refs = refs[:pps]
    v_refs = refs[pps:2 * pps]
    kn_ref, vn_ref, l1_ref, l2_ref, l3_ref, l4_ref, og_ref, o_ref, s_scr, m_sc, l_sc, acc_sc = refs[2 * pps:]
    phase = pl.program_id(1)
    step_id = pl.program_id(2)
    nsteps = pl.num_programs(2)
    prow = s_scr.shape[2]
    nchunk = prow // LANES
    lane = lax.broadcasted_iota(I32, (SUBLANES, LANES), 1)
    lane_head = lane % N_HEADS
    page_head = lax.broadcasted_iota(I32, (SUBLANES, prow), 1) % N_HEADS
    row = lax.broadcasted_iota(I32, (SUBLANES, prow), 0)
    tile = lambda x: jnp.concatenate([x] * nchunk, axis=1)

    def lane_class_reduce(x, op):
        for sh in (8, 16, 32, 64):
            x = op(x, pltpu.roll(x, sh, 1))
        return x

    def score_pages(ks, first, valid):
        q = q_ref[...]
        parts = []
        for pi, kp in enumerate(ks):
            sf = _dot_nt(q, kp.astype(BF16))
            page = []
            for c in range(nchunk):
                blk = sf[:, c * LANES:(c + 1) * LANES]
                s = jnp.zeros((SUBLANES, LANES), F32)
                for h in range(N_HEADS):
                    s = jnp.where(lane_head == h, blk[h * SUBLANES:(h + 1) * SUBLANES, :], s)
                page.append(s)
            s_page = jnp.concatenate(page, axis=1)
            if valid is not None:
                s_page = jnp.where(valid, s_page, NEG)
                page = [s_page[:, c * LANES:(c + 1) * LANES] for c in range(nchunk)]
            s_scr[first + pi] = s_page
            parts += page
        m_old = m_sc[...]
        m_new = jnp.maximum(m_old, functools.reduce(jnp.maximum, parts))
        l_sc[...] = jnp.exp(m_old - m_new) * l_sc[...] + functools.reduce(jnp.add, [jnp.exp(s - m_new) for s in parts])
        m_sc[...] = m_new

    def value_pages(vs, first):
        lam = _lam_value(l1_ref, l2_ref, l3_ref, l4_ref, lam_init)
        m_full = tile(m_sc[...])
        l_full = tile(l_sc[...])
        pv = jnp.zeros(acc_sc.shape, F32)
        for pi, vp in enumerate(vs):
            e = jnp.exp(s_scr[first + pi] - m_full) / l_full
            a = jnp.where(row < n_new, e - lam * pltpu.roll(e, SUBLANES - n_new, 0), 0.0)
            pexp = jnp.concatenate([jnp.where(page_head == h, a, 0.0) for h in range(N_HEADS)], axis=0)
            pv = pv + _dot(pexp.astype(BF16), vp.astype(BF16))
        acc_sc[...] = acc_sc[...] + pv

    @pl.when(phase == 0)
    def _():
        @pl.when(step_id == 0)
        def _():
            m_sc[...] = jnp.full(m_sc.shape, NEG, F32)
            l_sc[...] = jnp.zeros(l_sc.shape, F32)

        score_pages([r[...] for r in k_refs], step_id * pps, None)

        @pl.when(step_id == nsteps - 1)
        def _():
            tok = lax.broadcasted_iota(I32, (SUBLANES, prow), 1) // N_HEADS
            score_pages([kn_ref[...]], n_pages, tok <= row % n_new)
            m_lane = m_sc[...]
            m_head = lane_class_reduce(m_lane, jnp.maximum)
            l_sc[...] = lane_class_reduce(l_sc[...] * jnp.exp(m_lane - m_head), jnp.add)
            m_sc[...] = m_head

    @pl.when(phase == 1)
    def _():
        @pl.when(step_id == 0)
        def _():
            acc_sc[...] = jnp.zeros(acc_sc.shape, F32)

        value_pages([r[...] for r in v_refs], step_id * pps)

        @pl.when(step_id == nsteps - 1)
        def _():
            value_pages([vn_ref[...]], n_pages)
            acc = acc_sc[...]
            outs = []
            for h in range(N_HEADS):
                o = acc[h * SUBLANES:(h + 1) * SUBLANES]
                o = o * lax.rsqrt(jnp.mean(o * o, axis=-1, keepdims=True) + EPS) * og_ref[...]
                outs.append(o * (1.0 - lam_init))
            o_ref[...] = jnp.concatenate(outs, axis=1).astype(BF16)


def _dattn_paged(qall, cache_k, cache_v, page_table, k_new, v_new, lams, og, n_new, lam_init):
    nb, n_pages = page_table.shape
    pps = PAGES_PER_STEP
    prow = cache_k.shape[1]
    nsteps = n_pages // pps
    kspec = lambda j: pl.BlockSpec(
        (None, prow, HEAD_W),
        lambda b, ph, s, pt, j=j: (pt[b * n_pages + (s * (1 - ph) + (nsteps - 1) * ph) * pps + j], 0, 0))
    vspec = lambda j: pl.BlockSpec(
        (None, prow, HEAD_W), lambda b, ph, s, pt, j=j: (pt[b * n_pages + s * ph * pps + j], 0, 0))
    lsp = pl.BlockSpec((1, DQK), lambda b, ph, s, pt: (0, 0))
    nspec = pl.BlockSpec((None, k_new.shape[1], HEAD_W), lambda b, ph, s, pt: (b, 0, 0))
    grid_spec = pltpu.PrefetchScalarGridSpec(
        num_scalar_prefetch=1,
        grid=(nb, 2, nsteps),
        in_specs=[pl.BlockSpec((None, N_HEADS * SUBLANES, HEAD_W), lambda b, ph, s, pt: (b, 0, 0))]
                 + [kspec(j) for j in range(pps)] + [vspec(j) for j in range(pps)]
                 + [nspec, nspec, lsp, lsp, lsp, lsp, pl.BlockSpec((1, HEAD_W), lambda b, ph, s, pt: (0, 0))],
        out_specs=pl.BlockSpec((SUBLANES, N_HEADS * HEAD_W), lambda b, ph, s, pt: (b, 0)),
        scratch_shapes=[pltpu.VMEM((n_pages + 1, SUBLANES, prow), F32),
                        pltpu.VMEM((SUBLANES, LANES), F32), pltpu.VMEM((SUBLANES, LANES), F32),
                        pltpu.VMEM((N_HEADS * SUBLANES, HEAD_W), F32)])
    return pl.pallas_call(
        functools.partial(_dattn_paged_kernel, pps=pps, n_new=n_new, n_pages=n_pages, lam_init=lam_init),
        grid_spec=grid_spec,
        out_shape=jax.ShapeDtypeStruct((nb * SUBLANES, N_HEADS * HEAD_W), BF16),
        compiler_params=_cparams(("arbitrary",) * 3,
                                 4 * pps * prow * HEAD_W * 4 + (n_pages + 1) * SUBLANES * prow * 4 + 24 * 2**20),
        name="diff_attn_paged",
    )(page_table.reshape(-1), qall, *([cache_k] * pps), *([cache_v] * pps), k_new, v_new,
      *[x.reshape(1, DQK) for x in lams], og.reshape(1, HEAD_W))


def _split3(a):
    hi = a.astype(BF16)
    r1 = a - hi.astype(F32)
    mid = r1.astype(BF16)
    lo = (r1 - mid.astype(F32)).astype(BF16)
    return hi, mid, lo


def _mlstm_kernel(qk_ref, v_ref, o_ref, gt_ref, cw_ref, cb_ref, gb_ref, og_ref,
                  C0_ref, n0_ref, m0_ref, cv0_ref,
                  h_ref, C_ref, n_ref, m_ref, cv_ref, xbuf, *, L, rows_in, n_valid):
    c = pl.program_id(1)
    width = N_HEADS * HEAD_W

    @pl.when(c == 0)
    def _():
        C_ref[...] = C0_ref[...]
        n_ref[...] = n0_ref[...]
        m_ref[...] = m0_ref[...]
        xbuf[0:SUBLANES, :] = cv0_ref[...]

    def padded(ref):
        x = ref[...]
        if rows_in < L:
            x = jnp.concatenate([x, jnp.zeros((L - rows_in, x.shape[1]), x.dtype)], axis=0)
        return x

    xbuf[SUBLANES:SUBLANES + L, :] = padded(qk_ref)
    taps = [xbuf[SUBLANES - (CONV_W - 1) + j:SUBLANES - (CONV_W - 1) + j + L, :] * cw_ref[j:j + 1, :]
            for j in range(CONV_W)]
    conv = cb_ref[...] + functools.reduce(jnp.add, taps)
    cv_ref[...] = xbuf[n_valid:n_valid + SUBLANES, :]
    xbuf[0:SUBLANES, :] = xbuf[L:L + SUBLANES, :]
    conv = _silu(conv)
    q_all = conv[:, :width].astype(BF16)
    k_all = conv[:, width:] * (HEAD_W ** -0.5)
    v_all = padded(v_ref).astype(BF16)
    o_all = padded(o_ref)

    g = padded(gt_ref) + gb_ref[...]
    lane = lax.broadcasted_iota(I32, (L, LANES), 1)
    rowi = lax.broadcasted_iota(I32, (L, LANES), 0)
    is_f = (lane >= N_HEADS) & (lane < 2 * N_HEADS)
    logsig = jnp.minimum(g, 0.0) - jnp.log1p(jnp.exp(-jnp.abs(g)))
    a = jnp.where(is_f, logsig, g)
    if n_valid < L:
        a = jnp.where(rowi < n_valid, a, jnp.where(is_f, 0.0, NEG))
    tri = (lax.broadcasted_iota(I32, (L, L), 1) <= lax.broadcasted_iota(I32, (L, L), 0))
    tri_b = tri.astype(BF16)
    cum = functools.reduce(jnp.add, [_dot(tri_b, part) for part in _split3(jnp.where(is_f, a, 0.0))])
    a_t = a.T
    cum_t = cum.T

    for h in range(N_HEADS):
        sl = slice(h * HEAD_W, (h + 1) * HEAD_W)
        b_col = cum[:, N_HEADS + h:N_HEADS + h + 1]
        b_row = cum_t[N_HEADS + h:N_HEADS + h + 1, :]
        i_col = a[:, h:h + 1]
        i_row = a_t[h:h + 1, :]
        m_prev = m_ref[h:h + 1, 0:1]
        d = jnp.where(tri, (b_col - b_row) + i_row, NEG)
        inter = b_col + m_prev
        mt = jnp.maximum(inter, jnp.max(d, axis=-1, keepdims=True))
        w_intra = jnp.exp(d - mt)
        w_inter = jnp.exp(inter - mt)
        qh, kh, vh = q_all[:, sl], k_all[:, sl], v_all[:, sl]
        C_h = C_ref[h]
        n_h = n_ref[h:h + 1, :]
        s = _dot_nt(qh, kh.astype(BF16)) * w_intra
        num = _dot(s.astype(BF16), vh) + w_inter * _dot(qh, C_h.astype(BF16))
        den = (jnp.sum(s, axis=-1, keepdims=True)
               + w_inter * jnp.sum(qh.astype(F32) * n_h.astype(BF16).astype(F32), axis=-1, keepdims=True))
        hh = num / jnp.maximum(jnp.abs(den), jnp.exp(-mt))
        m_last = mt[L - 1:L, :]
        b_last = b_col[L - 1:L, :]
        gk = jnp.exp((b_last - b_col) + i_col - m_last) * kh
        decay = jnp.exp(b_last + m_prev - m_last)
        C_ref[h] = decay * C_h + _dot_tn(gk.astype(BF16), vh)
        n_ref[h:h + 1, :] = decay * n_h + jnp.sum(gk, axis=0, keepdims=True)
        m_ref[h:h + 1, :] = jnp.broadcast_to(m_last, (1, LANES))
        hn = hh * lax.rsqrt(jnp.mean(hh * hh, axis=-1, keepdims=True) + EPS) * og_ref[...]
        out = hn * jax.nn.sigmoid(o_all[:, sl])
        h_ref[:, sl] = out[:rows_in].astype(BF16)


def _mlstm(proj, gates, conv_w, conv_b, gate_b, og, C0, n0, m0, cv0, nb, rows_in, nchunks, n_valid):
    L = MLSTM_L
    w = N_HEADS * HEAD_W
    row = lambda cb: (lambda b, c: (b * nchunks + c, cb))
    st3 = lambda b, c: (b, 0, 0)
    kern = functools.partial(_mlstm_kernel, L=L, rows_in=rows_in, n_valid=n_valid)
    return pl.pallas_call(
        kern,
        grid=(nb, nchunks),
        in_specs=[pl.BlockSpec((rows_in, 2 * w), lambda b, c: (b * nchunks + c, 0)),
                  pl.BlockSpec((rows_in, w), row(2)), pl.BlockSpec((rows_in, w), row(3)),
                  pl.BlockSpec((rows_in, LANES), lambda b, c: (b * nchunks + c, 0)),
                  pl.BlockSpec((SUBLANES, 2 * w), lambda b, c: (0, 0)),
                  pl.BlockSpec((1, 2 * w), lambda b, c: (0, 0)),
                  pl.BlockSpec((1, LANES), lambda b, c: (0, 0)),
                  pl.BlockSpec((1, HEAD_W), lambda b, c: (0, 0)),
                  pl.BlockSpec((None, N_HEADS, HEAD_W, HEAD_W), lambda b, c: (b, 0, 0, 0)),
                  pl.BlockSpec((None, N_HEADS, HEAD_W), st3),
                  pl.BlockSpec((None, N_HEADS, LANES), st3),
                  pl.BlockSpec((None, SUBLANES, 2 * w), st3)],
        out_specs=[pl.BlockSpec((rows_in, w), lambda b, c: (b * nchunks + c, 0)),
                   pl.BlockSpec((None, N_HEADS, HEAD_W, HEAD_W), lambda b, c: (b, 0, 0, 0)),
                   pl.BlockSpec((None, N_HEADS, HEAD_W), st3),
                   pl.BlockSpec((None, N_HEADS, LANES), st3),
                   pl.BlockSpec((None, SUBLANES, 2 * w), st3)],
        out_shape=[jax.ShapeDtypeStruct((nb * nchunks * rows_in, w), BF16),
                   jax.ShapeDtypeStruct((nb, N_HEADS, HEAD_W, HEAD_W), F32),
                   jax.ShapeDtypeStruct((nb, N_HEADS, HEAD_W), F32),
                   jax.ShapeDtypeStruct((nb, N_HEADS, LANES), F32),
                   jax.ShapeDtypeStruct((nb, SUBLANES, 2 * w), F32)],
        scratch_shapes=[pltpu.VMEM((SUBLANES + L, 2 * w), F32)],
        compiler_params=_cparams(("arbitrary", "arbitrary"), 40 * 2**20),
        name="mlstm_chunkwise",
    )(proj, proj, proj, gates, conv_w, conv_b, gate_b, og, C0, n0, m0, cv0)


def _outproj_kernel(hm_ref, ha_ref, x_ref, g1_ref, sc_ref, sh_ref, ng_ref, wo_ref, wr_ref, br_ref,
                    x1_ref, hn_ref, ids_ref, gates_ref):
    half = hm_ref.shape[1]
    mix = _dot(hm_ref[...], wo_ref[0:half, :]) + _dot(ha_ref[...], wo_ref[half:2 * half, :])
    x1 = x_ref[...] + g1_ref[...] * mix
    x1_ref[...] = x1
    y = x1 * lax.rsqrt(jnp.mean(x1 * x1, axis=-1, keepdims=True) + EPS) * ng_ref[...]
    hn = y * (1.0 + sc_ref[...]) + sh_ref[...]
    hn_ref[...] = hn
    logits = _dot(hn.astype(BF16), wr_ref[...]) + br_ref[...]

    lane = lax.broadcasted_iota(I32, logits.shape, 1).astype(F32)
    big = 1000.0
    is_g = lane < N_GROUPS
    gl = jnp.where(is_g, logits, NEG)
    gmax = jnp.max(gl, axis=-1, keepdims=True)
    g_idx = jnp.min(jnp.where(gl == gmax, lane, big), axis=-1, keepdims=True)
    g_p = 1.0 / jnp.sum(jnp.where(is_g, jnp.exp(gl - gmax), 0.0), axis=-1, keepdims=True)
    e_lo = N_GROUPS + EXPERTS_PER_GROUP * g_idx
    in_grp = (lane >= e_lo) & (lane < e_lo + EXPERTS_PER_GROUP)
    el = jnp.where(in_grp, logits, NEG)
    ex = jnp.where(in_grp, jnp.exp(el - jnp.max(el, axis=-1, keepdims=True)), 0.0)
    p = jnp.where(in_grp, ex / jnp.sum(ex, axis=-1, keepdims=True), -1.0)
    top1 = jnp.max(p, axis=-1, keepdims=True)
    idx1 = jnp.min(jnp.where(p == top1, lane, big), axis=-1, keepdims=True)
    p2 = jnp.where(lane == idx1, -1.0, p)
    top2 = jnp.max(p2, axis=-1, keepdims=True)
    idx2 = jnp.min(jnp.where(p2 == top2, lane, big), axis=-1, keepdims=True)
    tsum = top1 + top2
    ids_ref[...] = jnp.where(lane == 0.0, idx1 - N_GROUPS,
                             jnp.where(lane == 1.0, idx2 - N_GROUPS, -1.0)).astype(I32)
    gates_ref[...] = jnp.where(lane == 0.0, g_p * top1 / tsum, jnp.where(lane == 1.0, g_p * top2 / tsum, 0.0))


def _outproj(hm, ha, x, mod_specs, mods, ng, wo, wr, br, tm):
    m, d = x.shape
    half = hm.shape[1]
    row = lambda wdt: pl.BlockSpec((tm, wdt), lambda i: (i, 0))
    const = lambda shp: pl.BlockSpec(shp, lambda i: (0, 0))
    return pl.pallas_call(
        _outproj_kernel,
        grid=(m // tm,),
        in_specs=[row(half), row(half), row(d), *mod_specs, const((1, d)), const((d, d)),
                  const((d, LANES)), const((1, LANES))],
        out_specs=[row(d), row(d), row(LANES), row(LANES)],
        out_shape=[jax.ShapeDtypeStruct((m, d), F32), jax.ShapeDtypeStruct((m, d), F32),
                   jax.ShapeDtypeStruct((m, LANES), I32), jax.ShapeDtypeStruct((m, LANES), F32)],
        compiler_params=_cparams(("arbitrary",), 2 * d * d * 2 + 8 * tm * d * 4 + 8 * 2**20),
        name="outproj_norm2_route",
    )(hm, ha, x, *mods, ng.reshape(1, d), wo, wr, br)


def _rank_kernel(ids_ref, dest_ref, cnt_ref, carry, pstart):
    pss = pl.program_id(0)
    i = pl.program_id(1)
    ids = ids_ref[...]
    tm = ids.shape[0]
    lane = lax.broadcasted_iota(I32, ids.shape, 1)
    o0 = lane == ids[:, 0:1]
    o1 = lane == ids[:, 1:2]
    onehot = jnp.where(o0 | o1, 1.0, 0.0)
    col_counts = jnp.sum(onehot, axis=0, keepdims=True)

    @pl.when((pss == 0) & (i == 0))
    def _():
        carry[...] = jnp.zeros(carry.shape, F32)

    @pl.when(pss == 0)
    def _():
        carry[...] = carry[...] + col_counts

    @pl.when((pss == 1) & (i == 0))
    def _():
        cnt = carry[...]
        cnt_ref[...] = cnt
        blocks = jnp.floor((cnt + (MOE_ROWS - 1)) * (1.0 / MOE_ROWS))
        earlier = lax.broadcasted_iota(I32, (LANES, LANES), 0) < lax.broadcasted_iota(I32, (LANES, LANES), 1)
        pstart[...] = _dot(blocks.astype(BF16), earlier.astype(BF16)) * MOE_ROWS
        carry[...] = jnp.zeros(carry.shape, F32)

    @pl.when(pss == 1)
    def _():
        strict = lax.broadcasted_iota(I32, (tm, tm), 1) < lax.broadcasted_iota(I32, (tm, tm), 0)
        before = _dot(strict.astype(BF16), onehot.astype(BF16)) + carry[0:1, :] + pstart[0:1, :]
        r0 = jnp.sum(jnp.where(o0, before, 0.0), axis=-1, keepdims=True)
        r1 = jnp.sum(jnp.where(o1, before, 0.0), axis=-1, keepdims=True)
        dest_ref[...] = jnp.where(lane == 0, r0, jnp.where(lane == 1, r1, 0.0)).astype(I32)
        carry[...] = carry[...] + col_counts


def _ranks(ids, tm):
    m = ids.shape[0]
    return pl.pallas_call(
        _rank_kernel,
        grid=(2, m // tm),
        in_specs=[pl.BlockSpec((tm, LANES), lambda p, i: (i, 0))],
        out_specs=[pl.BlockSpec((tm, LANES), lambda p, i: (i * p, 0)),
                   pl.BlockSpec((SUBLANES, LANES), lambda p, i: (0, 0))],
        out_shape=[jax.ShapeDtypeStruct((m, LANES), I32), jax.ShapeDtypeStruct((SUBLANES, LANES), F32)],
        scratch_shapes=[pltpu.VMEM((SUBLANES, LANES), F32), pltpu.VMEM((SUBLANES, LANES), F32)],
        compiler_params=_cparams(("arbitrary", "arbitrary"), 16 * 2**20),
        name="moe_ranks",
    )(ids)


def _row_copies(idx_ref, t0, tm, copy):
    def body(t, _):
        for k in range(2):
            copy(t, k, idx_ref[2 * (t0 + t) + k]).start()
        return 0
    lax.fori_loop(0, tm, body, 0, unroll=8)


def _dispatch_kernel(dest_ref, h_ref, xin_ref, xs_ref, sem, *, tm, base):
    del xin_ref
    t0 = base + pl.program_id(0) * tm
    _row_copies(dest_ref, t0, tm,
                lambda t, k, d: pltpu.make_async_copy(h_ref.at[pl.ds(t, 1)], xs_ref.at[pl.ds(d, 1)], sem))
    for _ in range(2):
        pltpu.make_async_copy(h_ref, xs_ref.at[pl.ds(0, tm)], sem).wait()


def _dispatch(dest_flat, hn, xs, tm, base):
    m, d = hn.shape
    any_spec = pl.BlockSpec(memory_space=pl.ANY)
    grid_spec = pltpu.PrefetchScalarGridSpec(
        num_scalar_prefetch=1, grid=(m // tm,),
        in_specs=[pl.BlockSpec((tm, d), lambda i, dr: (i, 0)), any_spec], out_specs=any_spec,
        scratch_shapes=[pltpu.SemaphoreType.DMA])
    return pl.pallas_call(
        functools.partial(_dispatch_kernel, tm=tm, base=base),
        grid_spec=grid_spec,
        out_shape=jax.ShapeDtypeStruct(xs.shape, xs.dtype),
        input_output_aliases={2: 0},
        compiler_params=_cparams(("arbitrary",), 4 * tm * d * 4 + 8 * 2**20),
        name="moe_dispatch",
    )(dest_flat, hn, xs)


def _expert_weights(b, be_ref, bf_ref, nx_ref, w_hbm, w_stage, w_bf16, sem):
    def copies(e):
        return [pltpu.make_async_copy(src.at[0, e], dst, sem.at[i])
                for i, (src, dst) in enumerate(zip(w_hbm, w_stage))]

    @pl.when(b == 0)
    def _():
        for c in copies(be_ref[0]):
            c.start()

    @pl.when(bf_ref[b] == 1)
    def _():
        for c in copies(be_ref[b]):
            c.wait()
        for stage, wb in zip(w_stage, w_bf16):
            wb[...] = stage[...].astype(BF16)

        @pl.when(nx_ref[b] >= 0)
        def _():
            for c in copies(nx_ref[b]):
                c.start()


def _expert_up_kernel(be_ref, bf_ref, nx_ref, nv_ref, x_ref, w1_hbm, w3_hbm, h_ref, w1f, w3f, w1b, w3b, sem):
    b = pl.program_id(0)
    _expert_weights(b, be_ref, bf_ref, nx_ref, (w1_hbm, w3_hbm), (w1f, w3f), (w1b, w3b), sem)

    @pl.when(b < nv_ref[0])
    def _():
        x = x_ref[...].astype(BF16)
        h_ref[...] = (_silu(_dot(x, w1b[...])) * _dot(x, w3b[...])).astype(BF16)

    @pl.when(b >= nv_ref[0])
    def _():
        h_ref[...] = jnp.zeros(h_ref.shape, BF16)


def _expert_up(tabs, xs, w1, w3):
    rows, d = xs.shape
    f = w1.shape[-1]
    nb = rows // MOE_ROWS
    any_spec = pl.BlockSpec(memory_space=pl.ANY)
    grid_spec = pltpu.PrefetchScalarGridSpec(
        num_scalar_prefetch=4, grid=(nb,),
        in_specs=[pl.BlockSpec((MOE_ROWS, d), lambda b, *_: (b, 0)), any_spec, any_spec],
        out_specs=pl.BlockSpec((MOE_ROWS, f), lambda b, *_: (b, 0)),
        scratch_shapes=[pltpu.VMEM((d, f), F32), pltpu.VMEM((d, f), F32),
                        pltpu.VMEM((d, f), BF16), pltpu.VMEM((d, f), BF16), pltpu.SemaphoreType.DMA((2,))])
    return pl.pallas_call(
        _expert_up_kernel,
        grid_spec=grid_spec,
        out_shape=jax.ShapeDtypeStruct((rows, f), BF16),
        compiler_params=_cparams(("arbitrary",), 2 * d * f * 6 + 4 * MOE_ROWS * d * 4 + 8 * 2**20),
        name="moe_expert_up",
    )(*tabs, xs, w1, w3)


def _expert_down_kernel(be_ref, bf_ref, nx_ref, nv_ref, h_ref, w2_hbm, y_ref, w2f, w2b, sem):
    b = pl.program_id(0)
    _expert_weights(b, be_ref, bf_ref, nx_ref, (w2_hbm,), (w2f,), (w2b,), sem)

    @pl.when(b < nv_ref[0])
    def _():
        y_ref[...] = _dot(h_ref[...], w2b[...])

    @pl.when(b >= nv_ref[0])
    def _():
        y_ref[...] = jnp.zeros(y_ref.shape, F32)


def _expert_down(tabs, hs, w2):
    rows, f = hs.shape
    d = w2.shape[-1]
    nb = rows // MOE_ROWS
    grid_spec = pltpu.PrefetchScalarGridSpec(
        num_scalar_prefetch=4, grid=(nb,),
        in_specs=[pl.BlockSpec((MOE_ROWS, f), lambda b, *_: (b, 0)), pl.BlockSpec(memory_space=pl.ANY)],
        out_specs=pl.BlockSpec((MOE_ROWS, d), lambda b, *_: (b, 0)),
        scratch_shapes=[pltpu.VMEM((f, d), F32), pltpu.VMEM((f, d), BF16), pltpu.SemaphoreType.DMA((1,))])
    return pl.pallas_call(
        _expert_down_kernel,
        grid_spec=grid_spec,
        out_shape=jax.ShapeDtypeStruct((rows, d), F32),
        compiler_params=_cparams(("arbitrary",), f * d * 6 + 6 * MOE_ROWS * d * 4 + 8 * 2**20),
        name="moe_expert_down",
    )(*tabs, hs, w2)


def _combine_kernel(dest_ref, x1_ref, g2_ref, gates_ref, y_ref, o_ref, ybuf, sem, *, tm, base):
    i = pl.program_id(0)
    slot = i % 2

    def gather(step, s):
        _row_copies(dest_ref, base + step * tm, tm,
                    lambda t, k, d: pltpu.make_async_copy(y_ref.at[pl.ds(d, 1)], ybuf.at[s, k, pl.ds(t, 1)],
                                                          sem.at[s]))

    @pl.when(i == 0)
    def _():
        gather(0, 0)

    @pl.when(i + 1 < pl.num_programs(0))
    def _():
        gather(i + 1, 1 - slot)

    for k in range(2):
        pltpu.make_async_copy(y_ref.at[pl.ds(0, tm)], ybuf.at[slot, k], sem.at[slot]).wait()
    gates = gates_ref[...]
    ff = gates[:, 0:1] * ybuf[slot, 0] + gates[:, 1:2] * ybuf[slot, 1]
    o_ref[...] = x1_ref[...] + g2_ref[...] * ff


def _combine(dest_flat, x1, g2_spec, g2, gates, y, tm, base):
    m, d = x1.shape
    grid_spec = pltpu.PrefetchScalarGridSpec(
        num_scalar_prefetch=1, grid=(m // tm,),
        in_specs=[pl.BlockSpec((tm, d), lambda i, dr: (i, 0)), g2_spec,
                  pl.BlockSpec((tm, LANES), lambda i, dr: (i, 0)),
                  pl.BlockSpec(memory_space=pl.ANY)],
        out_specs=pl.BlockSpec((tm, d), lambda i, dr: (i, 0)),
        scratch_shapes=[pltpu.VMEM((2, 2, tm, d), F32), pltpu.SemaphoreType.DMA((2,))])
    return pl.pallas_call(
        functools.partial(_combine_kernel, tm=tm, base=base),
        grid_spec=grid_spec,
        out_shape=jax.ShapeDtypeStruct((m, d), F32),
        compiler_params=_cparams(("arbitrary",), 8 * tm * d * 4 + 8 * 2**20),
        name="moe_combine",
    )(dest_flat, x1, g2, gates, y)


def kernel(x_prompt, x_sample, cache_k, cache_v, page_table, state_mlstm_C, state_mlstm_n, state_mlstm_m, state_conv,
           c_prompt, c_sample, w_ada, b_ada, norm1_g, norm2_g, w_in, conv_w, conv_b, b_igate, b_fgate, mlstm_out_g,
           q_norm_g, k_norm_g, lam_q1, lam_k1, lam_q2, lam_k2, diff_out_g, w_out, w_grp, b_grp, w_rt, b_rt, w1, w3, w2):
    B, T, D = x_prompt.shape
    DB, TS, _ = x_sample.shape
    depth = w_ada.shape[0]
    assert depth == 1, "single-layer trunk"
    n_pages, page = page_table.shape[1], cache_k.shape[2]
    past = n_pages * page
    W = N_HEADS * HEAD_W
    lam_init = 0.8 - 0.6 * math.exp(-0.3 * 0)
    NP, NS = B * T, DB * TS
    SP = SAMPLE_PAD

    c_all = jnp.concatenate([c_prompt, c_sample, jnp.zeros((16 - B - DB, D), F32)], axis=0)
    w_main, w_gate = _regroup_w_in(w_in[0], 4 * W, 2 * N_HEADS)
    gate_b = jnp.pad(jnp.concatenate([b_igate[0], b_fgate[0]]), (0, LANES - 2 * N_HEADS)).reshape(1, LANES)
    cw = jnp.pad(conv_w[0], ((0, SUBLANES - CONV_W), (0, 0)))
    cb = conv_b[0].reshape(1, 2 * W)
    og_m = mlstm_out_g[0].reshape(1, HEAD_W)
    wo = w_out[0].astype(BF16)
    w_route = jnp.concatenate([w_grp[0], w_rt[0].transpose(1, 0, 2).reshape(D, N_EXPERTS)], axis=1)
    w_route = jnp.pad(w_route, ((0, 0), (0, LANES - N_GROUPS - N_EXPERTS))).astype(BF16)
    b_route = jnp.pad(jnp.concatenate([b_grp[0], b_rt[0].reshape(-1)]), (0, LANES - N_GROUPS - N_EXPERTS)).reshape(1, LANES)
    lams = (lam_q1[0], lam_k1[0], lam_q2[0], lam_k2[0])

    mod = _ada(c_all, w_ada[0], b_ada[0])
    mod3 = mod.reshape(16, 1, 6 * D)
    mod_s = jnp.repeat(mod[B:B + DB], SP, axis=0)

    def p_mod(chunk, tm):
        return pl.BlockSpec((None, 1, D), lambda i, *_: ((i * tm) // T, 0, chunk))

    def s_mod(chunk, tm):
        return pl.BlockSpec((tm, D), lambda i, *_: (i, chunk))

    xp = x_prompt.reshape(NP, D)
    tm_in = 512
    proj_p, gates_p = _inproj(xp, (p_mod(1, tm_in), p_mod(0, tm_in)), (mod3, mod3), norm1_g[0], w_main, w_gate, tm_in)
    cos_p, sin_p = _rope_tables(jnp.arange(T, dtype=I32))
    knew_p, kb_p, vnew_p, qt_p, vt_p = _qkrope(proj_p, 4, cos_p, sin_p, q_norm_g[0], k_norm_g[0], 256, B)
    ha_p = _dattn_prompt(qt_p, kb_p, vt_p, lams, diff_out_g[0], lam_init)
    zeros = lambda *s: jnp.zeros(s, F32)
    hm_p, C_p, n_p, m_p, cv_p = _mlstm(
        proj_p, gates_p, cw, cb, gate_b, og_m,
        zeros(B, N_HEADS, HEAD_W, HEAD_W), zeros(B, N_HEADS, HEAD_W), zeros(B, N_HEADS, LANES),
        zeros(B, SUBLANES, 2 * W), B, MLSTM_L, T // MLSTM_L, MLSTM_L)
    tm_o = 256
    x1_p, hn2_p, ids_p, gts_p = _outproj(
        hm_p, ha_p, xp, (p_mod(2, tm_o), p_mod(4, tm_o), p_mod(3, tm_o)), (mod3, mod3, mod3),
        norm2_g[0], wo, w_route, b_route, tm_o)

    MS = DB * SP
    xs_pad = jnp.pad(x_sample, ((0, 0), (0, SP - TS), (0, 0))).reshape(MS, D)
    proj_s, gates_s = _inproj(xs_pad, (s_mod(1, MS), s_mod(0, MS)), (mod_s, mod_s), norm1_g[0], w_main, w_gate, MS)
    pos_s = past + (jnp.arange(MS, dtype=I32) % SP)
    cos_s, sin_s = _rope_tables(pos_s)
    knew_s, kb_s, vnew_s, q_s, vb_s = _qkrope(proj_s, 4, cos_s, sin_s, q_norm_g[0], k_norm_g[0], MS)
    qr = q_s.reshape(DB, SP, N_HEADS, 2, DQK)[:, :TS].transpose(0, 2, 3, 1, 4)
    zq = jnp.zeros_like(qr[:, :, 0])
    qall = jnp.stack([jnp.concatenate([qr[:, :, 0], zq], axis=-1),
                      jnp.concatenate([zq, qr[:, :, 1]], axis=-1)], axis=2).reshape(DB, N_HEADS * 2 * TS, HEAD_W)
    new_rows = page * N_HEADS
    pad_new = lambda a: jnp.pad(a.reshape(DB, SP * N_HEADS, HEAD_W), ((0, 0), (0, new_rows - SP * N_HEADS), (0, 0)))
    ha_s = _dattn_paged(qall, cache_k[0].reshape(-1, new_rows, HEAD_W), cache_v[0].reshape(-1, new_rows, HEAD_W),
                        page_table, pad_new(kb_s), pad_new(vb_s), lams, diff_out_g[0], TS, lam_init)
    m0_s = jnp.broadcast_to(state_mlstm_m[0][:, :, None], (DB, N_HEADS, LANES))
    cv0_s = jnp.pad(state_conv[0], ((0, 0), (SUBLANES - (CONV_W - 1), 0), (0, 0)))
    hm_s, C_s, n_s, m_s, cv_s = _mlstm(proj_s, gates_s, cw, cb, gate_b, og_m, state_mlstm_C[0], state_mlstm_n[0],
                                       m0_s, cv0_s, DB, SP, 1, TS)
    x1_s, hn2_s, ids_s, gts_s = _outproj(
        hm_s, ha_s, xs_pad, (s_mod(2, MS), s_mod(4, MS), s_mod(3, MS)), (mod_s, mod_s, mod_s),
        norm2_g[0], wo, w_route, b_route, MS)
    real = lambda a: a.reshape(DB, SP, -1)[:, :TS].reshape(NS, -1)
    x1_s, hn2_s, ids_s, gts_s = real(x1_s), real(hn2_s), real(ids_s), real(gts_s)

    NT = NP + NS
    tm_r = 256
    n_rank = -(-NT // tm_r) * tm_r
    ids_all = jnp.concatenate([ids_p, ids_s, jnp.full((n_rank - NT, LANES), -1, I32)], axis=0)
    dest2, cnt = _ranks(ids_all, tm_r)
    dest = dest2[:NT, :2].reshape(-1)
    counts = cnt[0, :N_EXPERTS].astype(I32)
    pend = jnp.cumsum((counts + MOE_ROWS - 1) // MOE_ROWS)
    n_blocks = (2 * NT + N_EXPERTS * (MOE_ROWS - 1)) // MOE_ROWS
    n_valid = pend[-1:]
    blk = jnp.minimum(jnp.arange(n_blocks, dtype=I32), n_valid[0] - 1)
    blk_e = jnp.sum((blk[:, None] >= pend[None, :]).astype(I32), axis=1)
    blk_first = jnp.concatenate([jnp.ones((1,), I32), (blk_e[1:] != blk_e[:-1]).astype(I32)])
    first_pos = jnp.where(blk_first == 1, jnp.arange(n_blocks, dtype=I32), n_blocks)
    next_first = jnp.concatenate([lax.cummin(first_pos, reverse=True)[1:], jnp.full((1,), n_blocks, I32)])
    blk_next = jnp.where(next_first < n_blocks, blk_e[jnp.minimum(next_first, n_blocks - 1)], -1).astype(I32)
    tabs = (blk_e, blk_first, blk_next, n_valid)
    xs_sorted = _dispatch(dest, hn2_p, jnp.zeros((n_blocks * MOE_ROWS, D), F32), 256, 0)
    xs_sorted = _dispatch(dest, hn2_s, xs_sorted, NS, NP)
    h_sorted = _expert_up(tabs, xs_sorted, w1, w3)
    y_sorted = _expert_down(tabs, h_sorted, w2)
    tm_c = 128
    y_p = _combine(dest, x1_p, pl.BlockSpec((None, 1, D), lambda i, dr: ((i * tm_c) // T, 0, 5)), mod3, gts_p,
                   y_sorted, tm_c, 0)
    g2_s = jnp.repeat(mod[B:B + DB, 5 * D:], TS, axis=0)
    y_s = _combine(dest, x1_s, pl.BlockSpec((NS, D), lambda i, dr: (0, 0)), g2_s, gts_s, y_sorted, NS, NP)

    st = lambda a: a[None]
    k5 = lambda a, nb, t: a.reshape(1, nb, t, N_HEADS, HEAD_W)
    tok_s = lambda a: a.reshape(DB, SP, -1)[:, :TS]
    conv_out = lambda cv: cv[None, :, SUBLANES - (CONV_W - 1):, :]
    return (y_p.reshape(B, T, D), y_s.reshape(DB, TS, D),
            k5(knew_p, B, T), k5(vnew_p, B, T), st(C_p), st(n_p), st(m_p[:, :, 0]), conv_out(cv_p),
            k5(tok_s(knew_s), DB, TS), k5(tok_s(vnew_s), DB, TS), st(C_s), st(n_s), st(m_s[:, :, 0]), conv_out(cv_s))
```

```python
import functools
import math

import jax
import jax.numpy as jnp
from jax import lax
from jax.experimental import pallas as pl
from jax.experimental.pallas import tpu as pltpu

F32 = jnp.float32
BF16 = jnp.bfloat16
I32 = jnp.int32

EPS = 1e-6
ROPE_THETA = 10000.0
NEG = -1e30

LANES = 128
SUBLANES = 8
VMEM_LIMIT_CAP = 56 * 1024 * 1024

N_HEADS = 8
HEAD_W = 128
DQK = 64
N_GROUPS = 4
EXPERTS_PER_GROUP = 8
N_EXPERTS = N_GROUPS * EXPERTS_PER_GROUP
CONV_W = 4
MOE_ROWS = 256
SAMPLE_PAD = 8
MLSTM_L = 128
PAGES_PER_STEP = 16


def _cparams(sem, vmem_bytes):
    return pltpu.CompilerParams(dimension_semantics=sem,
                                vmem_limit_bytes=int(min(max(vmem_bytes, 16 * 2**20), VMEM_LIMIT_CAP)))


def _silu(x):
    return x * jax.nn.sigmoid(x)


def _dot(a, b):
    return jnp.dot(a, b, preferred_element_type=F32)


def _dot_nt(a, b):
    return lax.dot_general(a, b, (((1,), (1,)), ((), ())), preferred_element_type=F32)


def _dot_tn(a, b):
    return lax.dot_general(a, b, (((0,), (0,)), ((), ())), preferred_element_type=F32)


def _ada_kernel(c_ref, w_ref, b_ref, o_ref):
    s = _silu(c_ref[...]).astype(BF16)
    o_ref[...] = _dot(s, w_ref[...].astype(BF16)) + b_ref[...]


def _ada(c_all, w_ada, b_ada):
    rows, d = c_all.shape
    n = w_ada.shape[1]
    tn = 1024
    return pl.pallas_call(
        _ada_kernel,
        grid=(n // tn,),
        in_specs=[pl.BlockSpec((rows, d), lambda j: (0, 0)),
                  pl.BlockSpec((d, tn), lambda j: (0, j)),
                  pl.BlockSpec((1, tn), lambda j: (0, j))],
        out_specs=pl.BlockSpec((rows, tn), lambda j: (0, j)),
        out_shape=jax.ShapeDtypeStruct((rows, n), F32),
        compiler_params=_cparams(("arbitrary",), 2 * d * tn * 4 + 3 * d * tn * 2 + 2**22),
        name="ada_mod",
    )(c_all, w_ada, b_ada.reshape(1, n))


def _regroup_kernel(w_hbm, main_ref, gate_ref, buf, gbuf, sem, *, lo, ng, tr):
    j = pl.program_id(0)
    slot = j % 2

    def block_copy(jj, s):
        start = pl.multiple_of(jnp.where(jj * tr < lo, jj * tr, jj * tr + ng), SUBLANES)
        return pltpu.make_async_copy(w_hbm.at[pl.ds(start, tr)], buf.at[s], sem.at[s])

    gate_copy = pltpu.make_async_copy(w_hbm.at[pl.ds(lo, ng)], gbuf, sem.at[2])

    @pl.when(j == 0)
    def _():
        block_copy(0, 0).start()
        gate_copy.start()

    @pl.when(j + 1 < pl.num_programs(0))
    def _():
        block_copy(j + 1, 1 - slot).start()

    block_copy(j, slot).wait()
    main_ref[...] = buf[slot].astype(BF16)

    @pl.when(j == 0)
    def _():
        gate_copy.wait()
        pad = jnp.zeros((LANES - ng, gbuf.shape[1]), F32)
        gate_ref[...] = jnp.concatenate([gbuf[...], pad], axis=0).astype(BF16)


def _regroup_w_in(w_t, lo, ng):
    n, d = w_t.shape
    tr = 1024
    assert lo % tr == 0 and (n - ng) % tr == 0 and ng % SUBLANES == 0
    return pl.pallas_call(
        functools.partial(_regroup_kernel, lo=lo, ng=ng, tr=tr),
        grid=((n - ng) // tr,),
        in_specs=[pl.BlockSpec(memory_space=pl.ANY)],
        out_specs=[pl.BlockSpec((tr, d), lambda j: (j, 0)), pl.BlockSpec((LANES, d), lambda j: (0, 0))],
        out_shape=[jax.ShapeDtypeStruct((n - ng, d), BF16), jax.ShapeDtypeStruct((LANES, d), BF16)],
        scratch_shapes=[pltpu.VMEM((2, tr, d), F32), pltpu.VMEM((ng, d), F32), pltpu.SemaphoreType.DMA((3,))],
        compiler_params=_cparams(("arbitrary",), 2 * tr * d * 4 + 4 * tr * d * 2 + 8 * 2**20),
        name="w_in_layout",
    )(w_t)


def _inproj_kernel(x_ref, sc_ref, sh_ref, g_ref, w_ref, wg_ref, o_ref, og_ref, hn_ref):
    @pl.when(pl.program_id(1) == 0)
    def _():
        rows = min(x_ref.shape[0], 256)

        def chunk(c, _):
            r = pl.ds(pl.multiple_of(c * rows, rows), rows)
            x = x_ref[r, :]
            y = x * lax.rsqrt(jnp.mean(x * x, axis=-1, keepdims=True) + EPS) * g_ref[...]
            sc = sc_ref[...] if sc_ref.shape[0] == 1 else sc_ref[r, :]
            sh = sh_ref[...] if sh_ref.shape[0] == 1 else sh_ref[r, :]
            hb = (y * (1.0 + sc) + sh).astype(BF16)
            hn_ref[r, :] = hb
            og_ref[r, :] = _dot_nt(hb, wg_ref[...])
            return 0
        lax.fori_loop(0, x_ref.shape[0] // rows, chunk, 0)

    o_ref[...] = _dot_nt(hn_ref[...], w_ref[...])


def _inproj(x, mod_specs, mods, g, w, wg, tm):
    m, d = x.shape
    n = w.shape[0]
    tn = 1024
    sc_spec, sh_spec = mod_specs
    return pl.pallas_call(
        _inproj_kernel,
        grid=(m // tm, n // tn),
        in_specs=[pl.BlockSpec((tm, d), lambda i, j: (i, 0)), sc_spec, sh_spec,
                  pl.BlockSpec((1, d), lambda i, j: (0, 0)),
                  pl.BlockSpec((tn, d), lambda i, j: (j, 0)),
                  pl.BlockSpec((LANES, d), lambda i, j: (0, 0))],
        out_specs=[pl.BlockSpec((tm, tn), lambda i, j: (i, j)),
                   pl.BlockSpec((tm, LANES), lambda i, j: (i, 0))],
        out_shape=[jax.ShapeDtypeStruct((m, n), F32), jax.ShapeDtypeStruct((m, LANES), F32)],
        scratch_shapes=[pltpu.VMEM((tm, d), BF16)],
        compiler_params=_cparams(("arbitrary", "arbitrary"),
                                 2 * tm * d * 4 + 2 * d * tn * 2 + 2 * tm * tn * 4 + tm * d * 2 + 12 * 2**20),
        name="norm1_inproj",
    )(x, mods[0], mods[1], g.reshape(1, d), w, wg)


def _qkrope_kernel(q_ref, k_ref, v_ref, cos_ref, sin_ref, qg_ref, kg_ref, ko_ref, kb_ref, vo_ref, qx_ref, vx_ref,
                   *, transposed):
    cos = cos_ref[...]
    sin = sin_ref[...]
    lane = lax.broadcasted_iota(I32, (1, LANES), 1)
    first_map = lane < DQK
    first_half = (lane % DQK) < (DQK // 2)

    def norm_rope(x, g):
        x2 = x * x
        s1 = jnp.sum(jnp.where(first_map, x2, 0.0), axis=-1, keepdims=True)
        s2 = jnp.sum(jnp.where(first_map, 0.0, x2), axis=-1, keepdims=True)
        inv = jnp.where(first_map, lax.rsqrt(s1 / DQK + EPS), lax.rsqrt(s2 / DQK + EPS))
        y = x * inv * g
        partner = jnp.where(first_half, pltpu.roll(y, LANES - DQK // 2, 1), pltpu.roll(y, DQK // 2, 1))
        return y * cos + partner * sin

    for h in range(N_HEADS):
        sl = slice(h * HEAD_W, (h + 1) * HEAD_W)
        qh = norm_rope(q_ref[:, sl], qg_ref[...]) * (DQK ** -0.5)
        kh = norm_rope(k_ref[:, sl], kg_ref[...])
        vh = v_ref[:, sl]
        ko_ref[:, sl] = kh
        kb_ref[:, sl] = kh.astype(BF16)
        vo_ref[:, sl] = vh
        if transposed:
            qx_ref[sl, :] = qh.T.astype(BF16)
            vx_ref[sl, :] = vh.T.astype(BF16)
        else:
            qx_ref[:, sl] = qh.astype(BF16)
            vx_ref[:, sl] = vh.astype(BF16)


def _qkrope(proj, col0, cos_t, sin_t, qg, kg, tm, transposed_batches=0):
    m = proj.shape[0]
    w = N_HEADS * HEAD_W
    nt = cos_t.shape[0] // tm
    blk = lambda c: pl.BlockSpec((tm, w), lambda i, c=c: (i, c))
    tab = pl.BlockSpec((tm, LANES), lambda i: (i % nt, 0))
    gsp = pl.BlockSpec((1, LANES), lambda i: (0, 0))
    osp = pl.BlockSpec((tm, w), lambda i: (i, 0))
    g2 = lambda g: jnp.tile(g, 2).reshape(1, LANES)
    out_specs = [osp] * 3
    out_shape = [jax.ShapeDtypeStruct((m, w), F32), jax.ShapeDtypeStruct((m, w), BF16),
                 jax.ShapeDtypeStruct((m, w), F32)]
    if transposed_batches:
        out_specs += [pl.BlockSpec((None, None, w, tm), lambda i: (i // nt, i % nt, 0, 0))] * 2
        out_shape += [jax.ShapeDtypeStruct((transposed_batches, nt, w, tm), BF16)] * 2
    else:
        out_specs += [osp] * 2
        out_shape += [jax.ShapeDtypeStruct((m, w), BF16)] * 2
    return pl.pallas_call(
        functools.partial(_qkrope_kernel, transposed=bool(transposed_batches)),
        grid=(m // tm,),
        in_specs=[blk(col0), blk(col0 + 1), blk(col0 + 2), tab, tab, gsp, gsp],
        out_specs=out_specs,
        out_shape=out_shape,
        compiler_params=_cparams(("arbitrary",), 2 * tm * w * (3 * 4 + 2 * 4 + 5 * 2) + 2**22),
        name="qknorm_rope",
    )(proj, proj, proj, cos_t, sin_t, g2(qg), g2(kg))


def _rope_tables(pos):
    half = DQK // 2
    inv = ROPE_THETA ** (-jnp.arange(half, dtype=F32) / half)
    ang = pos.astype(F32)[:, None] * inv[None, :]
    cos = jnp.tile(jnp.cos(ang), (1, LANES // half))
    sin = jnp.sin(ang)
    sin = jnp.tile(jnp.concatenate([-sin, sin], axis=1), (1, LANES // DQK))
    return cos, sin


def _lam_value(l1, l2, l3, l4, lam_init):
    a = jnp.sum(l1[...] * l2[...], axis=-1, keepdims=True)
    b = jnp.sum(l3[...] * l4[...], axis=-1, keepdims=True)
    return jnp.exp(a) - jnp.exp(b) + lam_init


def _dattn_kernel(qt_ref, k_ref, vt_ref, l1_ref, l2_ref, l3_ref, l4_ref, og_ref, o_ref,
                  s_a, s_b, p_a, p_b, cm_ref, stat_ref, acc_ref, *, tq, lam_init):
    qi = pl.program_id(2)
    tk = tq // 2
    s_bufs, p_bufs = (s_a, s_b), (p_a, p_b)
    qt = jnp.concatenate([qt_ref[0], qt_ref[1]], axis=1)
    feat = lax.broadcasted_iota(I32, (HEAD_W, 1), 0)
    zero = jnp.zeros_like(qt)
    qcat = jnp.concatenate([jnp.where(feat < DQK, qt, zero), jnp.where(feat < DQK, zero, qt)], axis=1)
    key = lax.broadcasted_iota(I32, (tk, 2 * tq), 0)
    qry = lax.broadcasted_iota(I32, (tk, 2 * tq), 1) % tq

    def set_scores(slot, s, mask):
        if mask is not None:
            s = jnp.where(mask, s, NEG)
        s_bufs[slot][...] = s
        cm_ref[slot:slot + 1, :] = jnp.max(s, axis=0, keepdims=True)

    def scores_into(j, slot, mask=None):
        off = pl.multiple_of(j * tk, tk)
        set_scores(slot, _dot(k_ref[pl.ds(off, tk), :], qcat), mask)

    def softmax_pv(j, slot):
        pv = _dot(vt_ref[jnp.maximum(j - 1, 0)], p_bufs[1 - slot][...])
        m = stat_ref[0:1, :]
        m_new = jnp.maximum(m, cm_ref[slot:slot + 1, :])
        alpha = jnp.exp(m - m_new)
        p = jnp.exp(s_bufs[slot][...] - m_new)
        stat_ref[0:1, :] = m_new
        stat_ref[1:2, :] = alpha * stat_ref[1:2, :] + jnp.sum(p, axis=0, keepdims=True)
        p_bufs[slot][...] = p.astype(BF16)
        acc_ref[...] = alpha * (acc_ref[...] + pv)

    stat_ref[0:1, :] = jnp.full((1, 2 * tq), NEG, F32)
    stat_ref[1:2, :] = jnp.zeros((1, 2 * tq), F32)
    acc_ref[...] = jnp.zeros(acc_ref.shape, F32)
    p_b[...] = jnp.zeros(p_b.shape, BF16)
    scores_into(0, 0)

    def body(t, _):
        scores_into(2 * t + 1, 1)
        softmax_pv(2 * t, 0)
        scores_into(2 * t + 2, 0)
        softmax_pv(2 * t + 1, 1)
        return 0

    lax.fori_loop(0, qi, body, 0)
    set_scores(0, s_a[...], key <= qry)
    scores_into(2 * qi + 1, 1, key + tk <= qry)
    softmax_pv(2 * qi, 0)
    softmax_pv(2 * qi + 1, 1)
    acc = (acc_ref[...] + _dot(vt_ref[2 * qi + 1], p_b[...])) / stat_ref[1:2, :]
    lam = _lam_value(l1_ref, l2_ref, l3_ref, l4_ref, lam_init)
    o = (acc[:, :tq] - lam * acc[:, tq:]).T
    o = o * lax.rsqrt(jnp.mean(o * o, axis=-1, keepdims=True) + EPS) * og_ref[...]
    o_ref[...] = (o * (1.0 - lam_init)).astype(BF16)


def _dattn_prompt(qt, k, vt, lams, og, lam_init):
    nb, nk, _, tk = qt.shape
    t = nk * tk
    tq = 2 * tk
    nq = t // tq
    lsp = pl.BlockSpec((1, DQK), lambda b, h, i: (0, 0))
    return pl.pallas_call(
        functools.partial(_dattn_kernel, tq=tq, lam_init=lam_init),
        grid=(nb, N_HEADS, nq),
        in_specs=[pl.BlockSpec((None, 2, HEAD_W, tk), lambda b, h, i: (b, i, h, 0)),
                  pl.BlockSpec((t, HEAD_W), lambda b, h, i: (b, h)),
                  pl.BlockSpec((None, nk, HEAD_W, tk), lambda b, h, i: (b, 0, h, 0)),
                  lsp, lsp, lsp, lsp,
                  pl.BlockSpec((1, HEAD_W), lambda b, h, i: (0, 0))],
        out_specs=pl.BlockSpec((tq, HEAD_W), lambda b, h, i: (b * nq + i, h)),
        out_shape=jax.ShapeDtypeStruct(k.shape, BF16),
        scratch_shapes=[pltpu.VMEM((tk, 2 * tq), F32), pltpu.VMEM((tk, 2 * tq), F32),
                        pltpu.VMEM((tk, 2 * tq), BF16), pltpu.VMEM((tk, 2 * tq), BF16),
                        pltpu.VMEM((SUBLANES, 2 * tq), F32), pltpu.VMEM((SUBLANES, 2 * tq), F32),
                        pltpu.VMEM((HEAD_W, 2 * tq), F32)],
        compiler_params=_cparams(("arbitrary",) * 3, 32 * 2**20),
        name="diff_attn_prompt",
    )(qt, k, vt, *[x.reshape(1, DQK) for x in lams], og.reshape(1, HEAD_W))


def _dattn_paged_kernel(pt_ref, q_ref, *refs, pps, n_new, n_pages, lam_init):
    k_refs = refs[:pps]
    v_refs = refs[pps:2 * pps]
    kn_ref, vn_ref, l1_ref, l2_ref, l3_ref, l4_ref, og_ref, o_ref, s_scr, m_sc, l_sc, acc_sc = refs[2 * pps:]
    phase = pl.program_id(1)
    step_id = pl.program_id(2)
    nsteps = pl.num_programs(2)
    prow = s_scr.shape[2]
    nchunk = prow // LANES
    lane = lax.broadcasted_iota(I32, (SUBLANES, LANES), 1)
    lane_head = lane % N_HEADS
    page_head = lax.broadcasted_iota(I32, (SUBLANES, prow), 1) % N_HEADS
    row = lax.broadcasted_iota(I32, (SUBLANES, prow), 0)
    tile = lambda x: jnp.concatenate([x] * nchunk, axis=1)

    def lane_class_reduce(x, op):
        for sh in (8, 16, 32, 64):
            x = op(x, pltpu.roll(x, sh, 1))
        return x

    def score_pages(ks, first, valid):
        q = q_ref[...]
        parts = []
        for pi, kp in enumerate(ks):
            sf = _dot_nt(q, kp.astype(BF16))
            page = []
            for c in range(nchunk):
                blk = sf[:, c * LANES:(c + 1) * LANES]
                s = jnp.zeros((SUBLANES, LANES), F32)
                for h in range(N_HEADS):
                    s = jnp.where(lane_head == h, blk[h * SUBLANES:(h + 1) * SUBLANES, :], s)
                page.append(s)
            s_page = jnp.concatenate(page, axis=1)
            if valid is not None:
                s_page = jnp.where(valid, s_page, NEG)
                page = [s_page[:, c * LANES:(c + 1) * LANES] for c in range(nchunk)]
            s_scr[first + pi] = s_page
            parts += page
        m_old = m_sc[...]
        m_new = jnp.maximum(m_old, functools.reduce(jnp.maximum, parts))
        l_sc[...] = jnp.exp(m_old - m_new) * l_sc[...] + functools.reduce(jnp.add, [jnp.exp(s - m_new) for s in parts])
        m_sc[...] = m_new

    def value_pages(vs, first):
        lam = _lam_value(l1_ref, l2_ref, l3_ref, l4_ref, lam_init)
        m_full = tile(m_sc[...])
        l_full = tile(l_sc[...])
        pv = jnp.zeros(acc_sc.shape, F32)
        for pi, vp in enumerate(vs):
            e = jnp.exp(s_scr[first + pi] - m_full) / l_full
            a = jnp.where(row < n_new, e - lam * pltpu.roll(e, SUBLANES - n_new, 0), 0.0)
            pexp = jnp.concatenate([jnp.where(page_head == h, a, 0.0) for h in range(N_HEADS)], axis=0)
            pv = pv + _dot(pexp.astype(BF16), vp.astype(BF16))
        acc_sc[...] = acc_sc[...] + pv

    @pl.when(phase == 0)
    def _():
        @pl.when(step_id == 0)
        def _():
            m_sc[...] = jnp.full(m_sc.shape, NEG, F32)
            l_sc[...] = jnp.zeros(l_sc.shape, F32)

        score_pages([r[...] for r in k_refs], step_id * pps, None)

        @pl.when(step_id == nsteps - 1)
        def _():
            tok = lax.broadcasted_iota(I32, (SUBLANES, prow), 1) // N_HEADS
            score_pages([kn_ref[...]], n_pages, tok <= row % n_new)
            m_lane = m_sc[...]
            m_head = lane_class_reduce(m_lane, jnp.maximum)
            l_sc[...] = lane_class_reduce(l_sc[...] * jnp.exp(m_lane - m_head), jnp.add)
            m_sc[...] = m_head

    @pl.when(phase == 1)
    def _():
        @pl.when(step_id == 0)
        def _():
            acc_sc[...] = jnp.zeros(acc_sc.shape, F32)

        value_pages([r[...] for r in v_refs], step_id * pps)

        @pl.when(step_id == nsteps - 1)
        def _():
            value_pages([vn_ref[...]], n_pages)
            acc = acc_sc[...]
            outs = []
            for h in range(N_HEADS):
                o = acc[h * SUBLANES:(h + 1) * SUBLANES]
                o = o * lax.rsqrt(jnp.mean(o * o, axis=-1, keepdims=True) + EPS) * og_ref[...]
                outs.append(o * (1.0 - lam_init))
            o_ref[...] = jnp.concatenate(outs, axis=1).astype(BF16)


def _dattn_paged(qall, cache_k, cache_v, page_table, k_new, v_new, lams, og, n_new, lam_init):
    nb, n_pages = page_table.shape
    pps = PAGES_PER_STEP
    prow = cache_k.shape[1]
    nsteps = n_pages // pps
    kspec = lambda j: pl.BlockSpec(
        (None, prow, HEAD_W),
        lambda b, ph, s, pt, j=j: (pt[b * n_pages + (s * (1 - ph) + (nsteps - 1) * ph) * pps + j], 0, 0))
    vspec = lambda j: pl.BlockSpec(
        (None, prow, HEAD_W), lambda b, ph, s, pt, j=j: (pt[b * n_pages + s * ph * pps + j], 0, 0))
    lsp = pl.BlockSpec((1, DQK), lambda b, ph, s, pt: (0, 0))
    nspec = pl.BlockSpec((None, k_new.shape[1], HEAD_W), lambda b, ph, s, pt: (b, 0, 0))
    grid_spec = pltpu.PrefetchScalarGridSpec(
        num_scalar_prefetch=1,
        grid=(nb, 2, nsteps),
        in_specs=[pl.BlockSpec((None, N_HEADS * SUBLANES, HEAD_W), lambda b, ph, s, pt: (b, 0, 0))]
                 + [kspec(j) for j in range(pps)] + [vspec(j) for j in range(pps)]
                 + [nspec, nspec, lsp, lsp, lsp, lsp, pl.BlockSpec((1, HEAD_W), lambda b, ph, s, pt: (0, 0))],
        out_specs=pl.BlockSpec((SUBLANES, N_HEADS * HEAD_W), lambda b, ph, s, pt: (b, 0)),
        scratch_shapes=[pltpu.VMEM((n_pages + 1, SUBLANES, prow), F32),
                        pltpu.VMEM((SUBLANES, LANES), F32), pltpu.VMEM((SUBLANES, LANES), F32),
                        pltpu.VMEM((N_HEADS * SUBLANES, HEAD_W), F32)])
    return pl.pallas_call(
        functools.partial(_dattn_paged_kernel, pps=pps, n_new=n_new, n_pages=n_pages, lam_init=lam_init),
        grid_spec=grid_spec,
        out_shape=jax.ShapeDtypeStruct((nb * SUBLANES, N_HEADS * HEAD_W), BF16),
        compiler_params=_cparams(("arbitrary",) * 3,
                                 4 * pps * prow * HEAD_W * 4 + (n_pages + 1) * SUBLANES * prow * 4 + 24 * 2**20),
        name="diff_attn_paged",
    )(page_table.reshape(-1), qall, *([cache_k] * pps), *([cache_v] * pps), k_new, v_new,
      *[x.reshape(1, DQK) for x in lams], og.reshape(1, HEAD_W))


def _split3(a):
    hi = a.astype(BF16)
    r1 = a - hi.astype(F32)
    mid = r1.astype(BF16)
    lo = (r1 - mid.astype(F32)).astype(BF16)
    return hi, mid, lo


def _mlstm_kernel(qk_ref, v_ref, o_ref, gt_ref, cw_ref, cb_ref, gb_ref, og_ref,
                  C0_ref, n0_ref, m0_ref, cv0_ref,
                  h_ref, C_ref, n_ref, m_ref, cv_ref, xbuf, *, L, rows_in, n_valid):
    c = pl.program_id(1)
    width = N_HEADS * HEAD_W

    @pl.when(c == 0)
    def _():
        C_ref[...] = C0_ref[...]
        n_ref[...] = n0_ref[...]
        m_ref[...] = m0_ref[...]
        xbuf[0:SUBLANES, :] = cv0_ref[...]

    def padded(ref):
        x = ref[...]
        if rows_in < L:
            x = jnp.concatenate([x, jnp.zeros((L - rows_in, x.shape[1]), x.dtype)], axis=0)
        return x

    xbuf[SUBLANES:SUBLANES + L, :] = padded(qk_ref)
    taps = [xbuf[SUBLANES - (CONV_W - 1) + j:SUBLANES - (CONV_W - 1) + j + L, :] * cw_ref[j:j + 1, :]
            for j in range(CONV_W)]
    conv = cb_ref[...] + functools.reduce(jnp.add, taps)
    cv_ref[...] = xbuf[n_valid:n_valid + SUBLANES, :]
    xbuf[0:SUBLANES, :] = xbuf[L:L + SUBLANES, :]
    conv = _silu(conv)
    q_all = conv[:, :width].astype(BF16)
    k_all = conv[:, width:] * (HEAD_W ** -0.5)
    v_all = padded(v_ref).astype(BF16)
    o_all = padded(o_ref)

    g = padded(gt_ref) + gb_ref[...]
    lane = lax.broadcasted_iota(I32, (L, LANES), 1)
    rowi = lax.broadcasted_iota(I32, (L, LANES), 0)
    is_f = (lane >= N_HEADS) & (lane < 2 * N_HEADS)
    logsig = jnp.minimum(g, 0.0) - jnp.log1p(jnp.exp(-jnp.abs(g)))
    a = jnp.where(is_f, logsig, g)
    if n_valid < L:
        a = jnp.where(rowi < n_valid, a, jnp.where(is_f, 0.0, NEG))
    tri = (lax.broadcasted_iota(I32, (L, L), 1) <= lax.broadcasted_iota(I32, (L, L), 0))
    tri_b = tri.astype(BF16)
    cum = functools.reduce(jnp.add, [_dot(tri_b, part) for part in _split3(jnp.where(is_f, a, 0.0))])
    a_t = a.T
    cum_t = cum.T

    m_all = m_ref[...]
    n_all = n_ref[...]
    m_rows, n_rows = [], []
    for h in range(N_HEADS):
        sl = slice(h * HEAD_W, (h + 1) * HEAD_W)
        b_col = cum[:, N_HEADS + h:N_HEADS + h + 1]
        b_row = cum_t[N_HEADS + h:N_HEADS + h + 1, :]
        i_col = a[:, h:h + 1]
        i_row = a_t[h:h + 1, :]
        m_prev = m_all[h:h + 1, 0:1]
        d = jnp.where(tri, (b_col - b_row) + i_row, NEG)
        inter = b_col + m_prev
        mt = jnp.maximum(inter, jnp.max(d, axis=-1, keepdims=True))
        w_intra = jnp.exp(d - mt)
        w_inter = jnp.exp(inter - mt)
        qh, kh, vh = q_all[:, sl], k_all[:, sl], v_all[:, sl]
        C_h = C_ref[h]
        n_h = n_all[h:h + 1, :]
        s = _dot_nt(qh, kh.astype(BF16)) * w_intra
        num = _dot(s.astype(BF16), vh) + w_inter * _dot(qh, C_h.astype(BF16))
        den = (jnp.sum(s, axis=-1, keepdims=True)
               + w_inter * jnp.sum(qh.astype(F32) * n_h.astype(BF16).astype(F32), axis=-1, keepdims=True))
        hh = num / jnp.maximum(jnp.abs(den), jnp.exp(-mt))
        m_last = mt[L - 1:L, :]
        b_last = b_col[L - 1:L, :]
        gk = jnp.exp((b_last - b_col) + i_col - m_last) * kh
        decay = jnp.exp(b_last + m_prev - m_last)
        C_ref[h] = decay * C_h + _dot_tn(gk.astype(BF16), vh)
        n_rows.append(decay * n_h + jnp.sum(gk, axis=0, keepdims=True))
        m_rows.append(jnp.broadcast_to(m_last, (1, LANES)))
        hn = hh * lax.rsqrt(jnp.mean(hh * hh, axis=-1, keepdims=True) + EPS) * og_ref[...]
        out = hn * jax.nn.sigmoid(o_all[:, sl])
        h_ref[:, sl] = out[:rows_in].astype(BF16)
    n_ref[...] = jnp.concatenate(n_rows, axis=0)
    m_ref[...] = jnp.concatenate(m_rows, axis=0)


def _mlstm(proj, gates, conv_w, conv_b, gate_b, og, C0, n0, m0, cv0, nb, rows_in, nchunks, n_valid):
    L = MLSTM_L
    w = N_HEADS * HEAD_W
    row = lambda cb: (lambda b, c: (b * nchunks + c, cb))
    st3 = lambda b, c: (b, 0, 0)
    kern = functools.partial(_mlstm_kernel, L=L, rows_in=rows_in, n_valid=n_valid)
    return pl.pallas_call(
        kern,
        grid=(nb, nchunks),
        in_specs=[pl.BlockSpec((rows_in, 2 * w), lambda b, c: (b * nchunks + c, 0)),
                  pl.BlockSpec((rows_in, w), row(2)), pl.BlockSpec((rows_in, w), row(3)),
                  pl.BlockSpec((rows_in, LANES), lambda b, c: (b * nchunks + c, 0)),
                  pl.BlockSpec((SUBLANES, 2 * w), lambda b, c: (0, 0)),
                  pl.BlockSpec((1, 2 * w), lambda b, c: (0, 0)),
                  pl.BlockSpec((1, LANES), lambda b, c: (0, 0)),
                  pl.BlockSpec((1, HEAD_W), lambda b, c: (0, 0)),
                  pl.BlockSpec((None, N_HEADS, HEAD_W, HEAD_W), lambda b, c: (b, 0, 0, 0)),
                  pl.BlockSpec((None, N_HEADS, HEAD_W), st3),
                  pl.BlockSpec((None, N_HEADS, LANES), st3),
                  pl.BlockSpec((None, SUBLANES, 2 * w), st3)],
        out_specs=[pl.BlockSpec((rows_in, w), lambda b, c: (b * nchunks + c, 0)),
                   pl.BlockSpec((None, N_HEADS, HEAD_W, HEAD_W), lambda b, c: (b, 0, 0, 0)),
                   pl.BlockSpec((None, N_HEADS, HEAD_W), st3),
                   pl.BlockSpec((None, N_HEADS, LANES), st3),
                   pl.BlockSpec((None, SUBLANES, 2 * w), st3)],
        out_shape=[jax.ShapeDtypeStruct((nb * nchunks * rows_in, w), BF16),
                   jax.ShapeDtypeStruct((nb, N_HEADS, HEAD_W, HEAD_W), F32),
                   jax.ShapeDtypeStruct((nb, N_HEADS, HEAD_W), F32),
                   jax.ShapeDtypeStruct((nb, N_HEADS, LANES), F32),
                   jax.ShapeDtypeStruct((nb, SUBLANES, 2 * w), F32)],
        scratch_shapes=[pltpu.VMEM((SUBLANES + L, 2 * w), F32)],
        compiler_params=_cparams(("arbitrary", "arbitrary"), 40 * 2**20),
        name="mlstm_chunkwise",
    )(proj, proj, proj, gates, conv_w, conv_b, gate_b, og, C0, n0, m0, cv0)


def _outproj_kernel(hm_ref, ha_ref, x_ref, g1_ref, sc_ref, sh_ref, ng_ref, wo_ref, wr_ref, br_ref,
                    x1_ref, hn_ref, ids_ref, gates_ref):
    half = hm_ref.shape[1]
    mix = _dot(hm_ref[...], wo_ref[0:half, :]) + _dot(ha_ref[...], wo_ref[half:2 * half, :])
    x1 = x_ref[...] + g1_ref[...] * mix
    x1_ref[...] = x1
    y = x1 * lax.rsqrt(jnp.mean(x1 * x1, axis=-1, keepdims=True) + EPS) * ng_ref[...]
    hn = y * (1.0 + sc_ref[...]) + sh_ref[...]
    hn_ref[...] = hn
    logits = _dot(hn.astype(BF16), wr_ref[...]) + br_ref[...]

    lane = lax.broadcasted_iota(I32, logits.shape, 1).astype(F32)
    big = 1000.0
    is_g = lane < N_GROUPS
    gl = jnp.where(is_g, logits, NEG)
    gmax = jnp.max(gl, axis=-1, keepdims=True)
    g_idx = jnp.min(jnp.where(gl == gmax, lane, big), axis=-1, keepdims=True)
    g_p = 1.0 / jnp.sum(jnp.where(is_g, jnp.exp(gl - gmax), 0.0), axis=-1, keepdims=True)
    e_lo = N_GROUPS + EXPERTS_PER_GROUP * g_idx
    in_grp = (lane >= e_lo) & (lane < e_lo + EXPERTS_PER_GROUP)
    el = jnp.where(in_grp, logits, NEG)
    ex = jnp.where(in_grp, jnp.exp(el - jnp.max(el, axis=-1, keepdims=True)), 0.0)
    p = jnp.where(in_grp, ex / jnp.sum(ex, axis=-1, keepdims=True), -1.0)
    top1 = jnp.max(p, axis=-1, keepdims=True)
    idx1 = jnp.min(jnp.where(p == top1, lane, big), axis=-1, keepdims=True)
    p2 = jnp.where(lane == idx1, -1.0, p)
    top2 = jnp.max(p2, axis=-1, keepdims=True)
    idx2 = jnp.min(jnp.where(p2 == top2, lane, big), axis=-1, keepdims=True)
    tsum = top1 + top2
    ids_ref[...] = jnp.where(lane == 0.0, idx1 - N_GROUPS,
                             jnp.where(lane == 1.0, idx2 - N_GROUPS, -1.0)).astype(I32)
    gates_ref[...] = jnp.where(lane == 0.0, g_p * top1 / tsum, jnp.where(lane == 1.0, g_p * top2 / tsum, 0.0))


def _outproj(hm, ha, x, mod_specs, mods, ng, wo, wr, br, tm):
    m, d = x.shape
    half = hm.shape[1]
    row = lambda wdt: pl.BlockSpec((tm, wdt), lambda i: (i, 0))
    const = lambda shp: pl.BlockSpec(shp, lambda i: (0, 0))
    return pl.pallas_call(
        _outproj_kernel,
        grid=(m // tm,),
        in_specs=[row(half), row(half), row(d), *mod_specs, const((1, d)), const((d, d)),
                  const((d, LANES)), const((1, LANES))],
        out_specs=[row(d), row(d), row(LANES), row(LANES)],
        out_shape=[jax.ShapeDtypeStruct((m, d), F32), jax.ShapeDtypeStruct((m, d), F32),
                   jax.ShapeDtypeStruct((m, LANES), I32), jax.ShapeDtypeStruct((m, LANES), F32)],
        compiler_params=_cparams(("arbitrary",), 2 * d * d * 2 + 8 * tm * d * 4 + 8 * 2**20),
        name="outproj_norm2_route",
    )(hm, ha, x, *mods, ng.reshape(1, d), wo, wr, br)


def _rank_kernel(ids_ref, dest_ref, cnt_ref, carry, pstart):
    pss = pl.program_id(0)
    i = pl.program_id(1)
    ids = ids_ref[...]
    tm = ids.shape[0]
    lane = lax.broadcasted_iota(I32, ids.shape, 1)
    o0 = lane == ids[:, 0:1]
    o1 = lane == ids[:, 1:2]
    onehot = jnp.where(o0 | o1, 1.0, 0.0)
    col_counts = jnp.sum(onehot, axis=0, keepdims=True)

    @pl.when((pss == 0) & (i == 0))
    def _():
        carry[...] = jnp.zeros(carry.shape, F32)

    @pl.when(pss == 0)
    def _():
        carry[...] = carry[...] + col_counts

    @pl.when((pss == 1) & (i == 0))
    def _():
        cnt = carry[...]
        cnt_ref[...] = cnt
        blocks = jnp.floor((cnt + (MOE_ROWS - 1)) * (1.0 / MOE_ROWS))
        earlier = lax.broadcasted_iota(I32, (LANES, LANES), 0) < lax.broadcasted_iota(I32, (LANES, LANES), 1)
        pstart[...] = _dot(blocks.astype(BF16), earlier.astype(BF16)) * MOE_ROWS
        carry[...] = jnp.zeros(carry.shape, F32)

    @pl.when(pss == 1)
    def _():
        strict = lax.broadcasted_iota(I32, (tm, tm), 1) < lax.broadcasted_iota(I32, (tm, tm), 0)
        before = _dot(strict.astype(BF16), onehot.astype(BF16)) + carry[0:1, :] + pstart[0:1, :]
        r0 = jnp.sum(jnp.where(o0, before, 0.0), axis=-1, keepdims=True)
        r1 = jnp.sum(jnp.where(o1, before, 0.0), axis=-1, keepdims=True)
        dest_ref[...] = jnp.where(lane == 0, r0, jnp.where(lane == 1, r1, 0.0)).astype(I32)
        carry[...] = carry[...] + col_counts


def _ranks(ids, tm):
    m = ids.shape[0]
    return pl.pallas_call(
        _rank_kernel,
        grid=(2, m // tm),
        in_specs=[pl.BlockSpec((tm, LANES), lambda p, i: (i, 0))],
        out_specs=[pl.BlockSpec((tm, LANES), lambda p, i: (i * p, 0)),
                   pl.BlockSpec((SUBLANES, LANES), lambda p, i: (0, 0))],
        out_shape=[jax.ShapeDtypeStruct((m, LANES), I32), jax.ShapeDtypeStruct((SUBLANES, LANES), F32)],
        scratch_shapes=[pltpu.VMEM((SUBLANES, LANES), F32), pltpu.VMEM((SUBLANES, LANES), F32)],
        compiler_params=_cparams(("arbitrary", "arbitrary"), 16 * 2**20),
        name="moe_ranks",
    )(ids)


def _row_copies(idx_ref, t0, tm, copy):
    def body(t, _):
        for k in range(2):
            copy(t, k, idx_ref[2 * (t0 + t) + k]).start()
        return 0
    lax.fori_loop(0, tm, body, 0, unroll=8)


def _dispatch_kernel(dest_ref, h_ref, xin_ref, xs_ref, sem, *, tm, base):
    del xin_ref
    t0 = base + pl.program_id(0) * tm
    _row_copies(dest_ref, t0, tm,
                lambda t, k, d: pltpu.make_async_copy(h_ref.at[pl.ds(t, 1)], xs_ref.at[pl.ds(d, 1)], sem))
    for _ in range(2):
        pltpu.make_async_copy(h_ref, xs_ref.at[pl.ds(0, tm)], sem).wait()


def _dispatch(dest_flat, hn, xs, tm, base):
    m, d = hn.shape
    any_spec = pl.BlockSpec(memory_space=pl.ANY)
    grid_spec = pltpu.PrefetchScalarGridSpec(
        num_scalar_prefetch=1, grid=(m // tm,),
        in_specs=[pl.BlockSpec((tm, d), lambda i, dr: (i, 0)), any_spec], out_specs=any_spec,
        scratch_shapes=[pltpu.SemaphoreType.DMA])
    return pl.pallas_call(
        functools.partial(_dispatch_kernel, tm=tm, base=base),
        grid_spec=grid_spec,
        out_shape=jax.ShapeDtypeStruct(xs.shape, xs.dtype),
        input_output_aliases={2: 0},
        compiler_params=_cparams(("arbitrary",), 4 * tm * d * 4 + 8 * 2**20),
        name="moe_dispatch",
    )(dest_flat, hn, xs)


def _expert_weights(b, be_ref, bf_ref, nx_ref, w_hbm, w_stage, w_bf16, sem):
    def copies(e):
        return [pltpu.make_async_copy(src.at[0, e], dst, sem.at[i])
                for i, (src, dst) in enumerate(zip(w_hbm, w_stage))]

    @pl.when(b == 0)
    def _():
        for c in copies(be_ref[0]):
            c.start()

    @pl.when(bf_ref[b] == 1)
    def _():
        for c in copies(be_ref[b]):
            c.wait()
        for stage, wb in zip(w_stage, w_bf16):
            wb[...] = stage[...].astype(BF16)

        @pl.when(nx_ref[b] >= 0)
        def _():
            for c in copies(nx_ref[b]):
                c.start()


def _expert_up_kernel(be_ref, bf_ref, nx_ref, nv_ref, x_ref, w1_hbm, w3_hbm, h_ref, w1f, w3f, w1b, w3b, sem):
    b = pl.program_id(0)
    _expert_weights(b, be_ref, bf_ref, nx_ref, (w1_hbm, w3_hbm), (w1f, w3f), (w1b, w3b), sem)

    @pl.when(b < nv_ref[0])
    def _():
        x = x_ref[...].astype(BF16)
        h_ref[...] = (_silu(_dot(x, w1b[...])) * _dot(x, w3b[...])).astype(BF16)

    @pl.when(b >= nv_ref[0])
    def _():
        h_ref[...] = jnp.zeros(h_ref.shape, BF16)


def _expert_up(tabs, xs, w1, w3):
    rows, d = xs.shape
    f = w1.shape[-1]
    nb = rows // MOE_ROWS
    any_spec = pl.BlockSpec(memory_space=pl.ANY)
    grid_spec = pltpu.PrefetchScalarGridSpec(
        num_scalar_prefetch=4, grid=(nb,),
        in_specs=[pl.BlockSpec((MOE_ROWS, d), lambda b, *_: (b, 0)), any_spec, any_spec],
        out_specs=pl.BlockSpec((MOE_ROWS, f), lambda b, *_: (b, 0)),
        scratch_shapes=[pltpu.VMEM((d, f), F32), pltpu.VMEM((d, f), F32),
                        pltpu.VMEM((d, f), BF16), pltpu.VMEM((d, f), BF16), pltpu.SemaphoreType.DMA((2,))])
    return pl.pallas_call(
        _expert_up_kernel,
        grid_spec=grid_spec,
        out_shape=jax.ShapeDtypeStruct((rows, f), BF16),
        compiler_params=_cparams(("arbitrary",), 2 * d * f * 6 + 4 * MOE_ROWS * d * 4 + 8 * 2**20),
        name="moe_expert_up",
    )(*tabs, xs, w1, w3)


def _expert_down_kernel(be_ref, bf_ref, nx_ref, nv_ref, h_ref, w2_hbm, y_ref, w2f, w2b, sem):
    b = pl.program_id(0)
    _expert_weights(b, be_ref, bf_ref, nx_ref, (w2_hbm,), (w2f,), (w2b,), sem)

    @pl.when(b < nv_ref[0])
    def _():
        y_ref[...] = _dot(h_ref[...], w2b[...])

    @pl.when(b >= nv_ref[0])
    def _():
        y_ref[...] = jnp.zeros(y_ref.shape, F32)


def _expert_down(tabs, hs, w2):
    rows, f = hs.shape
    d = w2.shape[-1]
    nb = rows // MOE_ROWS
    grid_spec = pltpu.PrefetchScalarGridSpec(
        num_scalar_prefetch=4, grid=(nb,),
        in_specs=[pl.BlockSpec((MOE_ROWS, f), lambda b, *_: (b, 0)), pl.BlockSpec(memory_space=pl.ANY)],
        out_specs=pl.BlockSpec((MOE_ROWS, d), lambda b, *_: (b, 0)),
        scratch_shapes=[pltpu.VMEM((f, d), F32), pltpu.VMEM((f, d), BF16), pltpu.SemaphoreType.DMA((1,))])
    return pl.pallas_call(
        _expert_down_kernel,
        grid_spec=grid_spec,
        out_shape=jax.ShapeDtypeStruct((rows, d), F32),
        compiler_params=_cparams(("arbitrary",), f * d * 6 + 6 * MOE_ROWS * d * 4 + 8 * 2**20),
        name="moe_expert_down",
    )(*tabs, hs, w2)


def _combine_kernel(dest_ref, x1_ref, g2_ref, gates_ref, y_ref, o_ref, ybuf, sem, *, tm, base):
    i = pl.program_id(0)
    slot = i % 2

    def gather(step, s):
        _row_copies(dest_ref, base + step * tm, tm,
                    lambda t, k, d: pltpu.make_async_copy(y_ref.at[pl.ds(d, 1)], ybuf.at[s, k, pl.ds(t, 1)],
                                                          sem.at[s]))

    @pl.when(i == 0)
    def _():
        gather(0, 0)

    @pl.when(i + 1 < pl.num_programs(0))
    def _():
        gather(i + 1, 1 - slot)

    for k in range(2):
        pltpu.make_async_copy(y_ref.at[pl.ds(0, tm)], ybuf.at[slot, k], sem.at[slot]).wait()
    gates = gates_ref[...]
    ff = gates[:, 0:1] * ybuf[slot, 0] + gates[:, 1:2] * ybuf[slot, 1]
    o_ref[...] = x1_ref[...] + g2_ref[...] * ff


def _combine(dest_flat, x1, g2_spec, g2, gates, y, tm, base):
    m, d = x1.shape
    grid_spec = pltpu.PrefetchScalarGridSpec(
        num_scalar_prefetch=1, grid=(m // tm,),
        in_specs=[pl.BlockSpec((tm, d), lambda i, dr: (i, 0)), g2_spec,
                  pl.BlockSpec((tm, LANES), lambda i, dr: (i, 0)),
                  pl.BlockSpec(memory_space=pl.ANY)],
        out_specs=pl.BlockSpec((tm, d), lambda i, dr: (i, 0)),
        scratch_shapes=[pltpu.VMEM((2, 2, tm, d), F32), pltpu.SemaphoreType.DMA((2,))])
    return pl.pallas_call(
        functools.partial(_combine_kernel, tm=tm, base=base),
        grid_spec=grid_spec,
        out_shape=jax.ShapeDtypeStruct((m, d), F32),
        compiler_params=_cparams(("arbitrary",), 8 * tm * d * 4 + 8 * 2**20),
        name="moe_combine",
    )(dest_flat, x1, g2, gates, y)


def kernel(x_prompt, x_sample, cache_k, cache_v, page_table, state_mlstm_C, state_mlstm_n, state_mlstm_m, state_conv,
           c_prompt, c_sample, w_ada, b_ada, norm1_g, norm2_g, w_in, conv_w, conv_b, b_igate, b_fgate, mlstm_out_g,
           q_norm_g, k_norm_g, lam_q1, lam_k1, lam_q2, lam_k2, diff_out_g, w_out, w_grp, b_grp, w_rt, b_rt, w1, w3, w2):
    B, T, D = x_prompt.shape
    DB, TS, _ = x_sample.shape
    depth = w_ada.shape[0]
    assert depth == 1, "single-layer trunk"
    n_pages, page = page_table.shape[1], cache_k.shape[2]
    past = n_pages * page
    W = N_HEADS * HEAD_W
    lam_init = 0.8 - 0.6 * math.exp(-0.3 * 0)
    NP, NS = B * T, DB * TS
    SP = SAMPLE_PAD

    c_all = jnp.concatenate([c_prompt, c_sample, jnp.zeros((16 - B - DB, D), F32)], axis=0)
    w_main, w_gate = _regroup_w_in(w_in[0].T, 4 * W, 2 * N_HEADS)
    gate_b = jnp.pad(jnp.concatenate([b_igate[0], b_fgate[0]]), (0, LANES - 2 * N_HEADS)).reshape(1, LANES)
    cw = jnp.pad(conv_w[0], ((0, SUBLANES - CONV_W), (0, 0)))
    cb = conv_b[0].reshape(1, 2 * W)
    og_m = mlstm_out_g[0].reshape(1, HEAD_W)
    wo = w_out[0].astype(BF16)
    w_route = jnp.concatenate([w_grp[0], w_rt[0].transpose(1, 0, 2).reshape(D, N_EXPERTS)], axis=1)
    w_route = jnp.pad(w_route, ((0, 0), (0, LANES - N_GROUPS - N_EXPERTS))).astype(BF16)
    b_route = jnp.pad(jnp.concatenate([b_grp[0], b_rt[0].reshape(-1)]), (0, LANES - N_GROUPS - N_EXPERTS)).reshape(1, LANES)
    lams = (lam_q1[0], lam_k1[0], lam_q2[0], lam_k2[0])

    mod = _ada(c_all, w_ada[0], b_ada[0])
    mod3 = mod.reshape(16, 1, 6 * D)
    mod_s = jnp.repeat(mod[B:B + DB], SP, axis=0)

    def p_mod(chunk, tm):
        return pl.BlockSpec((None, 1, D), lambda i, *_: ((i * tm) // T, 0, chunk))

    def s_mod(chunk, tm):
        return pl.BlockSpec((tm, D), lambda i, *_: (i, chunk))

    xp = x_prompt.reshape(NP, D)
    tm_in = 1024
    proj_p, gates_p = _inproj(xp, (p_mod(1, tm_in), p_mod(0, tm_in)), (mod3, mod3), norm1_g[0], w_main, w_gate, tm_in)
    cos_p, sin_p = _rope_tables(jnp.arange(T, dtype=I32))
    knew_p, kb_p, vnew_p, qt_p, vt_p = _qkrope(proj_p, 4, cos_p, sin_p, q_norm_g[0], k_norm_g[0], 256, B)
    ha_p = _dattn_prompt(qt_p, kb_p, vt_p, lams, diff_out_g[0], lam_init)
    zeros = lambda *s: jnp.zeros(s, F32)
    hm_p, C_p, n_p, m_p, cv_p = _mlstm(
        proj_p, gates_p, cw, cb, gate_b, og_m,
        zeros(B, N_HEADS, HEAD_W, HEAD_W), zeros(B, N_HEADS, HEAD_W), zeros(B, N_HEADS, LANES),
        zeros(B, SUBLANES, 2 * W), B, MLSTM_L, T // MLSTM_L, MLSTM_L)
    tm_o = 256
    x1_p, hn2_p, ids_p, gts_p = _outproj(
        hm_p, ha_p, xp, (p_mod(2, tm_o), p_mod(4, tm_o), p_mod(3, tm_o)), (mod3, mod3, mod3),
        norm2_g[0], wo, w_route, b_route, tm_o)

    MS = DB * SP
    xs_pad = jnp.pad(x_sample, ((0, 0), (0, SP - TS), (0, 0))).reshape(MS, D)
    proj_s, gates_s = _inproj(xs_pad, (s_mod(1, MS), s_mod(0, MS)), (mod_s, mod_s), norm1_g[0], w_main, w_gate, MS)
    pos_s = past + (jnp.arange(MS, dtype=I32) % SP)
    cos_s, sin_s = _rope_tables(pos_s)
    knew_s, kb_s, vnew_s, q_s, vb_s = _qkrope(proj_s, 4, cos_s, sin_s, q_norm_g[0], k_norm_g[0], MS)
    qr = q_s.reshape(DB, SP, N_HEADS, 2, DQK)[:, :TS].transpose(0, 2, 3, 1, 4)
    zq = jnp.zeros_like(qr[:, :, 0])
    qall = jnp.stack([jnp.concatenate([qr[:, :, 0], zq], axis=-1),
                      jnp.concatenate([zq, qr[:, :, 1]], axis=-1)], axis=2).reshape(DB, N_HEADS * 2 * TS, HEAD_W)
    new_rows = page * N_HEADS
    pad_new = lambda a: jnp.pad(a.reshape(DB, SP * N_HEADS, HEAD_W), ((0, 0), (0, new_rows - SP * N_HEADS), (0, 0)))
    ha_s = _dattn_paged(qall, cache_k[0].reshape(-1, new_rows, HEAD_W), cache_v[0].reshape(-1, new_rows, HEAD_W),
                        page_table, pad_new(kb_s), pad_new(vb_s), lams, diff_out_g[0], TS, lam_init)
    m0_s = jnp.broadcast_to(state_mlstm_m[0][:, :, None], (DB, N_HEADS, LANES))
    cv0_s = jnp.pad(state_conv[0], ((0, 0), (SUBLANES - (CONV_W - 1), 0), (0, 0)))
    hm_s, C_s, n_s, m_s, cv_s = _mlstm(proj_s, gates_s, cw, cb, gate_b, og_m, state_mlstm_C[0], state_mlstm_n[0],
                                       m0_s, cv0_s, DB, SP, 1, TS)
    x1_s, hn2_s, ids_s, gts_s = _outproj(
        hm_s, ha_s, xs_pad, (s_mod(2, MS), s_mod(4, MS), s_mod(3, MS)), (mod_s, mod_s, mod_s),
        norm2_g[0], wo, w_route, b_route, MS)
    real = lambda a: a.reshape(DB, SP, -1)[:, :TS].reshape(NS, -1)
    x1_s, hn2_s, ids_s, gts_s = real(x1_s), real(hn2_s), real(ids_s), real(gts_s)

    NT = NP + NS
    tm_r = 256
    n_rank = -(-NT // tm_r) * tm_r
    ids_all = jnp.concatenate([ids_p, ids_s, jnp.full((n_rank - NT, LANES), -1, I32)], axis=0)
    dest2, cnt = _ranks(ids_all, tm_r)
    dest = dest2[:NT, :2].reshape(-1)
    counts = cnt[0, :N_EXPERTS].astype(I32)
    pend = jnp.cumsum((counts + MOE_ROWS - 1) // MOE_ROWS)
    n_blocks = (2 * NT + N_EXPERTS * (MOE_ROWS - 1)) // MOE_ROWS
    n_valid = pend[-1:]
    blk = jnp.minimum(jnp.arange(n_blocks, dtype=I32), n_valid[0] - 1)
    blk_e = jnp.sum((blk[:, None] >= pend[None, :]).astype(I32), axis=1)
    blk_first = jnp.concatenate([jnp.ones((1,), I32), (blk_e[1:] != blk_e[:-1]).astype(I32)])
    first_pos = jnp.where(blk_first == 1, jnp.arange(n_blocks, dtype=I32), n_blocks)
    next_first = jnp.concatenate([lax.cummin(first_pos, reverse=True)[1:], jnp.full((1,), n_blocks, I32)])
    blk_next = jnp.where(next_first < n_blocks, blk_e[jnp.minimum(next_first, n_blocks - 1)], -1).astype(I32)
    tabs = (blk_e, blk_first, blk_next, n_valid)
    xs_sorted = _dispatch(dest, hn2_p, jnp.zeros((n_blocks * MOE_ROWS, D), F32), 256, 0)
    xs_sorted = _dispatch(dest, hn2_s, xs_sorted, NS, NP)
    h_sorted = _expert_up(tabs, xs_sorted, w1, w3)
    y_sorted = _expert_down(tabs, h_sorted, w2)
    tm_c = 128
    y_p = _combine(dest, x1_p, pl.BlockSpec((None, 1, D), lambda i, dr: ((i * tm_c) // T, 0, 5)), mod3, gts_p,
                   y_sorted, tm_c, 0)
    g2_s = jnp.repeat(mod[B:B + DB, 5 * D:], TS, axis=0)
    y_s = _combine(dest, x1_s, pl.BlockSpec((NS, D), lambda i, dr: (0, 0)), g2_s, gts_s, y_sorted, NS, NP)

    st = lambda a: a[None]
    k5 = lambda a, nb, t: a.reshape(1, nb, t, N_HEADS, HEAD_W)
    tok_s = lambda a: a.reshape(DB, SP, -1)[:, :TS]
    conv_out = lambda cv: cv[None, :, SUBLANES - (CONV_W - 1):, :]
    return (y_p.reshape(B, T, D), y_s.reshape(DB, TS, D),
            k5(knew_p, B, T), k5(vnew_p, B, T), st(C_p), st(n_p), st(m_p[:, :, 0]), conv_out(cv_p),
            k5(tok_s(knew_s), DB, TS), k5(tok_s(vnew_s), DB, TS), st(C_s), st(n_s), st(m_s[:, :, 0]), conv_out(cv_s))
```

```python
import functools
import math

import jax
import jax.numpy as jnp
from jax import lax
from jax.experimental import pallas as pl
from jax.experimental.pallas import tpu as pltpu

F32 = jnp.float32
BF16 = jnp.bfloat16
I32 = jnp.int32

EPS = 1e-6
ROPE_THETA = 10000.0
NEG = -1e30

LANES = 128
SUBLANES = 8
VMEM_LIMIT_CAP = 56 * 1024 * 1024

N_HEADS = 8
HEAD_W = 128
DQK = 64
N_GROUPS = 4
EXPERTS_PER_GROUP = 8
N_EXPERTS = N_GROUPS * EXPERTS_PER_GROUP
CONV_W = 4
MOE_ROWS = 256
SAMPLE_PAD = 8
MLSTM_L = 128
PAGES_PER_STEP = 16


def _cparams(sem, vmem_bytes):
    return pltpu.CompilerParams(dimension_semantics=sem,
                                vmem_limit_bytes=int(min(max(vmem_bytes, 16 * 2**20), VMEM_LIMIT_CAP)))


def _silu(x):
    return x * jax.nn.sigmoid(x)


def _dot(a, b):
    return jnp.dot(a, b, preferred_element_type=F32)


def _pack_bf16_pairs(x):
    n = x.shape[1] // 2
    bits = lax.bitcast_convert_type(x.astype(BF16).astype(F32), jnp.uint32)
    return bits[:, :n] | (bits[:, n:] >> 16)


def _unpack_bf16_pairs(w):
    left = lax.bitcast_convert_type(w & jnp.uint32(0xFFFF0000), F32)
    right = lax.bitcast_convert_type(w << 16, F32)
    return left.astype(BF16), right.astype(BF16)


def _dot_nt(a, b):
    return lax.dot_general(a, b, (((1,), (1,)), ((), ())), preferred_element_type=F32)


def _dot_tn(a, b):
    return lax.dot_general(a, b, (((0,), (0,)), ((), ())), preferred_element_type=F32)


def _ada_kernel(c_ref, w_ref, b_ref, o_ref):
    s = _silu(c_ref[...]).astype(BF16)
    o_ref[...] = _dot(s, w_ref[...].astype(BF16)) + b_ref[...]


def _ada(c_all, w_ada, b_ada):
    rows, d = c_all.shape
    n = w_ada.shape[1]
    tn = 1024
    return pl.pallas_call(
        _ada_kernel,
        grid=(n // tn,),
        in_specs=[pl.BlockSpec((rows, d), lambda j: (0, 0)),
                  pl.BlockSpec((d, tn), lambda j: (0, j)),
                  pl.BlockSpec((1, tn), lambda j: (0, j))],
        out_specs=pl.BlockSpec((rows, tn), lambda j: (0, j)),
        out_shape=jax.ShapeDtypeStruct((rows, n), F32),
        compiler_params=_cparams(("arbitrary",), 2 * d * tn * 4 + 3 * d * tn * 2 + 2**22),
        name="ada_mod",
    )(c_all, w_ada, b_ada.reshape(1, n))


def _regroup_kernel(w_hbm, main_ref, gate_ref, buf, gbuf, sem, *, lo, ng, tr):
    j = pl.program_id(0)
    slot = j % 2

    def block_copy(jj, s):
        start = pl.multiple_of(jnp.where(jj * tr < lo, jj * tr, jj * tr + ng), SUBLANES)
        return pltpu.make_async_copy(w_hbm.at[pl.ds(start, tr)], buf.at[s], sem.at[s])

    gate_copy = pltpu.make_async_copy(w_hbm.at[pl.ds(lo, ng)], gbuf, sem.at[2])

    @pl.when(j == 0)
    def _():
        block_copy(0, 0).start()
        gate_copy.start()

    @pl.when(j + 1 < pl.num_programs(0))
    def _():
        block_copy(j + 1, 1 - slot).start()

    block_copy(j, slot).wait()
    main_ref[...] = buf[slot].astype(BF16)

    @pl.when(j == 0)
    def _():
        gate_copy.wait()
        pad = jnp.zeros((LANES - ng, gbuf.shape[1]), F32)
        gate_ref[...] = jnp.concatenate([gbuf[...], pad], axis=0).astype(BF16)


def _regroup_w_in(w_t, lo, ng):
    n, d = w_t.shape
    tr = 1024
    assert lo % tr == 0 and (n - ng) % tr == 0 and ng % SUBLANES == 0
    return pl.pallas_call(
        functools.partial(_regroup_kernel, lo=lo, ng=ng, tr=tr),
        grid=((n - ng) // tr,),
        in_specs=[pl.BlockSpec(memory_space=pl.ANY)],
        out_specs=[pl.BlockSpec((tr, d), lambda j: (j, 0)), pl.BlockSpec((LANES, d), lambda j: (0, 0))],
        out_shape=[jax.ShapeDtypeStruct((n - ng, d), BF16), jax.ShapeDtypeStruct((LANES, d), BF16)],
        scratch_shapes=[pltpu.VMEM((2, tr, d), F32), pltpu.VMEM((ng, d), F32), pltpu.SemaphoreType.DMA((3,))],
        compiler_params=_cparams(("arbitrary",), 2 * tr * d * 4 + 4 * tr * d * 2 + 8 * 2**20),
        name="w_in_layout",
    )(w_t)


def _inproj_kernel(x_ref, sc_ref, sh_ref, g_ref, w_ref, wg_ref, o_ref, og_ref, hn_ref):
    @pl.when(pl.program_id(1) == 0)
    def _():
        rows = min(x_ref.shape[0], 256)

        def chunk(c, _):
            r = pl.ds(pl.multiple_of(c * rows, rows), rows)
            x = x_ref[r, :]
            y = x * lax.rsqrt(jnp.mean(x * x, axis=-1, keepdims=True) + EPS) * g_ref[...]
            sc = sc_ref[...] if sc_ref.shape[0] == 1 else sc_ref[r, :]
            sh = sh_ref[...] if sh_ref.shape[0] == 1 else sh_ref[r, :]
            hb = (y * (1.0 + sc) + sh).astype(BF16)
            hn_ref[r, :] = hb
            og_ref[r, :] = _dot_nt(hb, wg_ref[...])
            return 0
        lax.fori_loop(0, x_ref.shape[0] // rows, chunk, 0)

    o_ref[...] = _dot_nt(hn_ref[...], w_ref[...])


def _inproj(x, mod_specs, mods, g, w, wg, tm):
    m, d = x.shape
    n = w.shape[0]
    tn = 1024
    sc_spec, sh_spec = mod_specs
    return pl.pallas_call(
        _inproj_kernel,
        grid=(m // tm, n // tn),
        in_specs=[pl.BlockSpec((tm, d), lambda i, j: (i, 0)), sc_spec, sh_spec,
                  pl.BlockSpec((1, d), lambda i, j: (0, 0)),
                  pl.BlockSpec((tn, d), lambda i, j: (j, 0)),
                  pl.BlockSpec((LANES, d), lambda i, j: (0, 0))],
        out_specs=[pl.BlockSpec((tm, tn), lambda i, j: (i, j)),
                   pl.BlockSpec((tm, LANES), lambda i, j: (i, 0))],
        out_shape=[jax.ShapeDtypeStruct((m, n), F32), jax.ShapeDtypeStruct((m, LANES), F32)],
        scratch_shapes=[pltpu.VMEM((tm, d), BF16)],
        compiler_params=_cparams(("arbitrary", "arbitrary"),
                                 2 * tm * d * 4 + 2 * d * tn * 2 + 2 * tm * tn * 4 + tm * d * 2 + 12 * 2**20),
        name="norm1_inproj",
    )(x, mods[0], mods[1], g.reshape(1, d), w, wg)


def _qkrope_kernel(q_ref, k_ref, v_ref, cos_ref, sin_ref, qg_ref, kg_ref, ko_ref, kb_ref, vo_ref, qx_ref, vx_ref,
                   *, transposed):
    cos = cos_ref[...]
    sin = sin_ref[...]
    lane = lax.broadcasted_iota(I32, (1, LANES), 1)
    first_map = lane < DQK
    first_half = (lane % DQK) < (DQK // 2)

    def norm_rope(x, g):
        x2 = x * x
        s1 = jnp.sum(jnp.where(first_map, x2, 0.0), axis=-1, keepdims=True)
        s2 = jnp.sum(jnp.where(first_map, 0.0, x2), axis=-1, keepdims=True)
        inv = jnp.where(first_map, lax.rsqrt(s1 / DQK + EPS), lax.rsqrt(s2 / DQK + EPS))
        y = x * inv * g
        partner = jnp.where(first_half, pltpu.roll(y, LANES - DQK // 2, 1), pltpu.roll(y, DQK // 2, 1))
        return y * cos + partner * sin

    for h in range(N_HEADS):
        sl = slice(h * HEAD_W, (h + 1) * HEAD_W)
        qh = norm_rope(q_ref[:, sl], qg_ref[...]) * (DQK ** -0.5)
        kh = norm_rope(k_ref[:, sl], kg_ref[...])
        vh = v_ref[:, sl]
        ko_ref[:, sl] = kh
        kb_ref[:, sl] = kh.astype(BF16)
        vo_ref[:, sl] = vh
        if transposed:
            qx_ref[sl, :] = qh.T.astype(BF16)
            vx_ref[sl, :] = vh.T.astype(BF16)
        else:
            qx_ref[:, sl] = qh.astype(BF16)
            vx_ref[:, sl] = vh.astype(BF16)


def _qkrope(proj, col0, cos_t, sin_t, qg, kg, tm, transposed_batches=0):
    m = proj.shape[0]
    w = N_HEADS * HEAD_W
    nt = cos_t.shape[0] // tm
    blk = lambda c: pl.BlockSpec((tm, w), lambda i, c=c: (i, c))
    tab = pl.BlockSpec((tm, LANES), lambda i: (i % nt, 0))
    gsp = pl.BlockSpec((1, LANES), lambda i: (0, 0))
    osp = pl.BlockSpec((tm, w), lambda i: (i, 0))
    g2 = lambda g: jnp.tile(g, 2).reshape(1, LANES)
    out_specs = [osp] * 3
    out_shape = [jax.ShapeDtypeStruct((m, w), F32), jax.ShapeDtypeStruct((m, w), BF16),
                 jax.ShapeDtypeStruct((m, w), F32)]
    if transposed_batches:
        out_specs += [pl.BlockSpec((None, None, w, tm), lambda i: (i // nt, i % nt, 0, 0))] * 2
        out_shape += [jax.ShapeDtypeStruct((transposed_batches, nt, w, tm), BF16)] * 2
    else:
        out_specs += [osp] * 2
        out_shape += [jax.ShapeDtypeStruct((m, w), BF16)] * 2
    return pl.pallas_call(
        functools.partial(_qkrope_kernel, transposed=bool(transposed_batches)),
        grid=(m // tm,),
        in_specs=[blk(col0), blk(col0 + 1), blk(col0 + 2), tab, tab, gsp, gsp],
        out_specs=out_specs,
        out_shape=out_shape,
        compiler_params=_cparams(("arbitrary",), 2 * tm * w * (3 * 4 + 2 * 4 + 5 * 2) + 2**22),
        name="qknorm_rope",
    )(proj, proj, proj, cos_t, sin_t, g2(qg), g2(kg))


def _rope_tables(pos):
    half = DQK // 2
    inv = ROPE_THETA ** (-jnp.arange(half, dtype=F32) / half)
    ang = pos.astype(F32)[:, None] * inv[None, :]
    cos = jnp.tile(jnp.cos(ang), (1, LANES // half))
    sin = jnp.sin(ang)
    sin = jnp.tile(jnp.concatenate([-sin, sin], axis=1), (1, LANES // DQK))
    return cos, sin


def _lam_value(l1, l2, l3, l4, lam_init):
    a = jnp.sum(l1[...] * l2[...], axis=-1, keepdims=True)
    b = jnp.sum(l3[...] * l4[...], axis=-1, keepdims=True)
    return jnp.exp(a) - jnp.exp(b) + lam_init


def _dattn_kernel(qt_ref, k_ref, vt_ref, l1_ref, l2_ref, l3_ref, l4_ref, og_ref, o_ref,
                  s_a, s_b, p_a, p_b, cm_ref, stat_ref, acc_ref, *, tq, lam_init):
    qi = pl.program_id(2)
    tk = tq // 2
    s_bufs, p_bufs = (s_a, s_b), (p_a, p_b)
    qt = jnp.concatenate([qt_ref[0], qt_ref[1]], axis=1)
    feat = lax.broadcasted_iota(I32, (HEAD_W, 1), 0)
    zero = jnp.zeros_like(qt)
    qcat = jnp.concatenate([jnp.where(feat < DQK, qt, zero), jnp.where(feat < DQK, zero, qt)], axis=1)
    key = lax.broadcasted_iota(I32, (tk, 2 * tq), 0)
    qry = lax.broadcasted_iota(I32, (tk, 2 * tq), 1) % tq

    def set_scores(slot, s, mask):
        if mask is not None:
            s = jnp.where(mask, s, NEG)
        s_bufs[slot][...] = s
        cm_ref[slot:slot + 1, :] = jnp.max(s, axis=0, keepdims=True)

    def scores_into(j, slot, mask=None):
        off = pl.multiple_of(j * tk, tk)
        set_scores(slot, _dot(k_ref[pl.ds(off, tk), :], qcat), mask)

    def softmax_pv(j, slot):
        pv = _dot(vt_ref[jnp.maximum(j - 1, 0)], p_bufs[1 - slot][...])
        m = stat_ref[0:1, :]
        m_new = jnp.maximum(m, cm_ref[slot:slot + 1, :])
        alpha = jnp.exp(m - m_new)
        p = jnp.exp(s_bufs[slot][...] - m_new)
        stat_ref[0:1, :] = m_new
        stat_ref[1:2, :] = alpha * stat_ref[1:2, :] + jnp.sum(p, axis=0, keepdims=True)
        p_bufs[slot][...] = p.astype(BF16)
        acc_ref[...] = alpha * (acc_ref[...] + pv)

    stat_ref[0:1, :] = jnp.full((1, 2 * tq), NEG, F32)
    stat_ref[1:2, :] = jnp.zeros((1, 2 * tq), F32)
    acc_ref[...] = jnp.zeros(acc_ref.shape, F32)
    p_b[...] = jnp.zeros(p_b.shape, BF16)
    scores_into(0, 0)

    def body(t, _):
        scores_into(2 * t + 1, 1)
        softmax_pv(2 * t, 0)
        scores_into(2 * t + 2, 0)
        softmax_pv(2 * t + 1, 1)
        return 0

    lax.fori_loop(0, qi, body, 0)
    set_scores(0, s_a[...], key <= qry)
    scores_into(2 * qi + 1, 1, key + tk <= qry)
    softmax_pv(2 * qi, 0)
    softmax_pv(2 * qi + 1, 1)
    acc = (acc_ref[...] + _dot(vt_ref[2 * qi + 1], p_b[...])) / stat_ref[1:2, :]
    lam = _lam_value(l1_ref, l2_ref, l3_ref, l4_ref, lam_init)
    o = (acc[:, :tq] - lam * acc[:, tq:]).T
    o = o * lax.rsqrt(jnp.mean(o * o, axis=-1, keepdims=True) + EPS) * og_ref[...]
    o_ref[...] = (o * (1.0 - lam_init)).astype(BF16)


def _dattn_prompt(qt, k, vt, lams, og, lam_init):
    nb, nk, _, tk = qt.shape
    t = nk * tk
    tq = 2 * tk
    nq = t // tq
    lsp = pl.BlockSpec((1, DQK), lambda b, h, i: (0, 0))
    return pl.pallas_call(
        functools.partial(_dattn_kernel, tq=tq, lam_init=lam_init),
        grid=(nb, N_HEADS, nq),
        in_specs=[pl.BlockSpec((None, 2, HEAD_W, tk), lambda b, h, i: (b, i, h, 0)),
                  pl.BlockSpec((t, HEAD_W), lambda b, h, i: (b, h)),
                  pl.BlockSpec((None, nk, HEAD_W, tk), lambda b, h, i: (b, 0, h, 0)),
                  lsp, lsp, lsp, lsp,
                  pl.BlockSpec((1, HEAD_W), lambda b, h, i: (0, 0))],
        out_specs=pl.BlockSpec((tq, HEAD_W), lambda b, h, i: (b * nq + i, h)),
        out_shape=jax.ShapeDtypeStruct(k.shape, BF16),
        scratch_shapes=[pltpu.VMEM((tk, 2 * tq), F32), pltpu.VMEM((tk, 2 * tq), F32),
                        pltpu.VMEM((tk, 2 * tq), BF16), pltpu.VMEM((tk, 2 * tq), BF16),
                        pltpu.VMEM((SUBLANES, 2 * tq), F32), pltpu.VMEM((SUBLANES, 2 * tq), F32),
                        pltpu.VMEM((HEAD_W, 2 * tq), F32)],
        compiler_params=_cparams(("arbitrary",) * 3, 32 * 2**20),
        name="diff_attn_prompt",
    )(qt, k, vt, *[x.reshape(1, DQK) for x in lams], og.reshape(1, HEAD_W))


def _dattn_paged_kernel(pt_ref, q_ref, *refs, pps, n_new, n_pages, lam_init):
    k_refs = refs[:pps]
    v_refs = refs[pps:2 * pps]
    kn_ref, vn_ref, l1_ref, l2_ref, l3_ref, l4_ref, og_ref, o_ref, s_scr, m_sc, l_sc, acc_sc = refs[2 * pps:]
    phase = pl.program_id(1)
    step_id = pl.program_id(2)
    nsteps = pl.num_programs(2)
    prow = s_scr.shape[2]
    nchunk = prow // LANES
    lane = lax.broadcasted_iota(I32, (SUBLANES, LANES), 1)
    lane_head = lane % N_HEADS
    page_head = lax.broadcasted_iota(I32, (SUBLANES, prow), 1) % N_HEADS
    row = lax.broadcasted_iota(I32, (SUBLANES, prow), 0)
    tile = lambda x: jnp.concatenate([x] * nchunk, axis=1)

    def lane_class_reduce(x, op):
        for sh in (8, 16, 32, 64):
            x = op(x, pltpu.roll(x, sh, 1))
        return x

    def score_pages(ks, first, valid):
        q = q_ref[...]
        parts = []
        for pi, kp in enumerate(ks):
            sf = _dot_nt(q, kp.astype(BF16))
            page = []
            for c in range(nchunk):
                blk = sf[:, c * LANES:(c + 1) * LANES]
                s = jnp.zeros((SUBLANES, LANES), F32)
                for h in range(N_HEADS):
                    s = jnp.where(lane_head == h, blk[h * SUBLANES:(h + 1) * SUBLANES, :], s)
                page.append(s)
            s_page = jnp.concatenate(page, axis=1)
            if valid is not None:
                s_page = jnp.where(valid, s_page, NEG)
                page = [s_page[:, c * LANES:(c + 1) * LANES] for c in range(nchunk)]
            s_scr[first + pi] = s_page
            parts += page
        m_old = m_sc[...]
        m_new = jnp.maximum(m_old, functools.reduce(jnp.maximum, parts))
        l_sc[...] = jnp.exp(m_old - m_new) * l_sc[...] + functools.reduce(jnp.add, [jnp.exp(s - m_new) for s in parts])
        m_sc[...] = m_new

    def value_pages(vs, first):
        lam = _lam_value(l1_ref, l2_ref, l3_ref, l4_ref, lam_init)
        m_full = tile(m_sc[...])
        l_full = tile(l_sc[...])
        pv = jnp.zeros(acc_sc.shape, F32)
        for pi, vp in enumerate(vs):
            e = jnp.exp(s_scr[first + pi] - m_full) / l_full
            a = jnp.where(row < n_new, e - lam * pltpu.roll(e, SUBLANES - n_new, 0), 0.0)
            pexp = jnp.concatenate([jnp.where(page_head == h, a, 0.0) for h in range(N_HEADS)], axis=0)
            pv = pv + _dot(pexp.astype(BF16), vp.astype(BF16))
        acc_sc[...] = acc_sc[...] + pv

    @pl.when(phase == 0)
    def _():
        @pl.when(step_id == 0)
        def _():
            m_sc[...] = jnp.full(m_sc.shape, NEG, F32)
            l_sc[...] = jnp.zeros(l_sc.shape, F32)

        score_pages([r[...] for r in k_refs], step_id * pps, None)

        @pl.when(step_id == nsteps - 1)
        def _():
            tok = lax.broadcasted_iota(I32, (SUBLANES, prow), 1) // N_HEADS
            score_pages([kn_ref[...]], n_pages, tok <= row % n_new)
            m_lane = m_sc[...]
            m_head = lane_class_reduce(m_lane, jnp.maximum)
            l_sc[...] = lane_class_reduce(l_sc[...] * jnp.exp(m_lane - m_head), jnp.add)
            m_sc[...] = m_head

    @pl.when(phase == 1)
    def _():
        @pl.when(step_id == 0)
        def _():
            acc_sc[...] = jnp.zeros(acc_sc.shape, F32)

        value_pages([r[...] for r in v_refs], step_id * pps)

        @pl.when(step_id == nsteps - 1)
        def _():
            value_pages([vn_ref[...]], n_pages)
            acc = acc_sc[...]
            outs = []
            for h in range(N_HEADS):
                o = acc[h * SUBLANES:(h + 1) * SUBLANES]
                o = o * lax.rsqrt(jnp.mean(o * o, axis=-1, keepdims=True) + EPS) * og_ref[...]
                outs.append(o * (1.0 - lam_init))
            o_ref[...] = jnp.concatenate(outs, axis=1).astype(BF16)


def _dattn_paged(qall, cache_k, cache_v, page_table, k_new, v_new, lams, og, n_new, lam_init):
    nb, n_pages = page_table.shape
    pps = PAGES_PER_STEP
    prow = cache_k.shape[1]
    nsteps = n_pages // pps
    kspec = lambda j: pl.BlockSpec(
        (None, prow, HEAD_W),
        lambda b, ph, s, pt, j=j: (pt[b * n_pages + (s * (1 - ph) + (nsteps - 1) * ph) * pps + j], 0, 0))
    vspec = lambda j: pl.BlockSpec(
        (None, prow, HEAD_W), lambda b, ph, s, pt, j=j: (pt[b * n_pages + s * ph * pps + j], 0, 0))
    lsp = pl.BlockSpec((1, DQK), lambda b, ph, s, pt: (0, 0))
    nspec = pl.BlockSpec((None, k_new.shape[1], HEAD_W), lambda b, ph, s, pt: (b, 0, 0))
    grid_spec = pltpu.PrefetchScalarGridSpec(
        num_scalar_prefetch=1,
        grid=(nb, 2, nsteps),
        in_specs=[pl.BlockSpec((None, N_HEADS * SUBLANES, HEAD_W), lambda b, ph, s, pt: (b, 0, 0))]
                 + [kspec(j) for j in range(pps)] + [vspec(j) for j in range(pps)]
                 + [nspec, nspec, lsp, lsp, lsp, lsp, pl.BlockSpec((1, HEAD_W), lambda b, ph, s, pt: (0, 0))],
        out_specs=pl.BlockSpec((SUBLANES, N_HEADS * HEAD_W), lambda b, ph, s, pt: (b, 0)),
        scratch_shapes=[pltpu.VMEM((n_pages + 1, SUBLANES, prow), F32),
                        pltpu.VMEM((SUBLANES, LANES), F32), pltpu.VMEM((SUBLANES, LANES), F32),
                        pltpu.VMEM((N_HEADS * SUBLANES, HEAD_W), F32)])
    return pl.pallas_call(
        functools.partial(_dattn_paged_kernel, pps=pps, n_new=n_new, n_pages=n_pages, lam_init=lam_init),
        grid_spec=grid_spec,
        out_shape=jax.ShapeDtypeStruct((nb * SUBLANES, N_HEADS * HEAD_W), BF16),
        compiler_params=_cparams(("arbitrary",) * 3,
                                 4 * pps * prow * HEAD_W * 4 + (n_pages + 1) * SUBLANES * prow * 4 + 24 * 2**20),
        name="diff_attn_paged",
    )(page_table.reshape(-1), qall, *([cache_k] * pps), *([cache_v] * pps), k_new, v_new,
      *[x.reshape(1, DQK) for x in lams], og.reshape(1, HEAD_W))


def _split3(a):
    hi = a.astype(BF16)
    r1 = a - hi.astype(F32)
    mid = r1.astype(BF16)
    lo = (r1 - mid.astype(F32)).astype(BF16)
    return hi, mid, lo


def _mlstm_kernel(qk_ref, v_ref, o_ref, gt_ref, cw_ref, cb_ref, gb_ref, og_ref,
                  C0_ref, n0_ref, m0_ref, cv0_ref,
                  h_ref, C_ref, n_ref, m_ref, cv_ref, xbuf, *, L, rows_in, n_valid):
    c = pl.program_id(1)
    width = N_HEADS * HEAD_W

    @pl.when(c == 0)
    def _():
        C_ref[...] = C0_ref[...]
        n_ref[...] = n0_ref[...]
        m_ref[...] = m0_ref[...]
        xbuf[0:SUBLANES, :] = cv0_ref[...]

    def padded(ref):
        x = ref[...]
        if rows_in < L:
            x = jnp.concatenate([x, jnp.zeros((L - rows_in, x.shape[1]), x.dtype)], axis=0)
        return x

    xbuf[SUBLANES:SUBLANES + L, :] = padded(qk_ref)
    taps = [xbuf[SUBLANES - (CONV_W - 1) + j:SUBLANES - (CONV_W - 1) + j + L, :] * cw_ref[j:j + 1, :]
            for j in range(CONV_W)]
    conv = cb_ref[...] + functools.reduce(jnp.add, taps)
    cv_ref[...] = xbuf[n_valid:n_valid + SUBLANES, :]
    xbuf[0:SUBLANES, :] = xbuf[L:L + SUBLANES, :]
    conv = _silu(conv)
    q_all = conv[:, :width].astype(BF16)
    k_all = conv[:, width:] * (HEAD_W ** -0.5)
    v_all = padded(v_ref).astype(BF16)
    o_all = padded(o_ref)

    g = padded(gt_ref) + gb_ref[...]
    lane = lax.broadcasted_iota(I32, (L, LANES), 1)
    rowi = lax.broadcasted_iota(I32, (L, LANES), 0)
    is_f = (lane >= N_HEADS) & (lane < 2 * N_HEADS)
    logsig = jnp.minimum(g, 0.0) - jnp.log1p(jnp.exp(-jnp.abs(g)))
    a = jnp.where(is_f, logsig, g)
    if n_valid < L:
        a = jnp.where(rowi < n_valid, a, jnp.where(is_f, 0.0, NEG))
    tri = (lax.broadcasted_iota(I32, (L, L), 1) <= lax.broadcasted_iota(I32, (L, L), 0))
    tri_b = tri.astype(BF16)
    cum = functools.reduce(jnp.add, [_dot(tri_b, part) for part in _split3(jnp.where(is_f, a, 0.0))])
    a_t = a.T
    cum_t = cum.T

    m_all = m_ref[...]
    n_all = n_ref[...]
    m_rows, n_rows = [], []
    for h in range(N_HEADS):
        sl = slice(h * HEAD_W, (h + 1) * HEAD_W)
        b_col = cum[:, N_HEADS + h:N_HEADS + h + 1]
        b_row = cum_t[N_HEADS + h:N_HEADS + h + 1, :]
        i_col = a[:, h:h + 1]
        i_row = a_t[h:h + 1, :]
        m_prev = m_all[h:h + 1, 0:1]
        d = jnp.where(tri, (b_col - b_row) + i_row, NEG)
        inter = b_col + m_prev
        mt = jnp.maximum(inter, jnp.max(d, axis=-1, keepdims=True))
        w_intra = jnp.exp(d - mt)
        w_inter = jnp.exp(inter - mt)
        qh, kh, vh = q_all[:, sl], k_all[:, sl], v_all[:, sl]
        C_h = C_ref[h]
        n_h = n_all[h:h + 1, :]
        s = _dot_nt(qh, kh.astype(BF16)) * w_intra
        num = _dot(s.astype(BF16), vh) + w_inter * _dot(qh, C_h.astype(BF16))
        den = (jnp.sum(s, axis=-1, keepdims=True)
               + w_inter * jnp.sum(qh.astype(F32) * n_h.astype(BF16).astype(F32), axis=-1, keepdims=True))
        hh = num / jnp.maximum(jnp.abs(den), jnp.exp(-mt))
        m_last = mt[L - 1:L, :]
        b_last = b_col[L - 1:L, :]
        gk = jnp.exp((b_last - b_col) + i_col - m_last) * kh
        decay = jnp.exp(b_last + m_prev - m_last)
        C_ref[h] = decay * C_h + _dot_tn(gk.astype(BF16), vh)
        n_rows.append(decay * n_h + jnp.sum(gk, axis=0, keepdims=True))
        m_rows.append(jnp.broadcast_to(m_last, (1, LANES)))
        hn = hh * lax.rsqrt(jnp.mean(hh * hh, axis=-1, keepdims=True) + EPS) * og_ref[...]
        out = hn * jax.nn.sigmoid(o_all[:, sl])
        h_ref[:, sl] = out[:rows_in].astype(BF16)
    n_ref[...] = jnp.concatenate(n_rows, axis=0)
    m_ref[...] = jnp.concatenate(m_rows, axis=0)


def _mlstm(proj, gates, conv_w, conv_b, gate_b, og, C0, n0, m0, cv0, nb, rows_in, nchunks, n_valid):
    L = MLSTM_L
    w = N_HEADS * HEAD_W
    row = lambda cb: (lambda b, c: (b * nchunks + c, cb))
    st3 = lambda b, c: (b, 0, 0)
    kern = functools.partial(_mlstm_kernel, L=L, rows_in=rows_in, n_valid=n_valid)
    return pl.pallas_call(
        kern,
        grid=(nb, nchunks),
        in_specs=[pl.BlockSpec((rows_in, 2 * w), lambda b, c: (b * nchunks + c, 0)),
                  pl.BlockSpec((rows_in, w), row(2)), pl.BlockSpec((rows_in, w), row(3)),
                  pl.BlockSpec((rows_in, LANES), lambda b, c: (b * nchunks + c, 0)),
                  pl.BlockSpec((SUBLANES, 2 * w), lambda b, c: (0, 0)),
                  pl.BlockSpec((1, 2 * w), lambda b, c: (0, 0)),
                  pl.BlockSpec((1, LANES), lambda b, c: (0, 0)),
                  pl.BlockSpec((1, HEAD_W), lambda b, c: (0, 0)),
                  pl.BlockSpec((None, N_HEADS, HEAD_W, HEAD_W), lambda b, c: (b, 0, 0, 0)),
                  pl.BlockSpec((None, N_HEADS, HEAD_W), st3),
                  pl.BlockSpec((None, N_HEADS, LANES), st3),
                  pl.BlockSpec((None, SUBLANES, 2 * w), st3)],
        out_specs=[pl.BlockSpec((rows_in, w), lambda b, c: (b * nchunks + c, 0)),
                   pl.BlockSpec((None, N_HEADS, HEAD_W, HEAD_W), lambda b, c: (b, 0, 0, 0)),
                   pl.BlockSpec((None, N_HEADS, HEAD_W), st3),
                   pl.BlockSpec((None, N_HEADS, LANES), st3),
                   pl.BlockSpec((None, SUBLANES, 2 * w), st3)],
        out_shape=[jax.ShapeDtypeStruct((nb * nchunks * rows_in, w), BF16),
                   jax.ShapeDtypeStruct((nb, N_HEADS, HEAD_W, HEAD_W), F32),
                   jax.ShapeDtypeStruct((nb, N_HEADS, HEAD_W), F32),
                   jax.ShapeDtypeStruct((nb, N_HEADS, LANES), F32),
                   jax.ShapeDtypeStruct((nb, SUBLANES, 2 * w), F32)],
        scratch_shapes=[pltpu.VMEM((SUBLANES + L, 2 * w), F32)],
        compiler_params=_cparams(("arbitrary", "arbitrary"), 40 * 2**20),
        name="mlstm_chunkwise",
    )(proj, proj, proj, gates, conv_w, conv_b, gate_b, og, C0, n0, m0, cv0)


def _outproj_kernel(hm_ref, ha_ref, x_ref, g1_ref, sc_ref, sh_ref, ng_ref, wo_ref, wr_ref, br_ref,
                    x1_ref, hn_ref, ids_ref, gates_ref):
    half = hm_ref.shape[1]
    mix = _dot(hm_ref[...], wo_ref[0:half, :]) + _dot(ha_ref[...], wo_ref[half:2 * half, :])
    x1 = x_ref[...] + g1_ref[...] * mix
    x1_ref[...] = x1
    y = x1 * lax.rsqrt(jnp.mean(x1 * x1, axis=-1, keepdims=True) + EPS) * ng_ref[...]
    hn = y * (1.0 + sc_ref[...]) + sh_ref[...]
    hn_ref[...] = _pack_bf16_pairs(hn)
    logits = _dot(hn.astype(BF16), wr_ref[...]) + br_ref[...]

    lane = lax.broadcasted_iota(I32, logits.shape, 1).astype(F32)
    big = 1000.0
    is_g = lane < N_GROUPS
    gl = jnp.where(is_g, logits, NEG)
    gmax = jnp.max(gl, axis=-1, keepdims=True)
    g_idx = jnp.min(jnp.where(gl == gmax, lane, big), axis=-1, keepdims=True)
    g_p = 1.0 / jnp.sum(jnp.where(is_g, jnp.exp(gl - gmax), 0.0), axis=-1, keepdims=True)
    e_lo = N_GROUPS + EXPERTS_PER_GROUP * g_idx
    in_grp = (lane >= e_lo) & (lane < e_lo + EXPERTS_PER_GROUP)
    el = jnp.where(in_grp, logits, NEG)
    ex = jnp.where(in_grp, jnp.exp(el - jnp.max(el, axis=-1, keepdims=True)), 0.0)
    p = jnp.where(in_grp, ex / jnp.sum(ex, axis=-1, keepdims=True), -1.0)
    top1 = jnp.max(p, axis=-1, keepdims=True)
    idx1 = jnp.min(jnp.where(p == top1, lane, big), axis=-1, keepdims=True)
    p2 = jnp.where(lane == idx1, -1.0, p)
    top2 = jnp.max(p2, axis=-1, keepdims=True)
    idx2 = jnp.min(jnp.where(p2 == top2, lane, big), axis=-1, keepdims=True)
    tsum = top1 + top2
    ids_ref[...] = jnp.where(lane == 0.0, idx1 - N_GROUPS,
                             jnp.where(lane == 1.0, idx2 - N_GROUPS, -1.0)).astype(I32)
    gates_ref[...] = jnp.where(lane == 0.0, g_p * top1 / tsum, jnp.where(lane == 1.0, g_p * top2 / tsum, 0.0))


def _outproj(hm, ha, x, mod_specs, mods, ng, wo, wr, br, tm):
    m, d = x.shape
    half = hm.shape[1]
    row = lambda wdt: pl.BlockSpec((tm, wdt), lambda i: (i, 0))
    const = lambda shp: pl.BlockSpec(shp, lambda i: (0, 0))
    return pl.pallas_call(
        _outproj_kernel,
        grid=(m // tm,),
        in_specs=[row(half), row(half), row(d), *mod_specs, const((1, d)), const((d, d)),
                  const((d, LANES)), const((1, LANES))],
        out_specs=[row(d), row(d // 2), row(LANES), row(LANES)],
        out_shape=[jax.ShapeDtypeStruct((m, d), F32), jax.ShapeDtypeStruct((m, d // 2), jnp.uint32),
                   jax.ShapeDtypeStruct((m, LANES), I32), jax.ShapeDtypeStruct((m, LANES), F32)],
        compiler_params=_cparams(("arbitrary",), 2 * d * d * 2 + 8 * tm * d * 4 + 8 * 2**20),
        name="outproj_norm2_route",
    )(hm, ha, x, *mods, ng.reshape(1, d), wo, wr, br)


def _rank_kernel(ids_ref, dest_ref, cnt_ref, carry, pstart):
    pss = pl.program_id(0)
    i = pl.program_id(1)
    ids = ids_ref[...]
    tm = ids.shape[0]
    lane = lax.broadcasted_iota(I32, ids.shape, 1)
    o0 = lane == ids[:, 0:1]
    o1 = lane == ids[:, 1:2]
    onehot = jnp.where(o0 | o1, 1.0, 0.0)
    col_counts = jnp.sum(onehot, axis=0, keepdims=True)

    @pl.when((pss == 0) & (i == 0))
    def _():
        carry[...] = jnp.zeros(carry.shape, F32)

    @pl.when(pss == 0)
    def _():
        carry[...] = carry[...] + col_counts

    @pl.when((pss == 1) & (i == 0))
    def _():
        cnt = carry[...]
        cnt_ref[...] = cnt
        blocks = jnp.floor((cnt + (MOE_ROWS - 1)) * (1.0 / MOE_ROWS))
        earlier = lax.broadcasted_iota(I32, (LANES, LANES), 0) < lax.broadcasted_iota(I32, (LANES, LANES), 1)
        pstart[...] = _dot(blocks.astype(BF16), earlier.astype(BF16)) * MOE_ROWS
        carry[...] = jnp.zeros(carry.shape, F32)

    @pl.when(pss == 1)
    def _():
        strict = lax.broadcasted_iota(I32, (tm, tm), 1) < lax.broadcasted_iota(I32, (tm, tm), 0)
        before = _dot(strict.astype(BF16), onehot.astype(BF16)) + carry[0:1, :] + pstart[0:1, :]
        r0 = jnp.sum(jnp.where(o0, before, 0.0), axis=-1, keepdims=True)
        r1 = jnp.sum(jnp.where(o1, before, 0.0), axis=-1, keepdims=True)
        dest_ref[...] = jnp.where(lane == 0, r0, jnp.where(lane == 1, r1, 0.0)).astype(I32)
        carry[...] = carry[...] + col_counts


def _ranks(ids, tm):
    m = ids.shape[0]
    return pl.pallas_call(
        _rank_kernel,
        grid=(2, m // tm),
        in_specs=[pl.BlockSpec((tm, LANES), lambda p, i: (i, 0))],
        out_specs=[pl.BlockSpec((tm, LANES), lambda p, i: (i * p, 0)),
                   pl.BlockSpec((SUBLANES, LANES), lambda p, i: (0, 0))],
        out_shape=[jax.ShapeDtypeStruct((m, LANES), I32), jax.ShapeDtypeStruct((SUBLANES, LANES), F32)],
        scratch_shapes=[pltpu.VMEM((SUBLANES, LANES), F32), pltpu.VMEM((SUBLANES, LANES), F32)],
        compiler_params=_cparams(("arbitrary", "arbitrary"), 16 * 2**20),
        name="moe_ranks",
    )(ids)


def _row_copies(idx_ref, t0, tm, copy):
    def body(t, _):
        for k in range(2):
            copy(t, k, idx_ref[2 * (t0 + t) + k]).start()
        return 0
    lax.fori_loop(0, tm, body, 0, unroll=8)


def _dispatch_kernel(dest_ref, h_ref, xin_ref, xs_ref, sem, *, tm, base):
    del xin_ref
    t0 = base + pl.program_id(0) * tm
    _row_copies(dest_ref, t0, tm,
                lambda t, k, d: pltpu.make_async_copy(h_ref.at[pl.ds(t, 1)], xs_ref.at[pl.ds(d, 1)], sem))
    for _ in range(2):
        pltpu.make_async_copy(h_ref, xs_ref.at[pl.ds(0, tm)], sem).wait()


def _dispatch(dest_flat, hn, xs, tm, base):
    m, d = hn.shape
    any_spec = pl.BlockSpec(memory_space=pl.ANY)
    grid_spec = pltpu.PrefetchScalarGridSpec(
        num_scalar_prefetch=1, grid=(m // tm,),
        in_specs=[pl.BlockSpec((tm, d), lambda i, dr: (i, 0)), any_spec], out_specs=any_spec,
        scratch_shapes=[pltpu.SemaphoreType.DMA])
    return pl.pallas_call(
        functools.partial(_dispatch_kernel, tm=tm, base=base),
        grid_spec=grid_spec,
        out_shape=jax.ShapeDtypeStruct(xs.shape, xs.dtype),
        input_output_aliases={2: 0},
        compiler_params=_cparams(("arbitrary",), 4 * tm * d * 4 + 8 * 2**20),
        name="moe_dispatch",
    )(dest_flat, hn, xs)


def _expert_weights(b, be_ref, bf_ref, nx_ref, w_hbm, w_stage, w_bf16, sem):
    def copies(e):
        return [pltpu.make_async_copy(src.at[0, e], dst, sem.at[i])
                for i, (src, dst) in enumerate(zip(w_hbm, w_stage))]

    @pl.when(b == 0)
    def _():
        for c in copies(be_ref[0]):
            c.start()

    @pl.when(bf_ref[b] == 1)
    def _():
        for c in copies(be_ref[b]):
            c.wait()
        for stage, wb in zip(w_stage, w_bf16):
            wb[...] = stage[...].astype(BF16)

        @pl.when(nx_ref[b] >= 0)
        def _():
            for c in copies(nx_ref[b]):
                c.start()


def _expert_up_kernel(be_ref, bf_ref, nx_ref, nv_ref, x_ref, w1_hbm, w3_hbm, h_ref, w1f, w3f, w1b, w3b, sem):
    b = pl.program_id(0)
    _expert_weights(b, be_ref, bf_ref, nx_ref, (w1_hbm, w3_hbm), (w1f, w3f), (w1b, w3b), sem)

    @pl.when(b < nv_ref[0])
    def _():
        xl, xr = _unpack_bf16_pairs(x_ref[...])
        half = xl.shape[1]
        up = lambda wb: _dot(xl, wb[0:half, :]) + _dot(xr, wb[half:2 * half, :])
        h_ref[...] = (_silu(up(w1b)) * up(w3b)).astype(BF16)

    @pl.when(b >= nv_ref[0])
    def _():
        h_ref[...] = jnp.zeros(h_ref.shape, BF16)


def _expert_up(tabs, xs, w1, w3):
    rows, dp = xs.shape
    d, f = w1.shape[-2:]
    assert d == 2 * dp
    nb = rows // MOE_ROWS
    any_spec = pl.BlockSpec(memory_space=pl.ANY)
    grid_spec = pltpu.PrefetchScalarGridSpec(
        num_scalar_prefetch=4, grid=(nb,),
        in_specs=[pl.BlockSpec((MOE_ROWS, dp), lambda b, *_: (b, 0)), any_spec, any_spec],
        out_specs=pl.BlockSpec((MOE_ROWS, f), lambda b, *_: (b, 0)),
        scratch_shapes=[pltpu.VMEM((d, f), F32), pltpu.VMEM((d, f), F32),
                        pltpu.VMEM((d, f), BF16), pltpu.VMEM((d, f), BF16), pltpu.SemaphoreType.DMA((2,))])
    return pl.pallas_call(
        _expert_up_kernel,
        grid_spec=grid_spec,
        out_shape=jax.ShapeDtypeStruct((rows, f), BF16),
        compiler_params=_cparams(("arbitrary",), 2 * d * f * 6 + 4 * MOE_ROWS * d * 4 + 8 * 2**20),
        name="moe_expert_up",
    )(*tabs, xs, w1, w3)


def _expert_down_kernel(be_ref, bf_ref, nx_ref, nv_ref, h_ref, w2_hbm, y_ref, w2f, w2b, sem):
    b = pl.program_id(0)
    _expert_weights(b, be_ref, bf_ref, nx_ref, (w2_hbm,), (w2f,), (w2b,), sem)

    @pl.when(b < nv_ref[0])
    def _():
        y_ref[...] = _dot(h_ref[...], w2b[...])

    @pl.when(b >= nv_ref[0])
    def _():
        y_ref[...] = jnp.zeros(y_ref.shape, F32)


def _expert_down(tabs, hs, w2):
    rows, f = hs.shape
    d = w2.shape[-1]
    nb = rows // MOE_ROWS
    grid_spec = pltpu.PrefetchScalarGridSpec(
        num_scalar_prefetch=4, grid=(nb,),
        in_specs=[pl.BlockSpec((MOE_ROWS, f), lambda b, *_: (b, 0)), pl.BlockSpec(memory_space=pl.ANY)],
        out_specs=pl.BlockSpec((MOE_ROWS, d), lambda b, *_: (b, 0)),
        scratch_shapes=[pltpu.VMEM((f, d), F32), pltpu.VMEM((f, d), BF16), pltpu.SemaphoreType.DMA((1,))])
    return pl.pallas_call(
        _expert_down_kernel,
        grid_spec=grid_spec,
        out_shape=jax.ShapeDtypeStruct((rows, d), F32),
        compiler_params=_cparams(("arbitrary",), f * d * 6 + 6 * MOE_ROWS * d * 4 + 8 * 2**20),
        name="moe_expert_down",
    )(*tabs, hs, w2)


def _combine_kernel(dest_ref, x1_ref, g2_ref, gates_ref, y_ref, o_ref, ybuf, sem, *, tm, base):
    i = pl.program_id(0)
    slot = i % 2

    def gather(step, s):
        _row_copies(dest_ref, base + step * tm, tm,
                    lambda t, k, d: pltpu.make_async_copy(y_ref.at[pl.ds(d, 1)], ybuf.at[s, k, pl.ds(t, 1)],
                                                          sem.at[s]))

    @pl.when(i == 0)
    def _():
        gather(0, 0)

    @pl.when(i + 1 < pl.num_programs(0))
    def _():
        gather(i + 1, 1 - slot)

    for k in range(2):
        pltpu.make_async_copy(y_ref.at[pl.ds(0, tm)], ybuf.at[slot, k], sem.at[slot]).wait()
    gates = gates_ref[...]
    ff = gates[:, 0:1] * ybuf[slot, 0] + gates[:, 1:2] * ybuf[slot, 1]
    o_ref[...] = x1_ref[...] + g2_ref[...] * ff


def _combine(dest_flat, x1, g2_spec, g2, gates, y, tm, base):
    m, d = x1.shape
    grid_spec = pltpu.PrefetchScalarGridSpec(
        num_scalar_prefetch=1, grid=(m // tm,),
        in_specs=[pl.BlockSpec((tm, d), lambda i, dr: (i, 0)), g2_spec,
                  pl.BlockSpec((tm, LANES), lambda i, dr: (i, 0)),
                  pl.BlockSpec(memory_space=pl.ANY)],
        out_specs=pl.BlockSpec((tm, d), lambda i, dr: (i, 0)),
        scratch_shapes=[pltpu.VMEM((2, 2, tm, d), F32), pltpu.SemaphoreType.DMA((2,))])
    return pl.pallas_call(
        functools.partial(_combine_kernel, tm=tm, base=base),
        grid_spec=grid_spec,
        out_shape=jax.ShapeDtypeStruct((m, d), F32),
        compiler_params=_cparams(("arbitrary",), 8 * tm * d * 4 + 8 * 2**20),
        name="moe_combine",
    )(dest_flat, x1, g2, gates, y)


def kernel(x_prompt, x_sample, cache_k, cache_v, page_table, state_mlstm_C, state_mlstm_n, state_mlstm_m, state_conv,
           c_prompt, c_sample, w_ada, b_ada, norm1_g, norm2_g, w_in, conv_w, conv_b, b_igate, b_fgate, mlstm_out_g,
           q_norm_g, k_norm_g, lam_q1, lam_k1, lam_q2, lam_k2, diff_out_g, w_out, w_grp, b_grp, w_rt, b_rt, w1, w3, w2):
    B, T, D = x_prompt.shape
    DB, TS, _ = x_sample.shape
    depth = w_ada.shape[0]
    assert depth == 1, "single-layer trunk"
    n_pages, page = page_table.shape[1], cache_k.shape[2]
    past = n_pages * page
    W = N_HEADS * HEAD_W
    lam_init = 0.8 - 0.6 * math.exp(-0.3 * 0)
    NP, NS = B * T, DB * TS
    SP = SAMPLE_PAD

    c_all = jnp.concatenate([c_prompt, c_sample, jnp.zeros((16 - B - DB, D), F32)], axis=0)
    w_main, w_gate = _regroup_w_in(w_in[0].T, 4 * W, 2 * N_HEADS)
    gate_b = jnp.pad(jnp.concatenate([b_igate[0], b_fgate[0]]), (0, LANES - 2 * N_HEADS)).reshape(1, LANES)
    cw = jnp.pad(conv_w[0], ((0, SUBLANES - CONV_W), (0, 0)))
    cb = conv_b[0].reshape(1, 2 * W)
    og_m = mlstm_out_g[0].reshape(1, HEAD_W)
    wo = w_out[0].astype(BF16)
    w_route = jnp.concatenate([w_grp[0], w_rt[0].transpose(1, 0, 2).reshape(D, N_EXPERTS)], axis=1)
    w_route = jnp.pad(w_route, ((0, 0), (0, LANES - N_GROUPS - N_EXPERTS))).astype(BF16)
    b_route = jnp.pad(jnp.concatenate([b_grp[0], b_rt[0].reshape(-1)]), (0, LANES - N_GROUPS - N_EXPERTS)).reshape(1, LANES)
    lams = (lam_q1[0], lam_k1[0], lam_q2[0], lam_k2[0])

    mod = _ada(c_all, w_ada[0], b_ada[0])
    mod3 = mod.reshape(16, 1, 6 * D)
    mod_s = jnp.repeat(mod[B:B + DB], SP, axis=0)

    def p_mod(chunk, tm):
        return pl.BlockSpec((None, 1, D), lambda i, *_: ((i * tm) // T, 0, chunk))

    def s_mod(chunk, tm):
        return pl.BlockSpec((tm, D), lambda i, *_: (i, chunk))

    xp = x_prompt.reshape(NP, D)
    tm_in = 1024
    proj_p, gates_p = _inproj(xp, (p_mod(1, tm_in), p_mod(0, tm_in)), (mod3, mod3), norm1_g[0], w_main, w_gate, tm_in)
    cos_p, sin_p = _rope_tables(jnp.arange(T, dtype=I32))
    knew_p, kb_p, vnew_p, qt_p, vt_p = _qkrope(proj_p, 4, cos_p, sin_p, q_norm_g[0], k_norm_g[0], 256, B)
    ha_p = _dattn_prompt(qt_p, kb_p, vt_p, lams, diff_out_g[0], lam_init)
    zeros = lambda *s: jnp.zeros(s, F32)
    hm_p, C_p, n_p, m_p, cv_p = _mlstm(
        proj_p, gates_p, cw, cb, gate_b, og_m,
        zeros(B, N_HEADS, HEAD_W, HEAD_W), zeros(B, N_HEADS, HEAD_W), zeros(B, N_HEADS, LANES),
        zeros(B, SUBLANES, 2 * W), B, MLSTM_L, T // MLSTM_L, MLSTM_L)
    tm_o = 256
    x1_p, hn2_p, ids_p, gts_p = _outproj(
        hm_p, ha_p, xp, (p_mod(2, tm_o), p_mod(4, tm_o), p_mod(3, tm_o)), (mod3, mod3, mod3),
        norm2_g[0], wo, w_route, b_route, tm_o)

    MS = DB * SP
    xs_pad = jnp.pad(x_sample, ((0, 0), (0, SP - TS), (0, 0))).reshape(MS, D)
    proj_s, gates_s = _inproj(xs_pad, (s_mod(1, MS), s_mod(0, MS)), (mod_s, mod_s), norm1_g[0], w_main, w_gate, MS)
    pos_s = past + (jnp.arange(MS, dtype=I32) % SP)
    cos_s, sin_s = _rope_tables(pos_s)
    knew_s, kb_s, vnew_s, q_s, vb_s = _qkrope(proj_s, 4, cos_s, sin_s, q_norm_g[0], k_norm_g[0], MS)
    qr = q_s.reshape(DB, SP, N_HEADS, 2, DQK)[:, :TS].transpose(0, 2, 3, 1, 4)
    zq = jnp.zeros_like(qr[:, :, 0])
    qall = jnp.stack([jnp.concatenate([qr[:, :, 0], zq], axis=-1),
                      jnp.concatenate([zq, qr[:, :, 1]], axis=-1)], axis=2).reshape(DB, N_HEADS * 2 * TS, HEAD_W)
    new_rows = page * N_HEADS
    pad_new = lambda a: jnp.pad(a.reshape(DB, SP * N_HEADS, HEAD_W), ((0, 0), (0, new_rows - SP * N_HEADS), (0, 0)))
    ha_s = _dattn_paged(qall, cache_k[0].reshape(-1, new_rows, HEAD_W), cache_v[0].reshape(-1, new_rows, HEAD_W),
                        page_table, pad_new(kb_s), pad_new(vb_s), lams, diff_out_g[0], TS, lam_init)
    m0_s = jnp.broadcast_to(state_mlstm_m[0][:, :, None], (DB, N_HEADS, LANES))
    cv0_s = jnp.pad(state_conv[0], ((0, 0), (SUBLANES - (CONV_W - 1), 0), (0, 0)))
    hm_s, C_s, n_s, m_s, cv_s = _mlstm(proj_s, gates_s, cw, cb, gate_b, og_m, state_mlstm_C[0], state_mlstm_n[0],
                                       m0_s, cv0_s, DB, SP, 1, TS)
    x1_s, hn2_s, ids_s, gts_s = _outproj(
        hm_s, ha_s, xs_pad, (s_mod(2, MS), s_mod(4, MS), s_mod(3, MS)), (mod_s, mod_s, mod_s),
        norm2_g[0], wo, w_route, b_route, MS)
    real = lambda a: a.reshape(DB, SP, -1)[:, :TS].reshape(NS, -1)
    x1_s, hn2_s, ids_s, gts_s = real(x1_s), real(hn2_s), real(ids_s), real(gts_s)

    NT = NP + NS
    tm_r = 256
    n_rank = -(-NT // tm_r) * tm_r
    ids_all = jnp.concatenate([ids_p, ids_s, jnp.full((n_rank - NT, LANES), -1, I32)], axis=0)
    dest2, cnt = _ranks(ids_all, tm_r)
    dest = dest2[:NT, :2].reshape(-1)
    counts = cnt[0, :N_EXPERTS].astype(I32)
    pend = jnp.cumsum((counts + MOE_ROWS - 1) // MOE_ROWS)
    n_blocks = (2 * NT + N_EXPERTS * (MOE_ROWS - 1)) // MOE_ROWS
    n_valid = pend[-1:]
    blk = jnp.minimum(jnp.arange(n_blocks, dtype=I32), n_valid[0] - 1)
    blk_e = jnp.sum((blk[:, None] >= pend[None, :]).astype(I32), axis=1)
    blk_first = jnp.concatenate([jnp.ones((1,), I32), (blk_e[1:] != blk_e[:-1]).astype(I32)])
    first_pos = jnp.where(blk_first == 1, jnp.arange(n_blocks, dtype=I32), n_blocks)
    next_first = jnp.concatenate([lax.cummin(first_pos, reverse=True)[1:], jnp.full((1,), n_blocks, I32)])
    blk_next = jnp.where(next_first < n_blocks, blk_e[jnp.minimum(next_first, n_blocks - 1)], -1).astype(I32)
    tabs = (blk_e, blk_first, blk_next, n_valid)
    xs_sorted = _dispatch(dest, hn2_p, jnp.zeros((n_blocks * MOE_ROWS, D // 2), jnp.uint32), 256, 0)
    xs_sorted = _dispatch(dest, hn2_s, xs_sorted, NS, NP)
    h_sorted = _expert_up(tabs, xs_sorted, w1, w3)
    y_sorted = _expert_down(tabs, h_sorted, w2)
    tm_c = 128
    y_p = _combine(dest, x1_p, pl.BlockSpec((None, 1, D), lambda i, dr: ((i * tm_c) // T, 0, 5)), mod3, gts_p,
                   y_sorted, tm_c, 0)
    g2_s = jnp.repeat(mod[B:B + DB, 5 * D:], TS, axis=0)
    y_s = _combine(dest, x1_s, pl.BlockSpec((NS, D), lambda i, dr: (0, 0)), g2_s, gts_s, y_sorted, NS, NP)

    st = lambda a: a[None]
    k5 = lambda a, nb, t: a.reshape(1, nb, t, N_HEADS, HEAD_W)
    tok_s = lambda a: a.reshape(DB, SP, -1)[:, :TS]
    conv_out = lambda cv: cv[None, :, SUBLANES - (CONV_W - 1):, :]
    return (y_p.reshape(B, T, D), y_s.reshape(DB, TS, D),
            k5(knew_p, B, T), k5(vnew_p, B, T), st(C_p), st(n_p), st(m_p[:, :, 0]), conv_out(cv_p),
            k5(tok_s(knew_s), DB, TS), k5(tok_s(vnew_s), DB, TS), st(C_s), st(n_s), st(m_s[:, :, 0]), conv_out(cv_s))
```

```python
import functools
import math

import jax
import jax.numpy as jnp
from jax import lax
from jax.experimental import pallas as pl
from jax.experimental.pallas import tpu as pltpu

F32 = jnp.float32
BF16 = jnp.bfloat16
I32 = jnp.int32

EPS = 1e-6
ROPE_THETA = 10000.0
NEG = -1e30

LANES = 128
SUBLANES = 8
VMEM_LIMIT_CAP = 56 * 1024 * 1024

N_HEADS = 8
HEAD_W = 128
DQK = 64
N_GROUPS = 4
EXPERTS_PER_GROUP = 8
N_EXPERTS = N_GROUPS * EXPERTS_PER_GROUP
CONV_W = 4
MOE_ROWS = 256
SAMPLE_PAD = 8
MLSTM_L = 128
PAGES_PER_STEP = 16


def _cparams(sem, vmem_bytes):
    return pltpu.CompilerParams(dimension_semantics=sem,
                                vmem_limit_bytes=int(min(max(vmem_bytes, 16 * 2**20), VMEM_LIMIT_CAP)))


def _silu(x):
    return x * jax.nn.sigmoid(x)


def _dot(a, b):
    return jnp.dot(a, b, preferred_element_type=F32)


def _pack_bf16_pairs(x):
    n = x.shape[1] // 2
    bits = lax.bitcast_convert_type(x.astype(BF16).astype(F32), jnp.uint32)
    return bits[:, :n] | (bits[:, n:] >> 16)


def _unpack_bf16_pairs(w):
    left = lax.bitcast_convert_type(w & jnp.uint32(0xFFFF0000), F32)
    right = lax.bitcast_convert_type(w << 16, F32)
    return left.astype(BF16), right.astype(BF16)


def _dot_nt(a, b):
    return lax.dot_general(a, b, (((1,), (1,)), ((), ())), preferred_element_type=F32)


def _dot_tn(a, b):
    return lax.dot_general(a, b, (((0,), (0,)), ((), ())), preferred_element_type=F32)


def _ada_kernel(c_ref, w_ref, b_ref, o_ref):
    s = _silu(c_ref[...]).astype(BF16)
    o_ref[...] = _dot(s, w_ref[...].astype(BF16)) + b_ref[...]


def _ada(c_all, w_ada, b_ada):
    rows, d = c_all.shape
    n = w_ada.shape[1]
    tn = 1024
    return pl.pallas_call(
        _ada_kernel,
        grid=(n // tn,),
        in_specs=[pl.BlockSpec((rows, d), lambda j: (0, 0)),
                  pl.BlockSpec((d, tn), lambda j: (0, j)),
                  pl.BlockSpec((1, tn), lambda j: (0, j))],
        out_specs=pl.BlockSpec((rows, tn), lambda j: (0, j)),
        out_shape=jax.ShapeDtypeStruct((rows, n), F32),
        compiler_params=_cparams(("arbitrary",), 2 * d * tn * 4 + 3 * d * tn * 2 + 2**22),
        name="ada_mod",
    )(c_all, w_ada, b_ada.reshape(1, n))


def _regroup_kernel(w_hbm, main_ref, gate_ref, buf, gbuf, sem, *, lo, ng, tr):
    j = pl.program_id(0)
    slot = j % 2

    def block_copy(jj, s):
        start = pl.multiple_of(jnp.where(jj * tr < lo, jj * tr, jj * tr + ng), SUBLANES)
        return pltpu.make_async_copy(w_hbm.at[pl.ds(start, tr)], buf.at[s], sem.at[s])

    gate_copy = pltpu.make_async_copy(w_hbm.at[pl.ds(lo, ng)], gbuf, sem.at[2])

    @pl.when(j == 0)
    def _():
        block_copy(0, 0).start()
        gate_copy.start()

    @pl.when(j + 1 < pl.num_programs(0))
    def _():
        block_copy(j + 1, 1 - slot).start()

    block_copy(j, slot).wait()
    main_ref[...] = buf[slot].astype(BF16)

    @pl.when(j == 0)
    def _():
        gate_copy.wait()
        pad = jnp.zeros((LANES - ng, gbuf.shape[1]), F32)
        gate_ref[...] = jnp.concatenate([gbuf[...], pad], axis=0).astype(BF16)


def _regroup_w_in(w_t, lo, ng):
    n, d = w_t.shape
    tr = 1024
    assert lo % tr == 0 and (n - ng) % tr == 0 and ng % SUBLANES == 0
    return pl.pallas_call(
        functools.partial(_regroup_kernel, lo=lo, ng=ng, tr=tr),
        grid=((n - ng) // tr,),
        in_specs=[pl.BlockSpec(memory_space=pl.ANY)],
        out_specs=[pl.BlockSpec((tr, d), lambda j: (j, 0)), pl.BlockSpec((LANES, d), lambda j: (0, 0))],
        out_shape=[jax.ShapeDtypeStruct((n - ng, d), BF16), jax.ShapeDtypeStruct((LANES, d), BF16)],
        scratch_shapes=[pltpu.VMEM((2, tr, d), F32), pltpu.VMEM((ng, d), F32), pltpu.SemaphoreType.DMA((3,))],
        compiler_params=_cparams(("arbitrary",), 2 * tr * d * 4 + 4 * tr * d * 2 + 8 * 2**20),
        name="w_in_layout",
    )(w_t)


def _inproj_kernel(x_ref, sc_ref, sh_ref, g_ref, w_ref, wg_ref, o_ref, og_ref, hn_ref):
    @pl.when(pl.program_id(1) == 0)
    def _():
        rows = min(x_ref.shape[0], 256)

        def chunk(c, _):
            r = pl.ds(pl.multiple_of(c * rows, rows), rows)
            x = x_ref[r, :]
            y = x * lax.rsqrt(jnp.mean(x * x, axis=-1, keepdims=True) + EPS) * g_ref[...]
            sc = sc_ref[...] if sc_ref.shape[0] == 1 else sc_ref[r, :]
            sh = sh_ref[...] if sh_ref.shape[0] == 1 else sh_ref[r, :]
            hb = (y * (1.0 + sc) + sh).astype(BF16)
            hn_ref[r, :] = hb
            og_ref[r, :] = _dot_nt(hb, wg_ref[...])
            return 0
        lax.fori_loop(0, x_ref.shape[0] // rows, chunk, 0)

    o_ref[...] = _dot_nt(hn_ref[...], w_ref[...])


def _inproj(x, mod_specs, mods, g, w, wg, tm):
    m, d = x.shape
    n = w.shape[0]
    tn = 1024
    sc_spec, sh_spec = mod_specs
    return pl.pallas_call(
        _inproj_kernel,
        grid=(m // tm, n // tn),
        in_specs=[pl.BlockSpec((tm, d), lambda i, j: (i, 0)), sc_spec, sh_spec,
                  pl.BlockSpec((1, d), lambda i, j: (0, 0)),
                  pl.BlockSpec((tn, d), lambda i, j: (j, 0)),
                  pl.BlockSpec((LANES, d), lambda i, j: (0, 0))],
        out_specs=[pl.BlockSpec((tm, tn), lambda i, j: (i, j)),
                   pl.BlockSpec((tm, LANES), lambda i, j: (i, 0))],
        out_shape=[jax.ShapeDtypeStruct((m, n), F32), jax.ShapeDtypeStruct((m, LANES), F32)],
        scratch_shapes=[pltpu.VMEM((tm, d), BF16)],
        compiler_params=_cparams(("arbitrary", "arbitrary"),
                                 2 * tm * d * 4 + 2 * d * tn * 2 + 2 * tm * tn * 4 + tm * d * 2 + 12 * 2**20),
        name="norm1_inproj",
    )(x, mods[0], mods[1], g.reshape(1, d), w, wg)


def _qkrope_kernel(q_ref, k_ref, v_ref, cos_ref, sin_ref, qg_ref, kg_ref, ko_ref, kb_ref, vo_ref, qx_ref, vx_ref,
                   *, transposed):
    cos = cos_ref[...]
    sin = sin_ref[...]
    lane = lax.broadcasted_iota(I32, (1, LANES), 1)
    first_map = lane < DQK
    first_half = (lane % DQK) < (DQK // 2)

    def norm_rope(x, g):
        x2 = x * x
        s1 = jnp.sum(jnp.where(first_map, x2, 0.0), axis=-1, keepdims=True)
        s2 = jnp.sum(jnp.where(first_map, 0.0, x2), axis=-1, keepdims=True)
        inv = jnp.where(first_map, lax.rsqrt(s1 / DQK + EPS), lax.rsqrt(s2 / DQK + EPS))
        y = x * inv * g
        partner = jnp.where(first_half, pltpu.roll(y, LANES - DQK // 2, 1), pltpu.roll(y, DQK // 2, 1))
        return y * cos + partner * sin

    for h in range(N_HEADS):
        sl = slice(h * HEAD_W, (h + 1) * HEAD_W)
        qh = norm_rope(q_ref[:, sl], qg_ref[...]) * (DQK ** -0.5)
        kh = norm_rope(k_ref[:, sl], kg_ref[...])
        vh = v_ref[:, sl]
        ko_ref[:, sl] = kh
        kb_ref[:, sl] = kh.astype(BF16)
        vo_ref[:, sl] = vh
        if transposed:
            qx_ref[sl, :] = qh.T.astype(BF16)
            vx_ref[sl, :] = vh.T.astype(BF16)
        else:
            qx_ref[:, sl] = qh.astype(BF16)
            vx_ref[:, sl] = vh.astype(BF16)


def _qkrope(proj, col0, cos_t, sin_t, qg, kg, tm, transposed_batches=0):
    m = proj.shape[0]
    w = N_HEADS * HEAD_W
    nt = cos_t.shape[0] // tm
    blk = lambda c: pl.BlockSpec((tm, w), lambda i, c=c: (i, c))
    tab = pl.BlockSpec((tm, LANES), lambda i: (i % nt, 0))
    gsp = pl.BlockSpec((1, LANES), lambda i: (0, 0))
    osp = pl.BlockSpec((tm, w), lambda i: (i, 0))
    g2 = lambda g: jnp.tile(g, 2).reshape(1, LANES)
    out_specs = [osp] * 3
    out_shape = [jax.ShapeDtypeStruct((m, w), F32), jax.ShapeDtypeStruct((m, w), BF16),
                 jax.ShapeDtypeStruct((m, w), F32)]
    if transposed_batches:
        out_specs += [pl.BlockSpec((None, None, w, tm), lambda i: (i // nt, i % nt, 0, 0))] * 2
        out_shape += [jax.ShapeDtypeStruct((transposed_batches, nt, w, tm), BF16)] * 2
    else:
        out_specs += [osp] * 2
        out_shape += [jax.ShapeDtypeStruct((m, w), BF16)] * 2
    return pl.pallas_call(
        functools.partial(_qkrope_kernel, transposed=bool(transposed_batches)),
        grid=(m // tm,),
        in_specs=[blk(col0), blk(col0 + 1), blk(col0 + 2), tab, tab, gsp, gsp],
        out_specs=out_specs,
        out_shape=out_shape,
        compiler_params=_cparams(("arbitrary",), 2 * tm * w * (3 * 4 + 2 * 4 + 5 * 2) + 2**22),
        name="qknorm_rope",
    )(proj, proj, proj, cos_t, sin_t, g2(qg), g2(kg))


def _rope_tables(pos):
    half = DQK // 2
    inv = ROPE_THETA ** (-jnp.arange(half, dtype=F32) / half)
    ang = pos.astype(F32)[:, None] * inv[None, :]
    cos = jnp.tile(jnp.cos(ang), (1, LANES // half))
    sin = jnp.sin(ang)
    sin = jnp.tile(jnp.concatenate([-sin, sin], axis=1), (1, LANES // DQK))
    return cos, sin


def _lam_value(l1, l2, l3, l4, lam_init):
    a = jnp.sum(l1[...] * l2[...], axis=-1, keepdims=True)
    b = jnp.sum(l3[...] * l4[...], axis=-1, keepdims=True)
    return jnp.exp(a) - jnp.exp(b) + lam_init


def _dattn_kernel(qt_ref, k_ref, vt_ref, l1_ref, l2_ref, l3_ref, l4_ref, og_ref, o_ref,
                  s_a, s_b, p_a, p_b, cm_ref, stat_ref, acc_ref, *, tq, lam_init):
    qi = pl.program_id(2)
    tk = tq // 2
    s_bufs, p_bufs = (s_a, s_b), (p_a, p_b)
    qt = jnp.concatenate([qt_ref[0], qt_ref[1]], axis=1)
    feat = lax.broadcasted_iota(I32, (HEAD_W, 1), 0)
    zero = jnp.zeros_like(qt)
    qcat = jnp.concatenate([jnp.where(feat < DQK, qt, zero), jnp.where(feat < DQK, zero, qt)], axis=1)
    key = lax.broadcasted_iota(I32, (tk, 2 * tq), 0)
    qry = lax.broadcasted_iota(I32, (tk, 2 * tq), 1) % tq

    def set_scores(slot, s, mask):
        if mask is not None:
            s = jnp.where(mask, s, NEG)
        s_bufs[slot][...] = s
        cm_ref[slot:slot + 1, :] = jnp.max(s, axis=0, keepdims=True)

    def scores_into(j, slot, mask=None):
        off = pl.multiple_of(j * tk, tk)
        set_scores(slot, _dot(k_ref[pl.ds(off, tk), :], qcat), mask)

    def softmax_pv(j, slot):
        pv = _dot(vt_ref[jnp.maximum(j - 1, 0)], p_bufs[1 - slot][...])
        m = stat_ref[0:1, :]
        m_new = jnp.maximum(m, cm_ref[slot:slot + 1, :])
        alpha = jnp.exp(m - m_new)
        p = jnp.exp(s_bufs[slot][...] - m_new)
        stat_ref[0:1, :] = m_new
        stat_ref[1:2, :] = alpha * stat_ref[1:2, :] + jnp.sum(p, axis=0, keepdims=True)
        p_bufs[slot][...] = p.astype(BF16)
        acc_ref[...] = alpha * (acc_ref[...] + pv)

    stat_ref[0:1, :] = jnp.full((1, 2 * tq), NEG, F32)
    stat_ref[1:2, :] = jnp.zeros((1, 2 * tq), F32)
    acc_ref[...] = jnp.zeros(acc_ref.shape, F32)
    p_b[...] = jnp.zeros(p_b.shape, BF16)
    scores_into(0, 0)

    def body(t, _):
        scores_into(2 * t + 1, 1)
        softmax_pv(2 * t, 0)
        scores_into(2 * t + 2, 0)
        softmax_pv(2 * t + 1, 1)
        return 0

    lax.fori_loop(0, qi, body, 0)
    set_scores(0, s_a[...], key <= qry)
    scores_into(2 * qi + 1, 1, key + tk <= qry)
    softmax_pv(2 * qi, 0)
    softmax_pv(2 * qi + 1, 1)
    acc = (acc_ref[...] + _dot(vt_ref[2 * qi + 1], p_b[...])) / stat_ref[1:2, :]
    lam = _lam_value(l1_ref, l2_ref, l3_ref, l4_ref, lam_init)
    o = (acc[:, :tq] - lam * acc[:, tq:]).T
    o = o * lax.rsqrt(jnp.mean(o * o, axis=-1, keepdims=True) + EPS) * og_ref[...]
    o_ref[...] = (o * (1.0 - lam_init)).astype(BF16)


def _dattn_prompt(qt, k, vt, lams, og, lam_init):
    nb, nk, _, tk = qt.shape
    t = nk * tk
    tq = 2 * tk
    nq = t // tq
    lsp = pl.BlockSpec((1, DQK), lambda b, h, i: (0, 0))
    return pl.pallas_call(
        functools.partial(_dattn_kernel, tq=tq, lam_init=lam_init),
        grid=(nb, N_HEADS, nq),
        in_specs=[pl.BlockSpec((None, 2, HEAD_W, tk), lambda b, h, i: (b, i, h, 0)),
                  pl.BlockSpec((t, HEAD_W), lambda b, h, i: (b, h)),
                  pl.BlockSpec((None, nk, HEAD_W, tk), lambda b, h, i: (b, 0, h, 0)),
                  lsp, lsp, lsp, lsp,
                  pl.BlockSpec((1, HEAD_W), lambda b, h, i: (0, 0))],
        out_specs=pl.BlockSpec((tq, HEAD_W), lambda b, h, i: (b * nq + i, h)),
        out_shape=jax.ShapeDtypeStruct(k.shape, BF16),
        scratch_shapes=[pltpu.VMEM((tk, 2 * tq), F32), pltpu.VMEM((tk, 2 * tq), F32),
                        pltpu.VMEM((tk, 2 * tq), BF16), pltpu.VMEM((tk, 2 * tq), BF16),
                        pltpu.VMEM((SUBLANES, 2 * tq), F32), pltpu.VMEM((SUBLANES, 2 * tq), F32),
                        pltpu.VMEM((HEAD_W, 2 * tq), F32)],
        compiler_params=_cparams(("arbitrary",) * 3, 32 * 2**20),
        name="diff_attn_prompt",
    )(qt, k, vt, *[x.reshape(1, DQK) for x in lams], og.reshape(1, HEAD_W))


def _dattn_paged_kernel(pt_ref, q_ref, *refs, pps, n_new, n_pages, lam_init):
    k_refs = refs[:pps]
    v_refs = refs[pps:2 * pps]
    kn_ref, vn_ref, l1_ref, l2_ref, l3_ref, l4_ref, og_ref, o_ref, s_scr, m_sc, l_sc, acc_sc = refs[2 * pps:]
    phase = pl.program_id(1)
    step_id = pl.program_id(2)
    nsteps = pl.num_programs(2)
    prow = s_scr.shape[2]
    nchunk = prow // LANES
    lane = lax.broadcasted_iota(I32, (SUBLANES, LANES), 1)
    lane_head = lane % N_HEADS
    page_head = lax.broadcasted_iota(I32, (SUBLANES, prow), 1) % N_HEADS
    row = lax.broadcasted_iota(I32, (SUBLANES, prow), 0)
    tile = lambda x: jnp.concatenate([x] * nchunk, axis=1)

    def lane_class_reduce(x, op):
        for sh in (8, 16, 32, 64):
            x = op(x, pltpu.roll(x, sh, 1))
        return x

    def score_pages(ks, first, valid):
        q = q_ref[...]
        parts = []
        for pi, kp in enumerate(ks):
            sf = _dot_nt(q, kp.astype(BF16))
            page = []
            for c in range(nchunk):
                blk = sf[:, c * LANES:(c + 1) * LANES]
                s = jnp.zeros((SUBLANES, LANES), F32)
                for h in range(N_HEADS):
                    s = jnp.where(lane_head == h, blk[h * SUBLANES:(h + 1) * SUBLANES, :], s)
                page.append(s)
            s_page = jnp.concatenate(page, axis=1)
            if valid is not None:
                s_page = jnp.where(valid, s_page, NEG)
                page = [s_page[:, c * LANES:(c + 1) * LANES] for c in range(nchunk)]
            s_scr[first + pi] = s_page
            parts += page
        m_old = m_sc[...]
        m_new = jnp.maximum(m_old, functools.reduce(jnp.maximum, parts))
        l_sc[...] = jnp.exp(m_old - m_new) * l_sc[...] + functools.reduce(jnp.add, [jnp.exp(s - m_new) for s in parts])
        m_sc[...] = m_new

    def value_pages(vs, first):
        lam = _lam_value(l1_ref, l2_ref, l3_ref, l4_ref, lam_init)
        m_full = tile(m_sc[...])
        l_full = tile(l_sc[...])
        pv = jnp.zeros(acc_sc.shape, F32)
        for pi, vp in enumerate(vs):
            e = jnp.exp(s_scr[first + pi] - m_full) / l_full
            a = jnp.where(row < n_new, e - lam * pltpu.roll(e, SUBLANES - n_new, 0), 0.0)
            pexp = jnp.concatenate([jnp.where(page_head == h, a, 0.0) for h in range(N_HEADS)], axis=0)
            pv = pv + _dot(pexp.astype(BF16), vp.astype(BF16))
        acc_sc[...] = acc_sc[...] + pv

    @pl.when(phase == 0)
    def _():
        @pl.when(step_id == 0)
        def _():
            m_sc[...] = jnp.full(m_sc.shape, NEG, F32)
            l_sc[...] = jnp.zeros(l_sc.shape, F32)

        score_pages([r[...] for r in k_refs], step_id * pps, None)

        @pl.when(step_id == nsteps - 1)
        def _():
            tok = lax.broadcasted_iota(I32, (SUBLANES, prow), 1) // N_HEADS
            score_pages([kn_ref[...]], n_pages, tok <= row % n_new)
            m_lane = m_sc[...]
            m_head = lane_class_reduce(m_lane, jnp.maximum)
            l_sc[...] = lane_class_reduce(l_sc[...] * jnp.exp(m_lane - m_head), jnp.add)
            m_sc[...] = m_head

    @pl.when(phase == 1)
    def _():
        @pl.when(step_id == 0)
        def _():
            acc_sc[...] = jnp.zeros(acc_sc.shape, F32)

        value_pages([r[...] for r in v_refs], step_id * pps)

        @pl.when(step_id == nsteps - 1)
        def _():
            value_pages([vn_ref[...]], n_pages)
            acc = acc_sc[...]
            outs = []
            for h in range(N_HEADS):
                o = acc[h * SUBLANES:(h + 1) * SUBLANES]
                o = o * lax.rsqrt(jnp.mean(o * o, axis=-1, keepdims=True) + EPS) * og_ref[...]
                outs.append(o * (1.0 - lam_init))
            o_ref[...] = jnp.concatenate(outs, axis=1).astype(BF16)


def _dattn_paged(qall, cache_k, cache_v, page_table, k_new, v_new, lams, og, n_new, lam_init):
    nb, n_pages = page_table.shape
    pps = PAGES_PER_STEP
    prow = cache_k.shape[1]
    nsteps = n_pages // pps
    kspec = lambda j: pl.BlockSpec(
        (None, prow, HEAD_W),
        lambda b, ph, s, pt, j=j: (pt[b * n_pages + (s * (1 - ph) + (nsteps - 1) * ph) * pps + j], 0, 0))
    vspec = lambda j: pl.BlockSpec(
        (None, prow, HEAD_W), lambda b, ph, s, pt, j=j: (pt[b * n_pages + s * ph * pps + j], 0, 0))
    lsp = pl.BlockSpec((1, DQK), lambda b, ph, s, pt: (0, 0))
    nspec = pl.BlockSpec((None, k_new.shape[1], HEAD_W), lambda b, ph, s, pt: (b, 0, 0))
    grid_spec = pltpu.PrefetchScalarGridSpec(
        num_scalar_prefetch=1,
        grid=(nb, 2, nsteps),
        in_specs=[pl.BlockSpec((None, N_HEADS * SUBLANES, HEAD_W), lambda b, ph, s, pt: (b, 0, 0))]
                 + [kspec(j) for j in range(pps)] + [vspec(j) for j in range(pps)]
                 + [nspec, nspec, lsp, lsp, lsp, lsp, pl.BlockSpec((1, HEAD_W), lambda b, ph, s, pt: (0, 0))],
        out_specs=pl.BlockSpec((SUBLANES, N_HEADS * HEAD_W), lambda b, ph, s, pt: (b, 0)),
        scratch_shapes=[pltpu.VMEM((n_pages + 1, SUBLANES, prow), F32),
                        pltpu.VMEM((SUBLANES, LANES), F32), pltpu.VMEM((SUBLANES, LANES), F32),
                        pltpu.VMEM((N_HEADS * SUBLANES, HEAD_W), F32)])
    return pl.pallas_call(
        functools.partial(_dattn_paged_kernel, pps=pps, n_new=n_new, n_pages=n_pages, lam_init=lam_init),
        grid_spec=grid_spec,
        out_shape=jax.ShapeDtypeStruct((nb * SUBLANES, N_HEADS * HEAD_W), BF16),
        compiler_params=_cparams(("arbitrary",) * 3,
                                 4 * pps * prow * HEAD_W * 4 + (n_pages + 1) * SUBLANES * prow * 4 + 24 * 2**20),
        name="diff_attn_paged",
    )(page_table.reshape(-1), qall, *([cache_k] * pps), *([cache_v] * pps), k_new, v_new,
      *[x.reshape(1, DQK) for x in lams], og.reshape(1, HEAD_W))


def _split3(a):
    hi = a.astype(BF16)
    r1 = a - hi.astype(F32)
    mid = r1.astype(BF16)
    lo = (r1 - mid.astype(F32)).astype(BF16)
    return hi, mid, lo


def _mlstm_kernel(qk_ref, v_ref, o_ref, gt_ref, cw_ref, cb_ref, gb_ref, og_ref,
                  C0_ref, n0_ref, m0_ref, cv0_ref,
                  h_ref, C_ref, n_ref, m_ref, cv_ref, xbuf, *, L, rows_in, n_valid):
    c = pl.program_id(1)
    width = N_HEADS * HEAD_W

    @pl.when(c == 0)
    def _():
        C_ref[...] = C0_ref[...]
        n_ref[...] = n0_ref[...]
        m_ref[...] = m0_ref[...]
        xbuf[0:SUBLANES, :] = cv0_ref[...]

    def padded(ref):
        x = ref[...]
        if rows_in < L:
            x = jnp.concatenate([x, jnp.zeros((L - rows_in, x.shape[1]), x.dtype)], axis=0)
        return x

    xbuf[SUBLANES:SUBLANES + L, :] = padded(qk_ref)
    taps = [xbuf[SUBLANES - (CONV_W - 1) + j:SUBLANES - (CONV_W - 1) + j + L, :] * cw_ref[j:j + 1, :]
            for j in range(CONV_W)]
    conv = cb_ref[...] + functools.reduce(jnp.add, taps)
    cv_ref[...] = xbuf[n_valid:n_valid + SUBLANES, :]
    xbuf[0:SUBLANES, :] = xbuf[L:L + SUBLANES, :]
    conv = _silu(conv)
    q_all = conv[:, :width].astype(BF16)
    k_all = conv[:, width:] * (HEAD_W ** -0.5)
    v_all = padded(v_ref).astype(BF16)
    o_all = padded(o_ref)

    g = padded(gt_ref) + gb_ref[...]
    lane = lax.broadcasted_iota(I32, (L, LANES), 1)
    rowi = lax.broadcasted_iota(I32, (L, LANES), 0)
    is_f = (lane >= N_HEADS) & (lane < 2 * N_HEADS)
    logsig = jnp.minimum(g, 0.0) - jnp.log1p(jnp.exp(-jnp.abs(g)))
    a = jnp.where(is_f, logsig, g)
    if n_valid < L:
        a = jnp.where(rowi < n_valid, a, jnp.where(is_f, 0.0, NEG))
    tri = (lax.broadcasted_iota(I32, (L, L), 1) <= lax.broadcasted_iota(I32, (L, L), 0))
    tri_b = tri.astype(BF16)
    cum = functools.reduce(jnp.add, [_dot(tri_b, part) for part in _split3(jnp.where(is_f, a, 0.0))])
    a_t = a.T
    cum_t = cum.T

    m_all = m_ref[...]
    n_all = n_ref[...]
    m_rows, n_rows = [], []
    for h in range(N_HEADS):
        sl = slice(h * HEAD_W, (h + 1) * HEAD_W)
        b_col = cum[:, N_HEADS + h:N_HEADS + h + 1]
        b_row = cum_t[N_HEADS + h:N_HEADS + h + 1, :]
        i_col = a[:, h:h + 1]
        i_row = a_t[h:h + 1, :]
        m_prev = m_all[h:h + 1, 0:1]
        d = jnp.where(tri, (b_col - b_row) + i_row, NEG)
        inter = b_col + m_prev
        mt = jnp.maximum(inter, jnp.max(d, axis=-1, keepdims=True))
        w_intra = jnp.exp(d - mt)
        w_inter = jnp.exp(inter - mt)
        qh, kh, vh = q_all[:, sl], k_all[:, sl], v_all[:, sl]
        C_h = C_ref[h]
        n_h = n_all[h:h + 1, :]
        s = _dot_nt(qh, kh.astype(BF16)) * w_intra
        num = _dot(s.astype(BF16), vh) + w_inter * _dot(qh, C_h.astype(BF16))
        den = (jnp.sum(s, axis=-1, keepdims=True)
               + w_inter * jnp.sum(qh.astype(F32) * n_h.astype(BF16).astype(F32), axis=-1, keepdims=True))
        hh = num / jnp.maximum(jnp.abs(den), jnp.exp(-mt))
        m_last = mt[L - 1:L, :]
        b_last = b_col[L - 1:L, :]
        gk = jnp.exp((b_last - b_col) + i_col - m_last) * kh
        decay = jnp.exp(b_last + m_prev - m_last)
        C_ref[h] = decay * C_h + _dot_tn(gk.astype(BF16), vh)
        n_rows.append(decay * n_h + jnp.sum(gk, axis=0, keepdims=True))
        m_rows.append(jnp.broadcast_to(m_last, (1, LANES)))
        hn = hh * lax.rsqrt(jnp.mean(hh * hh, axis=-1, keepdims=True) + EPS) * og_ref[...]
        out = hn * jax.nn.sigmoid(o_all[:, sl])
        h_ref[:, sl] = out[:rows_in].astype(BF16)
    n_ref[...] = jnp.concatenate(n_rows, axis=0)
    m_ref[...] = jnp.concatenate(m_rows, axis=0)


def _mlstm(proj, gates, conv_w, conv_b, gate_b, og, C0, n0, m0, cv0, nb, rows_in, nchunks, n_valid):
    L = MLSTM_L
    w = N_HEADS * HEAD_W
    row = lambda cb: (lambda b, c: (b * nchunks + c, cb))
    st3 = lambda b, c: (b, 0, 0)
    kern = functools.partial(_mlstm_kernel, L=L, rows_in=rows_in, n_valid=n_valid)
    return pl.pallas_call(
        kern,
        grid=(nb, nchunks),
        in_specs=[pl.BlockSpec((rows_in, 2 * w), lambda b, c: (b * nchunks + c, 0)),
                  pl.BlockSpec((rows_in, w), row(2)), pl.BlockSpec((rows_in, w), row(3)),
                  pl.BlockSpec((rows_in, LANES), lambda b, c: (b * nchunks + c, 0)),
                  pl.BlockSpec((SUBLANES, 2 * w), lambda b, c: (0, 0)),
                  pl.BlockSpec((1, 2 * w), lambda b, c: (0, 0)),
                  pl.BlockSpec((1, LANES), lambda b, c: (0, 0)),
                  pl.BlockSpec((1, HEAD_W), lambda b, c: (0, 0)),
                  pl.BlockSpec((None, N_HEADS, HEAD_W, HEAD_W), lambda b, c: (b, 0, 0, 0)),
                  pl.BlockSpec((None, N_HEADS, HEAD_W), st3),
                  pl.BlockSpec((None, N_HEADS, LANES), st3),
                  pl.BlockSpec((None, SUBLANES, 2 * w), st3)],
        out_specs=[pl.BlockSpec((rows_in, w), lambda b, c: (b * nchunks + c, 0)),
                   pl.BlockSpec((None, N_HEADS, HEAD_W, HEAD_W), lambda b, c: (b, 0, 0, 0)),
                   pl.BlockSpec((None, N_HEADS, HEAD_W), st3),
                   pl.BlockSpec((None, N_HEADS, LANES), st3),
                   pl.BlockSpec((None, SUBLANES, 2 * w), st3)],
        out_shape=[jax.ShapeDtypeStruct((nb * nchunks * rows_in, w), BF16),
                   jax.ShapeDtypeStruct((nb, N_HEADS, HEAD_W, HEAD_W), F32),
                   jax.ShapeDtypeStruct((nb, N_HEADS, HEAD_W), F32),
                   jax.ShapeDtypeStruct((nb, N_HEADS, LANES), F32),
                   jax.ShapeDtypeStruct((nb, SUBLANES, 2 * w), F32)],
        scratch_shapes=[pltpu.VMEM((SUBLANES + L, 2 * w), F32)],
        compiler_params=_cparams(("arbitrary", "arbitrary"), 40 * 2**20),
        name="mlstm_chunkwise",
    )(proj, proj, proj, gates, conv_w, conv_b, gate_b, og, C0, n0, m0, cv0)


def _outproj_kernel(hm_ref, ha_ref, x_ref, g1_ref, sc_ref, sh_ref, ng_ref, wo_ref, wr_ref, br_ref,
                    x1_ref, hn_ref, ids_ref, gates_ref):
    half = hm_ref.shape[1]
    mix = _dot(hm_ref[...], wo_ref[0:half, :]) + _dot(ha_ref[...], wo_ref[half:2 * half, :])
    x1 = x_ref[...] + g1_ref[...] * mix
    x1_ref[...] = x1
    y = x1 * lax.rsqrt(jnp.mean(x1 * x1, axis=-1, keepdims=True) + EPS) * ng_ref[...]
    hn = y * (1.0 + sc_ref[...]) + sh_ref[...]
    hn_ref[...] = _pack_bf16_pairs(hn)
    logits = _dot(hn.astype(BF16), wr_ref[...]) + br_ref[...]

    lane = lax.broadcasted_iota(I32, logits.shape, 1).astype(F32)
    big = 1000.0
    is_g = lane < N_GROUPS
    gl = jnp.where(is_g, logits, NEG)
    gmax = jnp.max(gl, axis=-1, keepdims=True)
    g_idx = jnp.min(jnp.where(gl == gmax, lane, big), axis=-1, keepdims=True)
    g_p = 1.0 / jnp.sum(jnp.where(is_g, jnp.exp(gl - gmax), 0.0), axis=-1, keepdims=True)
    e_lo = N_GROUPS + EXPERTS_PER_GROUP * g_idx
    in_grp = (lane >= e_lo) & (lane < e_lo + EXPERTS_PER_GROUP)
    el = jnp.where(in_grp, logits, NEG)
    ex = jnp.where(in_grp, jnp.exp(el - jnp.max(el, axis=-1, keepdims=True)), 0.0)
    p = jnp.where(in_grp, ex / jnp.sum(ex, axis=-1, keepdims=True), -1.0)
    top1 = jnp.max(p, axis=-1, keepdims=True)
    idx1 = jnp.min(jnp.where(p == top1, lane, big), axis=-1, keepdims=True)
    p2 = jnp.where(lane == idx1, -1.0, p)
    top2 = jnp.max(p2, axis=-1, keepdims=True)
    idx2 = jnp.min(jnp.where(p2 == top2, lane, big), axis=-1, keepdims=True)
    tsum = top1 + top2
    ids_ref[...] = jnp.where(lane == 0.0, idx1 - N_GROUPS,
                             jnp.where(lane == 1.0, idx2 - N_GROUPS, -1.0)).astype(I32)
    gates_ref[...] = jnp.where(lane == 0.0, g_p * top1 / tsum, jnp.where(lane == 1.0, g_p * top2 / tsum, 0.0))


def _outproj(hm, ha, x, mod_specs, mods, ng, wo, wr, br, tm):
    m, d = x.shape
    half = hm.shape[1]
    row = lambda wdt: pl.BlockSpec((tm, wdt), lambda i: (i, 0))
    const = lambda shp: pl.BlockSpec(shp, lambda i: (0, 0))
    return pl.pallas_call(
        _outproj_kernel,
        grid=(m // tm,),
        in_specs=[row(half), row(half), row(d), *mod_specs, const((1, d)), const((d, d)),
                  const((d, LANES)), const((1, LANES))],
        out_specs=[row(d), row(d // 2), row(LANES), row(LANES)],
        out_shape=[jax.ShapeDtypeStruct((m, d), F32), jax.ShapeDtypeStruct((m, d // 2), jnp.uint32),
                   jax.ShapeDtypeStruct((m, LANES), I32), jax.ShapeDtypeStruct((m, LANES), F32)],
        compiler_params=_cparams(("arbitrary",), 2 * d * d * 2 + 8 * tm * d * 4 + 8 * 2**20),
        name="outproj_norm2_route",
    )(hm, ha, x, *mods, ng.reshape(1, d), wo, wr, br)


def _rank_kernel(ids_ref, dest_ref, cnt_ref, carry, pstart):
    pss = pl.program_id(0)
    i = pl.program_id(1)
    ids = ids_ref[...]
    tm = ids.shape[0]
    lane = lax.broadcasted_iota(I32, ids.shape, 1)
    o0 = lane == ids[:, 0:1]
    o1 = lane == ids[:, 1:2]
    onehot = jnp.where(o0 | o1, 1.0, 0.0)
    col_counts = jnp.sum(onehot, axis=0, keepdims=True)

    @pl.when((pss == 0) & (i == 0))
    def _():
        carry[...] = jnp.zeros(carry.shape, F32)

    @pl.when(pss == 0)
    def _():
        carry[...] = carry[...] + col_counts

    @pl.when((pss == 1) & (i == 0))
    def _():
        cnt = carry[...]
        cnt_ref[...] = cnt
        blocks = jnp.floor((cnt + (MOE_ROWS - 1)) * (1.0 / MOE_ROWS))
        earlier = lax.broadcasted_iota(I32, (LANES, LANES), 0) < lax.broadcasted_iota(I32, (LANES, LANES), 1)
        pstart[...] = _dot(blocks.astype(BF16), earlier.astype(BF16)) * MOE_ROWS
        carry[...] = jnp.zeros(carry.shape, F32)

    @pl.when(pss == 1)
    def _():
        strict = lax.broadcasted_iota(I32, (tm, tm), 1) < lax.broadcasted_iota(I32, (tm, tm), 0)
        before = _dot(strict.astype(BF16), onehot.astype(BF16)) + carry[0:1, :] + pstart[0:1, :]
        r0 = jnp.sum(jnp.where(o0, before, 0.0), axis=-1, keepdims=True)
        r1 = jnp.sum(jnp.where(o1, before, 0.0), axis=-1, keepdims=True)
        dest_ref[...] = jnp.where(lane == 0, r0, jnp.where(lane == 1, r1, 0.0)).astype(I32)
        carry[...] = carry[...] + col_counts


def _ranks(ids, tm):
    m = ids.shape[0]
    return pl.pallas_call(
        _rank_kernel,
        grid=(2, m // tm),
        in_specs=[pl.BlockSpec((tm, LANES), lambda p, i: (i, 0))],
        out_specs=[pl.BlockSpec((tm, LANES), lambda p, i: (i * p, 0)),
                   pl.BlockSpec((SUBLANES, LANES), lambda p, i: (0, 0))],
        out_shape=[jax.ShapeDtypeStruct((m, LANES), I32), jax.ShapeDtypeStruct((SUBLANES, LANES), F32)],
        scratch_shapes=[pltpu.VMEM((SUBLANES, LANES), F32), pltpu.VMEM((SUBLANES, LANES), F32)],
        compiler_params=_cparams(("arbitrary", "arbitrary"), 16 * 2**20),
        name="moe_ranks",
    )(ids)


def _row_copies(idx_ref, t0, tm, copy):
    def body(t, _):
        for k in range(2):
            copy(t, k, idx_ref[2 * (t0 + t) + k]).start()
        return 0
    lax.fori_loop(0, tm, body, 0, unroll=8)


def _dispatch_kernel(dest_ref, h_ref, xin_ref, xs_ref, sem, *, tm, base):
    del xin_ref
    t0 = base + pl.program_id(0) * tm
    _row_copies(dest_ref, t0, tm,
                lambda t, k, d: pltpu.make_async_copy(h_ref.at[pl.ds(t, 1)], xs_ref.at[pl.ds(d, 1)], sem))
    for _ in range(2):
        pltpu.make_async_copy(h_ref, xs_ref.at[pl.ds(0, tm)], sem).wait()


def _dispatch(dest_flat, hn, xs, tm, base):
    m, d = hn.shape
    any_spec = pl.BlockSpec(memory_space=pl.ANY)
    grid_spec = pltpu.PrefetchScalarGridSpec(
        num_scalar_prefetch=1, grid=(m // tm,),
        in_specs=[pl.BlockSpec((tm, d), lambda i, dr: (i, 0)), any_spec], out_specs=any_spec,
        scratch_shapes=[pltpu.SemaphoreType.DMA])
    return pl.pallas_call(
        functools.partial(_dispatch_kernel, tm=tm, base=base),
        grid_spec=grid_spec,
        out_shape=jax.ShapeDtypeStruct(xs.shape, xs.dtype),
        input_output_aliases={2: 0},
        compiler_params=_cparams(("arbitrary",), 4 * tm * d * 4 + 8 * 2**20),
        name="moe_dispatch",
    )(dest_flat, hn, xs)


def _expert_weights(b, be_ref, bf_ref, nx_ref, w_hbm, w_stage, w_bf16, sem):
    def copies(e):
        return [pltpu.make_async_copy(src.at[0, e], dst, sem.at[i])
                for i, (src, dst) in enumerate(zip(w_hbm, w_stage))]

    @pl.when(b == 0)
    def _():
        for c in copies(be_ref[0]):
            c.start()

    @pl.when(bf_ref[b] == 1)
    def _():
        for c in copies(be_ref[b]):
            c.wait()
        for stage, wb in zip(w_stage, w_bf16):
            wb[...] = stage[...].astype(BF16)

        @pl.when(nx_ref[b] >= 0)
        def _():
            for c in copies(nx_ref[b]):
                c.start()


def _block_rows(br_ref, b, out_ref, compute):
    nrows = br_ref[b]
    half = out_ref.shape[0] // 2
    top, bottom = pl.ds(0, half), pl.ds(half, half)

    @pl.when(nrows > half)
    def _():
        out_ref[...] = compute(pl.ds(0, 2 * half))

    @pl.when(jnp.logical_and(nrows > 0, nrows <= half))
    def _():
        out_ref[top, :] = compute(top)
        out_ref[bottom, :] = jnp.zeros((half, out_ref.shape[1]), out_ref.dtype)

    @pl.when(nrows == 0)
    def _():
        out_ref[...] = jnp.zeros(out_ref.shape, out_ref.dtype)


def _expert_up_kernel(be_ref, bf_ref, nx_ref, br_ref, x_ref, w1_hbm, w3_hbm, h_ref, w1f, w3f, w1b, w3b, sem):
    b = pl.program_id(0)
    _expert_weights(b, be_ref, bf_ref, nx_ref, (w1_hbm, w3_hbm), (w1f, w3f), (w1b, w3b), sem)

    def swiglu(rows):
        xl, xr = _unpack_bf16_pairs(x_ref[rows, :])
        half = xl.shape[1]
        up = lambda wb: _dot(xl, wb[0:half, :]) + _dot(xr, wb[half:2 * half, :])
        return (_silu(up(w1b)) * up(w3b)).astype(BF16)

    _block_rows(br_ref, b, h_ref, swiglu)


def _expert_up(tabs, xs, w1, w3):
    rows, dp = xs.shape
    d, f = w1.shape[-2:]
    assert d == 2 * dp
    nb = rows // MOE_ROWS
    any_spec = pl.BlockSpec(memory_space=pl.ANY)
    grid_spec = pltpu.PrefetchScalarGridSpec(
        num_scalar_prefetch=4, grid=(nb,),
        in_specs=[pl.BlockSpec((MOE_ROWS, dp), lambda b, *_: (b, 0)), any_spec, any_spec],
        out_specs=pl.BlockSpec((MOE_ROWS, f), lambda b, *_: (b, 0)),
        scratch_shapes=[pltpu.VMEM((d, f), F32), pltpu.VMEM((d, f), F32),
                        pltpu.VMEM((d, f), BF16), pltpu.VMEM((d, f), BF16), pltpu.SemaphoreType.DMA((2,))])
    return pl.pallas_call(
        _expert_up_kernel,
        grid_spec=grid_spec,
        out_shape=jax.ShapeDtypeStruct((rows, f), BF16),
        compiler_params=_cparams(("arbitrary",), 2 * d * f * 6 + 4 * MOE_ROWS * d * 4 + 8 * 2**20),
        name="moe_expert_up",
    )(*tabs, xs, w1, w3)


def _expert_down_kernel(be_ref, bf_ref, nx_ref, br_ref, h_ref, w2_hbm, y_ref, w2f, w2b, sem):
    b = pl.program_id(0)
    _expert_weights(b, be_ref, bf_ref, nx_ref, (w2_hbm,), (w2f,), (w2b,), sem)
    _block_rows(br_ref, b, y_ref, lambda rows: _dot(h_ref[rows, :], w2b[...]))


def _expert_down(tabs, hs, w2):
    rows, f = hs.shape
    d = w2.shape[-1]
    nb = rows // MOE_ROWS
    grid_spec = pltpu.PrefetchScalarGridSpec(
        num_scalar_prefetch=4, grid=(nb,),
        in_specs=[pl.BlockSpec((MOE_ROWS, f), lambda b, *_: (b, 0)), pl.BlockSpec(memory_space=pl.ANY)],
        out_specs=pl.BlockSpec((MOE_ROWS, d), lambda b, *_: (b, 0)),
        scratch_shapes=[pltpu.VMEM((f, d), F32), pltpu.VMEM((f, d), BF16), pltpu.SemaphoreType.DMA((1,))])
    return pl.pallas_call(
        _expert_down_kernel,
        grid_spec=grid_spec,
        out_shape=jax.ShapeDtypeStruct((rows, d), F32),
        compiler_params=_cparams(("arbitrary",), f * d * 6 + 6 * MOE_ROWS * d * 4 + 8 * 2**20),
        name="moe_expert_down",
    )(*tabs, hs, w2)


def _combine_kernel(dest_ref, x1_ref, g2_ref, gates_ref, y_ref, o_ref, ybuf, sem, *, tm, base):
    i = pl.program_id(0)
    slot = i % 2

    def gather(step, s):
        _row_copies(dest_ref, base + step * tm, tm,
                    lambda t, k, d: pltpu.make_async_copy(y_ref.at[pl.ds(d, 1)], ybuf.at[s, k, pl.ds(t, 1)],
                                                          sem.at[s]))

    @pl.when(i == 0)
    def _():
        gather(0, 0)

    @pl.when(i + 1 < pl.num_programs(0))
    def _():
        gather(i + 1, 1 - slot)

    for k in range(2):
        pltpu.make_async_copy(y_ref.at[pl.ds(0, tm)], ybuf.at[slot, k], sem.at[slot]).wait()
    gates = gates_ref[...]
    ff = gates[:, 0:1] * ybuf[slot, 0] + gates[:, 1:2] * ybuf[slot, 1]
    o_ref[...] = x1_ref[...] + g2_ref[...] * ff


def _combine(dest_flat, x1, g2_spec, g2, gates, y, tm, base):
    m, d = x1.shape
    grid_spec = pltpu.PrefetchScalarGridSpec(
        num_scalar_prefetch=1, grid=(m // tm,),
        in_specs=[pl.BlockSpec((tm, d), lambda i, dr: (i, 0)), g2_spec,
                  pl.BlockSpec((tm, LANES), lambda i, dr: (i, 0)),
                  pl.BlockSpec(memory_space=pl.ANY)],
        out_specs=pl.BlockSpec((tm, d), lambda i, dr: (i, 0)),
        scratch_shapes=[pltpu.VMEM((2, 2, tm, d), F32), pltpu.SemaphoreType.DMA((2,))])
    return pl.pallas_call(
        functools.partial(_combine_kernel, tm=tm, base=base),
        grid_spec=grid_spec,
        out_shape=jax.ShapeDtypeStruct((m, d), F32),
        compiler_params=_cparams(("arbitrary",), 8 * tm * d * 4 + 8 * 2**20),
        name="moe_combine",
    )(dest_flat, x1, g2, gates, y)


def kernel(x_prompt, x_sample, cache_k, cache_v, page_table, state_mlstm_C, state_mlstm_n, state_mlstm_m, state_conv,
           c_prompt, c_sample, w_ada, b_ada, norm1_g, norm2_g, w_in, conv_w, conv_b, b_igate, b_fgate, mlstm_out_g,
           q_norm_g, k_norm_g, lam_q1, lam_k1, lam_q2, lam_k2, diff_out_g, w_out, w_grp, b_grp, w_rt, b_rt, w1, w3, w2):
    B, T, D = x_prompt.shape
    DB, TS, _ = x_sample.shape
    depth = w_ada.shape[0]
    assert depth == 1, "single-layer trunk"
    n_pages, page = page_table.shape[1], cache_k.shape[2]
    past = n_pages * page
    W = N_HEADS * HEAD_W
    lam_init = 0.8 - 0.6 * math.exp(-0.3 * 0)
    NP, NS = B * T, DB * TS
    SP = SAMPLE_PAD

    c_all = jnp.concatenate([c_prompt, c_sample, jnp.zeros((16 - B - DB, D), F32)], axis=0)
    w_main, w_gate = _regroup_w_in(w_in[0].T, 4 * W, 2 * N_HEADS)
    gate_b = jnp.pad(jnp.concatenate([b_igate[0], b_fgate[0]]), (0, LANES - 2 * N_HEADS)).reshape(1, LANES)
    cw = jnp.pad(conv_w[0], ((0, SUBLANES - CONV_W), (0, 0)))
    cb = conv_b[0].reshape(1, 2 * W)
    og_m = mlstm_out_g[0].reshape(1, HEAD_W)
    wo = w_out[0].astype(BF16)
    w_route = jnp.concatenate([w_grp[0], w_rt[0].transpose(1, 0, 2).reshape(D, N_EXPERTS)], axis=1)
    w_route = jnp.pad(w_route, ((0, 0), (0, LANES - N_GROUPS - N_EXPERTS))).astype(BF16)
    b_route = jnp.pad(jnp.concatenate([b_grp[0], b_rt[0].reshape(-1)]), (0, LANES - N_GROUPS - N_EXPERTS)).reshape(1, LANES)
    lams = (lam_q1[0], lam_k1[0], lam_q2[0], lam_k2[0])

    mod = _ada(c_all, w_ada[0], b_ada[0])
    mod3 = mod.reshape(16, 1, 6 * D)
    mod_s = jnp.repeat(mod[B:B + DB], SP, axis=0)

    def p_mod(chunk, tm):
        return pl.BlockSpec((None, 1, D), lambda i, *_: ((i * tm) // T, 0, chunk))

    def s_mod(chunk, tm):
        return pl.BlockSpec((tm, D), lambda i, *_: (i, chunk))

    xp = x_prompt.reshape(NP, D)
    tm_in = 1024
    proj_p, gates_p = _inproj(xp, (p_mod(1, tm_in), p_mod(0, tm_in)), (mod3, mod3), norm1_g[0], w_main, w_gate, tm_in)
    cos_p, sin_p = _rope_tables(jnp.arange(T, dtype=I32))
    knew_p, kb_p, vnew_p, qt_p, vt_p = _qkrope(proj_p, 4, cos_p, sin_p, q_norm_g[0], k_norm_g[0], 256, B)
    ha_p = _dattn_prompt(qt_p, kb_p, vt_p, lams, diff_out_g[0], lam_init)
    zeros = lambda *s: jnp.zeros(s, F32)
    hm_p, C_p, n_p, m_p, cv_p = _mlstm(
        proj_p, gates_p, cw, cb, gate_b, og_m,
        zeros(B, N_HEADS, HEAD_W, HEAD_W), zeros(B, N_HEADS, HEAD_W), zeros(B, N_HEADS, LANES),
        zeros(B, SUBLANES, 2 * W), B, MLSTM_L, T // MLSTM_L, MLSTM_L)
    tm_o = 256
    x1_p, hn2_p, ids_p, gts_p = _outproj(
        hm_p, ha_p, xp, (p_mod(2, tm_o), p_mod(4, tm_o), p_mod(3, tm_o)), (mod3, mod3, mod3),
        norm2_g[0], wo, w_route, b_route, tm_o)

    MS = DB * SP
    xs_pad = jnp.pad(x_sample, ((0, 0), (0, SP - TS), (0, 0))).reshape(MS, D)
    proj_s, gates_s = _inproj(xs_pad, (s_mod(1, MS), s_mod(0, MS)), (mod_s, mod_s), norm1_g[0], w_main, w_gate, MS)
    pos_s = past + (jnp.arange(MS, dtype=I32) % SP)
    cos_s, sin_s = _rope_tables(pos_s)
    knew_s, kb_s, vnew_s, q_s, vb_s = _qkrope(proj_s, 4, cos_s, sin_s, q_norm_g[0], k_norm_g[0], MS)
    qr = q_s.reshape(DB, SP, N_HEADS, 2, DQK)[:, :TS].transpose(0, 2, 3, 1, 4)
    zq = jnp.zeros_like(qr[:, :, 0])
    qall = jnp.stack([jnp.concatenate([qr[:, :, 0], zq], axis=-1),
                      jnp.concatenate([zq, qr[:, :, 1]], axis=-1)], axis=2).reshape(DB, N_HEADS * 2 * TS, HEAD_W)
    new_rows = page * N_HEADS
    pad_new = lambda a: jnp.pad(a.reshape(DB, SP * N_HEADS, HEAD_W), ((0, 0), (0, new_rows - SP * N_HEADS), (0, 0)))
    ha_s = _dattn_paged(qall, cache_k[0].reshape(-1, new_rows, HEAD_W), cache_v[0].reshape(-1, new_rows, HEAD_W),
                        page_table, pad_new(kb_s), pad_new(vb_s), lams, diff_out_g[0], TS, lam_init)
    m0_s = jnp.broadcast_to(state_mlstm_m[0][:, :, None], (DB, N_HEADS, LANES))
    cv0_s = jnp.pad(state_conv[0], ((0, 0), (SUBLANES - (CONV_W - 1), 0), (0, 0)))
    hm_s, C_s, n_s, m_s, cv_s = _mlstm(proj_s, gates_s, cw, cb, gate_b, og_m, state_mlstm_C[0], state_mlstm_n[0],
                                       m0_s, cv0_s, DB, SP, 1, TS)
    x1_s, hn2_s, ids_s, gts_s = _outproj(
        hm_s, ha_s, xs_pad, (s_mod(2, MS), s_mod(4, MS), s_mod(3, MS)), (mod_s, mod_s, mod_s),
        norm2_g[0], wo, w_route, b_route, MS)
    real = lambda a: a.reshape(DB, SP, -1)[:, :TS].reshape(NS, -1)
    x1_s, hn2_s, ids_s, gts_s = real(x1_s), real(hn2_s), real(ids_s), real(gts_s)

    NT = NP + NS
    tm_r = 256
    n_rank = -(-NT // tm_r) * tm_r
    ids_all = jnp.concatenate([ids_p, ids_s, jnp.full((n_rank - NT, LANES), -1, I32)], axis=0)
    dest2, cnt = _ranks(ids_all, tm_r)
    dest = dest2[:NT, :2].reshape(-1)
    counts = cnt[0, :N_EXPERTS].astype(I32)
    e_blocks = (counts + MOE_ROWS - 1) // MOE_ROWS
    pend = jnp.cumsum(e_blocks)
    n_blocks = (2 * NT + N_EXPERTS * (MOE_ROWS - 1)) // MOE_ROWS
    n_valid = pend[-1:]
    blk_id = jnp.arange(n_blocks, dtype=I32)
    blk = jnp.minimum(blk_id, n_valid[0] - 1)
    blk_e = jnp.sum((blk[:, None] >= pend[None, :]).astype(I32), axis=1)
    in_expert = blk_id - (pend - e_blocks)[blk_e]
    blk_rows = jnp.where(blk_id < n_valid[0], jnp.clip(counts[blk_e] - in_expert * MOE_ROWS, 0, MOE_ROWS), 0).astype(I32)
    blk_first = jnp.concatenate([jnp.ones((1,), I32), (blk_e[1:] != blk_e[:-1]).astype(I32)])
    first_pos = jnp.where(blk_first == 1, jnp.arange(n_blocks, dtype=I32), n_blocks)
    next_first = jnp.concatenate([lax.cummin(first_pos, reverse=True)[1:], jnp.full((1,), n_blocks, I32)])
    blk_next = jnp.where(next_first < n_blocks, blk_e[jnp.minimum(next_first, n_blocks - 1)], -1).astype(I32)
    tabs = (blk_e, blk_first, blk_next, blk_rows)
    xs_sorted = _dispatch(dest, hn2_p, jnp.zeros((n_blocks * MOE_ROWS, D // 2), jnp.uint32), 256, 0)
    xs_sorted = _dispatch(dest, hn2_s, xs_sorted, NS, NP)
    h_sorted = _expert_up(tabs, xs_sorted, w1, w3)
    y_sorted = _expert_down(tabs, h_sorted, w2)
    tm_c = 128
    y_p = _combine(dest, x1_p, pl.BlockSpec((None, 1, D), lambda i, dr: ((i * tm_c) // T, 0, 5)), mod3, gts_p,
                   y_sorted, tm_c, 0)
    g2_s = jnp.repeat(mod[B:B + DB, 5 * D:], TS, axis=0)
    y_s = _combine(dest, x1_s, pl.BlockSpec((NS, D), lambda i, dr: (0, 0)), g2_s, gts_s, y_sorted, NS, NP)

    st = lambda a: a[None]
    k5 = lambda a, nb, t: a.reshape(1, nb, t, N_HEADS, HEAD_W)
    tok_s = lambda a: a.reshape(DB, SP, -1)[:, :TS]
    conv_out = lambda cv: cv[None, :, SUBLANES - (CONV_W - 1):, :]
    return (y_p.reshape(B, T, D), y_s.reshape(DB, TS, D),
            k5(knew_p, B, T), k5(vnew_p, B, T), st(C_p), st(n_p), st(m_p[:, :, 0]), conv_out(cv_p),
            k5(tok_s(knew_s), DB, TS), k5(tok_s(vnew_s), DB, TS), st(C_s), st(n_s), st(m_s[:, :, 0]), conv_out(cv_s))
```

```python
import functools
import math

import jax
import jax.numpy as jnp
from jax import lax
from jax.experimental import pallas as pl
from jax.experimental.pallas import tpu as pltpu

F32 = jnp.float32
BF16 = jnp.bfloat16
I32 = jnp.int32

EPS = 1e-6
ROPE_THETA = 10000.0
NEG = -1e30

LANES = 128
SUBLANES = 8
VMEM_LIMIT_CAP = 56 * 1024 * 1024

N_HEADS = 8
HEAD_W = 128
DQK = 64
N_GROUPS = 4
EXPERTS_PER_GROUP = 8
N_EXPERTS = N_GROUPS * EXPERTS_PER_GROUP
CONV_W = 4
MOE_ROWS = 256
SAMPLE_PAD = 8
MLSTM_L = 128
PAGES_PER_STEP = 16


def _cparams(sem, vmem_bytes):
    return pltpu.CompilerParams(dimension_semantics=sem,
                                vmem_limit_bytes=int(min(max(vmem_bytes, 16 * 2**20), VMEM_LIMIT_CAP)))


def _silu(x):
    return x * jax.nn.sigmoid(x)


def _dot(a, b):
    return jnp.dot(a, b, preferred_element_type=F32)


def _pack_bf16_pairs(x):
    n = x.shape[1] // 2
    bits = lax.bitcast_convert_type(x.astype(BF16).astype(F32), jnp.uint32)
    return bits[:, :n] | (bits[:, n:] >> 16)


def _unpack_bf16_pairs(w):
    left = lax.bitcast_convert_type(w & jnp.uint32(0xFFFF0000), F32)
    right = lax.bitcast_convert_type(w << 16, F32)
    return left.astype(BF16), right.astype(BF16)


def _dot_nt(a, b):
    return lax.dot_general(a, b, (((1,), (1,)), ((), ())), preferred_element_type=F32)


def _dot_tn(a, b):
    return lax.dot_general(a, b, (((0,), (0,)), ((), ())), preferred_element_type=F32)


def _ada_kernel(c_ref, w_ref, b_ref, o_ref):
    s = _silu(c_ref[...]).astype(BF16)
    o_ref[...] = _dot(s, w_ref[...].astype(BF16)) + b_ref[...]


def _ada(c_all, w_ada, b_ada):
    rows, d = c_all.shape
    n = w_ada.shape[1]
    tn = 1024
    return pl.pallas_call(
        _ada_kernel,
        grid=(n // tn,),
        in_specs=[pl.BlockSpec((rows, d), lambda j: (0, 0)),
                  pl.BlockSpec((d, tn), lambda j: (0, j)),
                  pl.BlockSpec((1, tn), lambda j: (0, j))],
        out_specs=pl.BlockSpec((rows, tn), lambda j: (0, j)),
        out_shape=jax.ShapeDtypeStruct((rows, n), F32),
        compiler_params=_cparams(("arbitrary",), 2 * d * tn * 4 + 3 * d * tn * 2 + 2**22),
        name="ada_mod",
    )(c_all, w_ada, b_ada.reshape(1, n))


def _regroup_kernel(w_hbm, main_ref, gate_ref, buf, gbuf, sem, *, lo, ng, tr):
    j = pl.program_id(0)
    slot = j % 2

    def block_copy(jj, s):
        start = pl.multiple_of(jnp.where(jj * tr < lo, jj * tr, jj * tr + ng), SUBLANES)
        return pltpu.make_async_copy(w_hbm.at[pl.ds(start, tr)], buf.at[s], sem.at[s])

    gate_copy = pltpu.make_async_copy(w_hbm.at[pl.ds(lo, ng)], gbuf, sem.at[2])

    @pl.when(j == 0)
    def _():
        block_copy(0, 0).start()
        gate_copy.start()

    @pl.when(j + 1 < pl.num_programs(0))
    def _():
        block_copy(j + 1, 1 - slot).start()

    block_copy(j, slot).wait()
    main_ref[...] = buf[slot].astype(BF16)

    @pl.when(j == 0)
    def _():
        gate_copy.wait()
        pad = jnp.zeros((LANES - ng, gbuf.shape[1]), F32)
        gate_ref[...] = jnp.concatenate([gbuf[...], pad], axis=0).astype(BF16)


def _regroup_w_in(w_t, lo, ng):
    n, d = w_t.shape
    tr = 1024
    assert lo % tr == 0 and (n - ng) % tr == 0 and ng % SUBLANES == 0
    return pl.pallas_call(
        functools.partial(_regroup_kernel, lo=lo, ng=ng, tr=tr),
        grid=((n - ng) // tr,),
        in_specs=[pl.BlockSpec(memory_space=pl.ANY)],
        out_specs=[pl.BlockSpec((tr, d), lambda j: (j, 0)), pl.BlockSpec((LANES, d), lambda j: (0, 0))],
        out_shape=[jax.ShapeDtypeStruct((n - ng, d), BF16), jax.ShapeDtypeStruct((LANES, d), BF16)],
        scratch_shapes=[pltpu.VMEM((2, tr, d), F32), pltpu.VMEM((ng, d), F32), pltpu.SemaphoreType.DMA((3,))],
        compiler_params=_cparams(("arbitrary",), 2 * tr * d * 4 + 4 * tr * d * 2 + 8 * 2**20),
        name="w_in_layout",
    )(w_t)


def _inproj_kernel(x_ref, sc_ref, sh_ref, g_ref, w_ref, wg_ref, o_ref, og_ref, hn_ref):
    @pl.when(pl.program_id(1) == 0)
    def _():
        rows = min(x_ref.shape[0], 256)

        def chunk(c, _):
            r = pl.ds(pl.multiple_of(c * rows, rows), rows)
            x = x_ref[r, :]
            y = x * lax.rsqrt(jnp.mean(x * x, axis=-1, keepdims=True) + EPS) * g_ref[...]
            sc = sc_ref[...] if sc_ref.shape[0] == 1 else sc_ref[r, :]
            sh = sh_ref[...] if sh_ref.shape[0] == 1 else sh_ref[r, :]
            hb = (y * (1.0 + sc) + sh).astype(BF16)
            hn_ref[r, :] = hb
            og_ref[r, :] = _dot_nt(hb, wg_ref[...])
            return 0
        lax.fori_loop(0, x_ref.shape[0] // rows, chunk, 0)

    o_ref[...] = _dot_nt(hn_ref[...], w_ref[...])


def _inproj(x, mod_specs, mods, g, w, wg, tm):
    m, d = x.shape
    n = w.shape[0]
    tn = 1024
    sc_spec, sh_spec = mod_specs
    return pl.pallas_call(
        _inproj_kernel,
        grid=(m // tm, n // tn),
        in_specs=[pl.BlockSpec((tm, d), lambda i, j: (i, 0)), sc_spec, sh_spec,
                  pl.BlockSpec((1, d), lambda i, j: (0, 0)),
                  pl.BlockSpec((tn, d), lambda i, j: (j, 0)),
                  pl.BlockSpec((LANES, d), lambda i, j: (0, 0))],
        out_specs=[pl.BlockSpec((tm, tn), lambda i, j: (i, j)),
                   pl.BlockSpec((tm, LANES), lambda i, j: (i, 0))],
        out_shape=[jax.ShapeDtypeStruct((m, n), F32), jax.ShapeDtypeStruct((m, LANES), F32)],
        scratch_shapes=[pltpu.VMEM((tm, d), BF16)],
        compiler_params=_cparams(("arbitrary", "arbitrary"),
                                 2 * tm * d * 4 + 2 * d * tn * 2 + 2 * tm * tn * 4 + tm * d * 2 + 12 * 2**20),
        name="norm1_inproj",
    )(x, mods[0], mods[1], g.reshape(1, d), w, wg)


def _qkrope_kernel(q_ref, k_ref, v_ref, cos_ref, sin_ref, qg_ref, kg_ref, ko_ref, kb_ref, vo_ref, qx_ref, vx_ref,
                   *, transposed):
    cos = cos_ref[...]
    sin = sin_ref[...]
    lane = lax.broadcasted_iota(I32, (1, LANES), 1)
    first_map = lane < DQK
    first_half = (lane % DQK) < (DQK // 2)

    group_mean = jnp.where(lax.broadcasted_iota(I32, (LANES, LANES), 0) // DQK
                           == lax.broadcasted_iota(I32, (LANES, LANES), 1) // DQK, 1.0 / DQK, 0.0).astype(BF16)

    def norm_rope(x, g):
        x2 = x * x
        if transposed:
            ms = functools.reduce(jnp.add, [_dot(part, group_mean) for part in _split3(x2)])
            inv = lax.rsqrt(ms + EPS)
        else:
            s1 = jnp.sum(jnp.where(first_map, x2, 0.0), axis=-1, keepdims=True)
            s2 = jnp.sum(jnp.where(first_map, 0.0, x2), axis=-1, keepdims=True)
            inv = jnp.where(first_map, lax.rsqrt(s1 / DQK + EPS), lax.rsqrt(s2 / DQK + EPS))
        y = x * inv * g
        partner = jnp.where(first_half, pltpu.roll(y, LANES - DQK // 2, 1), pltpu.roll(y, DQK // 2, 1))
        return y * cos + partner * sin

    for h in range(N_HEADS):
        sl = slice(h * HEAD_W, (h + 1) * HEAD_W)
        qh = norm_rope(q_ref[:, sl], qg_ref[...]) * (DQK ** -0.5)
        kh = norm_rope(k_ref[:, sl], kg_ref[...])
        vh = v_ref[:, sl]
        ko_ref[:, sl] = kh
        kb_ref[:, sl] = kh.astype(BF16)
        vo_ref[:, sl] = vh
        if transposed:
            qx_ref[sl, :] = qh.T.astype(BF16)
            vx_ref[sl, :] = vh.T.astype(BF16)
        else:
            qx_ref[:, sl] = qh.astype(BF16)
            vx_ref[:, sl] = vh.astype(BF16)


def _qkrope(proj, col0, cos_t, sin_t, qg, kg, tm, transposed_batches=0):
    m = proj.shape[0]
    w = N_HEADS * HEAD_W
    nt = cos_t.shape[0] // tm
    blk = lambda c: pl.BlockSpec((tm, w), lambda i, c=c: (i, c))
    tab = pl.BlockSpec((tm, LANES), lambda i: (i % nt, 0))
    gsp = pl.BlockSpec((1, LANES), lambda i: (0, 0))
    osp = pl.BlockSpec((tm, w), lambda i: (i, 0))
    g2 = lambda g: jnp.tile(g, 2).reshape(1, LANES)
    out_specs = [osp] * 3
    out_shape = [jax.ShapeDtypeStruct((m, w), F32), jax.ShapeDtypeStruct((m, w), BF16),
                 jax.ShapeDtypeStruct((m, w), F32)]
    if transposed_batches:
        out_specs += [pl.BlockSpec((None, None, w, tm), lambda i: (i // nt, i % nt, 0, 0))] * 2
        out_shape += [jax.ShapeDtypeStruct((transposed_batches, nt, w, tm), BF16)] * 2
    else:
        out_specs += [osp] * 2
        out_shape += [jax.ShapeDtypeStruct((m, w), BF16)] * 2
    return pl.pallas_call(
        functools.partial(_qkrope_kernel, transposed=bool(transposed_batches)),
        grid=(m // tm,),
        in_specs=[blk(col0), blk(col0 + 1), blk(col0 + 2), tab, tab, gsp, gsp],
        out_specs=out_specs,
        out_shape=out_shape,
        compiler_params=_cparams(("arbitrary",), 2 * tm * w * (3 * 4 + 2 * 4 + 5 * 2) + 2**22),
        name="qknorm_rope",
    )(proj, proj, proj, cos_t, sin_t, g2(qg), g2(kg))


def _rope_tables(pos):
    half = DQK // 2
    inv = ROPE_THETA ** (-jnp.arange(half, dtype=F32) / half)
    ang = pos.astype(F32)[:, None] * inv[None, :]
    cos = jnp.tile(jnp.cos(ang), (1, LANES // half))
    sin = jnp.sin(ang)
    sin = jnp.tile(jnp.concatenate([-sin, sin], axis=1), (1, LANES // DQK))
    return cos, sin


def _lam_value(l1, l2, l3, l4, lam_init):
    a = jnp.sum(l1[...] * l2[...], axis=-1, keepdims=True)
    b = jnp.sum(l3[...] * l4[...], axis=-1, keepdims=True)
    return jnp.exp(a) - jnp.exp(b) + lam_init


def _dattn_kernel(qt_ref, k_ref, vt_ref, l1_ref, l2_ref, l3_ref, l4_ref, og_ref, o_ref,
                  s_a, s_b, p_a, p_b, cm_ref, stat_ref, acc_ref, *, tq, lam_init):
    qi = pl.program_id(2)
    tk = tq // 2
    s_bufs, p_bufs = (s_a, s_b), (p_a, p_b)
    qt = jnp.concatenate([qt_ref[0], qt_ref[1]], axis=1)
    feat = lax.broadcasted_iota(I32, (HEAD_W, 1), 0)
    zero = jnp.zeros_like(qt)
    qcat = jnp.concatenate([jnp.where(feat < DQK, qt, zero), jnp.where(feat < DQK, zero, qt)], axis=1)
    key = lax.broadcasted_iota(I32, (tk, 2 * tq), 0)
    qry = lax.broadcasted_iota(I32, (tk, 2 * tq), 1) % tq

    def set_scores(slot, s, mask):
        if mask is not None:
            s = jnp.where(mask, s, NEG)
        s_bufs[slot][...] = s
        cm_ref[slot:slot + 1, :] = jnp.max(s, axis=0, keepdims=True)

    def scores_into(j, slot, mask=None):
        off = pl.multiple_of(j * tk, tk)
        set_scores(slot, _dot(k_ref[pl.ds(off, tk), :], qcat), mask)

    def softmax_pv(j, slot):
        pv = _dot(vt_ref[jnp.maximum(j - 1, 0)], p_bufs[1 - slot][...])
        m = stat_ref[0:1, :]
        m_new = jnp.maximum(m, cm_ref[slot:slot + 1, :])
        alpha = jnp.exp(m - m_new)
        p = jnp.exp(s_bufs[slot][...] - m_new)
        stat_ref[0:1, :] = m_new
        stat_ref[1:2, :] = alpha * stat_ref[1:2, :] + jnp.sum(p, axis=0, keepdims=True)
        p_bufs[slot][...] = p.astype(BF16)
        acc_ref[...] = alpha * (acc_ref[...] + pv)

    stat_ref[0:1, :] = jnp.full((1, 2 * tq), NEG, F32)
    stat_ref[1:2, :] = jnp.zeros((1, 2 * tq), F32)
    acc_ref[...] = jnp.zeros(acc_ref.shape, F32)
    p_b[...] = jnp.zeros(p_b.shape, BF16)
    scores_into(0, 0)

    def body(t, _):
        scores_into(2 * t + 1, 1)
        softmax_pv(2 * t, 0)
        scores_into(2 * t + 2, 0)
        softmax_pv(2 * t + 1, 1)
        return 0

    lax.fori_loop(0, qi, body, 0)
    set_scores(0, s_a[...], key <= qry)
    scores_into(2 * qi + 1, 1, key + tk <= qry)
    softmax_pv(2 * qi, 0)
    softmax_pv(2 * qi + 1, 1)
    acc = (acc_ref[...] + _dot(vt_ref[2 * qi + 1], p_b[...])) / stat_ref[1:2, :]
    lam = _lam_value(l1_ref, l2_ref, l3_ref, l4_ref, lam_init)
    o = (acc[:, :tq] - lam * acc[:, tq:]).T
    o = o * lax.rsqrt(jnp.mean(o * o, axis=-1, keepdims=True) + EPS) * og_ref[...]
    o_ref[...] = (o * (1.0 - lam_init)).astype(BF16)


def _dattn_prompt(qt, k, vt, lams, og, lam_init):
    nb, nk, _, tk = qt.shape
    t = nk * tk
    tq = 2 * tk
    nq = t // tq
    lsp = pl.BlockSpec((1, DQK), lambda b, h, i: (0, 0))
    return pl.pallas_call(
        functools.partial(_dattn_kernel, tq=tq, lam_init=lam_init),
        grid=(nb, N_HEADS, nq),
        in_specs=[pl.BlockSpec((None, 2, HEAD_W, tk), lambda b, h, i: (b, i, h, 0)),
                  pl.BlockSpec((t, HEAD_W), lambda b, h, i: (b, h)),
                  pl.BlockSpec((None, nk, HEAD_W, tk), lambda b, h, i: (b, 0, h, 0)),
                  lsp, lsp, lsp, lsp,
                  pl.BlockSpec((1, HEAD_W), lambda b, h, i: (0, 0))],
        out_specs=pl.BlockSpec((tq, HEAD_W), lambda b, h, i: (b * nq + i, h)),
        out_shape=jax.ShapeDtypeStruct(k.shape, BF16),
        scratch_shapes=[pltpu.VMEM((tk, 2 * tq), F32), pltpu.VMEM((tk, 2 * tq), F32),
                        pltpu.VMEM((tk, 2 * tq), BF16), pltpu.VMEM((tk, 2 * tq), BF16),
                        pltpu.VMEM((SUBLANES, 2 * tq), F32), pltpu.VMEM((SUBLANES, 2 * tq), F32),
                        pltpu.VMEM((HEAD_W, 2 * tq), F32)],
        compiler_params=_cparams(("arbitrary",) * 3, 32 * 2**20),
        name="diff_attn_prompt",
    )(qt, k, vt, *[x.reshape(1, DQK) for x in lams], og.reshape(1, HEAD_W))


def _dattn_paged_kernel(pt_ref, q_ref, *refs, pps, n_new, n_pages, lam_init):
    k_refs = refs[:pps]
    v_refs = refs[pps:2 * pps]
    kn_ref, vn_ref, l1_ref, l2_ref, l3_ref, l4_ref, og_ref, o_ref, s_scr, m_sc, l_sc, acc_sc = refs[2 * pps:]
    phase = pl.program_id(1)
    step_id = pl.program_id(2)
    nsteps = pl.num_programs(2)
    prow = s_scr.shape[2]
    nchunk = prow // LANES
    lane = lax.broadcasted_iota(I32, (SUBLANES, LANES), 1)
    lane_head = lane % N_HEADS
    page_head = lax.broadcasted_iota(I32, (SUBLANES, prow), 1) % N_HEADS
    row = lax.broadcasted_iota(I32, (SUBLANES, prow), 0)
    tile = lambda x: jnp.concatenate([x] * nchunk, axis=1)

    def lane_class_reduce(x, op):
        for sh in (8, 16, 32, 64):
            x = op(x, pltpu.roll(x, sh, 1))
        return x

    def score_pages(ks, first, valid):
        q = q_ref[...]
        parts = []
        for pi, kp in enumerate(ks):
            sf = _dot_nt(q, kp.astype(BF16))
            page = []
            for c in range(nchunk):
                blk = sf[:, c * LANES:(c + 1) * LANES]
                s = jnp.zeros((SUBLANES, LANES), F32)
                for h in range(N_HEADS):
                    s = jnp.where(lane_head == h, blk[h * SUBLANES:(h + 1) * SUBLANES, :], s)
                page.append(s)
            s_page = jnp.concatenate(page, axis=1)
            if valid is not None:
                s_page = jnp.where(valid, s_page, NEG)
                page = [s_page[:, c * LANES:(c + 1) * LANES] for c in range(nchunk)]
            s_scr[first + pi] = s_page
            parts += page
        m_old = m_sc[...]
        m_new = jnp.maximum(m_old, functools.reduce(jnp.maximum, parts))
        l_sc[...] = jnp.exp(m_old - m_new) * l_sc[...] + functools.reduce(jnp.add, [jnp.exp(s - m_new) for s in parts])
        m_sc[...] = m_new

    def value_pages(vs, first):
        lam = _lam_value(l1_ref, l2_ref, l3_ref, l4_ref, lam_init)
        m_full = tile(m_sc[...])
        l_full = tile(l_sc[...])
        pv = jnp.zeros(acc_sc.shape, F32)
        for pi, vp in enumerate(vs):
            e = jnp.exp(s_scr[first + pi] - m_full) / l_full
            a = jnp.where(row < n_new, e - lam * pltpu.roll(e, SUBLANES - n_new, 0), 0.0)
            pexp = jnp.concatenate([jnp.where(page_head == h, a, 0.0) for h in range(N_HEADS)], axis=0)
            pv = pv + _dot(pexp.astype(BF16), vp.astype(BF16))
        acc_sc[...] = acc_sc[...] + pv

    @pl.when(phase == 0)
    def _():
        @pl.when(step_id == 0)
        def _():
            m_sc[...] = jnp.full(m_sc.shape, NEG, F32)
            l_sc[...] = jnp.zeros(l_sc.shape, F32)

        score_pages([r[...] for r in k_refs], step_id * pps, None)

        @pl.when(step_id == nsteps - 1)
        def _():
            tok = lax.broadcasted_iota(I32, (SUBLANES, prow), 1) // N_HEADS
            score_pages([kn_ref[...]], n_pages, tok <= row % n_new)
            m_lane = m_sc[...]
            m_head = lane_class_reduce(m_lane, jnp.maximum)
            l_sc[...] = lane_class_reduce(l_sc[...] * jnp.exp(m_lane - m_head), jnp.add)
            m_sc[...] = m_head

    @pl.when(phase == 1)
    def _():
        @pl.when(step_id == 0)
        def _():
            acc_sc[...] = jnp.zeros(acc_sc.shape, F32)

        value_pages([r[...] for r in v_refs], step_id * pps)

        @pl.when(step_id == nsteps - 1)
        def _():
            value_pages([vn_ref[...]], n_pages)
            acc = acc_sc[...]
            outs = []
            for h in range(N_HEADS):
                o = acc[h * SUBLANES:(h + 1) * SUBLANES]
                o = o * lax.rsqrt(jnp.mean(o * o, axis=-1, keepdims=True) + EPS) * og_ref[...]
                outs.append(o * (1.0 - lam_init))
            o_ref[...] = jnp.concatenate(outs, axis=1).astype(BF16)


def _dattn_paged(qall, cache_k, cache_v, page_table, k_new, v_new, lams, og, n_new, lam_init):
    nb, n_pages = page_table.shape
    pps = PAGES_PER_STEP
    prow = cache_k.shape[1]
    nsteps = n_pages // pps
    kspec = lambda j: pl.BlockSpec(
        (None, prow, HEAD_W),
        lambda b, ph, s, pt, j=j: (pt[b * n_pages + (s * (1 - ph) + (nsteps - 1) * ph) * pps + j], 0, 0))
    vspec = lambda j: pl.BlockSpec(
        (None, prow, HEAD_W), lambda b, ph, s, pt, j=j: (pt[b * n_pages + s * ph * pps + j], 0, 0))
    lsp = pl.BlockSpec((1, DQK), lambda b, ph, s, pt: (0, 0))
    nspec = pl.BlockSpec((None, k_new.shape[1], HEAD_W), lambda b, ph, s, pt: (b, 0, 0))
    grid_spec = pltpu.PrefetchScalarGridSpec(
        num_scalar_prefetch=1,
        grid=(nb, 2, nsteps),
        in_specs=[pl.BlockSpec((None, N_HEADS * SUBLANES, HEAD_W), lambda b, ph, s, pt: (b, 0, 0))]
                 + [kspec(j) for j in range(pps)] + [vspec(j) for j in range(pps)]
                 + [nspec, nspec, lsp, lsp, lsp, lsp, pl.BlockSpec((1, HEAD_W), lambda b, ph, s, pt: (0, 0))],
        out_specs=pl.BlockSpec((SUBLANES, N_HEADS * HEAD_W), lambda b, ph, s, pt: (b, 0)),
        scratch_shapes=[pltpu.VMEM((n_pages + 1, SUBLANES, prow), F32),
                        pltpu.VMEM((SUBLANES, LANES), F32), pltpu.VMEM((SUBLANES, LANES), F32),
                        pltpu.VMEM((N_HEADS * SUBLANES, HEAD_W), F32)])
    return pl.pallas_call(
        functools.partial(_dattn_paged_kernel, pps=pps, n_new=n_new, n_pages=n_pages, lam_init=lam_init),
        grid_spec=grid_spec,
        out_shape=jax.ShapeDtypeStruct((nb * SUBLANES, N_HEADS * HEAD_W), BF16),
        compiler_params=_cparams(("arbitrary",) * 3,
                                 4 * pps * prow * HEAD_W * 4 + (n_pages + 1) * SUBLANES * prow * 4 + 24 * 2**20),
        name="diff_attn_paged",
    )(page_table.reshape(-1), qall, *([cache_k] * pps), *([cache_v] * pps), k_new, v_new,
      *[x.reshape(1, DQK) for x in lams], og.reshape(1, HEAD_W))


def _split3(a):
    hi = a.astype(BF16)
    r1 = a - hi.astype(F32)
    mid = r1.astype(BF16)
    lo = (r1 - mid.astype(F32)).astype(BF16)
    return hi, mid, lo


def _mlstm_kernel(qk_ref, v_ref, o_ref, gt_ref, cw_ref, cb_ref, gb_ref, og_ref,
                  C0_ref, n0_ref, m0_ref, cv0_ref,
                  h_ref, C_ref, n_ref, m_ref, cv_ref, xbuf, *, L, rows_in, n_valid):
    c = pl.program_id(1)
    width = N_HEADS * HEAD_W

    @pl.when(c == 0)
    def _():
        C_ref[...] = C0_ref[...]
        n_ref[...] = n0_ref[...]
        m_ref[...] = m0_ref[...]
        xbuf[0:SUBLANES, :] = cv0_ref[...]

    def padded(ref):
        x = ref[...]
        if rows_in < L:
            x = jnp.concatenate([x, jnp.zeros((L - rows_in, x.shape[1]), x.dtype)], axis=0)
        return x

    xbuf[SUBLANES:SUBLANES + L, :] = padded(qk_ref)
    taps = [xbuf[SUBLANES - (CONV_W - 1) + j:SUBLANES - (CONV_W - 1) + j + L, :] * cw_ref[j:j + 1, :]
            for j in range(CONV_W)]
    conv = cb_ref[...] + functools.reduce(jnp.add, taps)
    cv_ref[...] = xbuf[n_valid:n_valid + SUBLANES, :]
    xbuf[0:SUBLANES, :] = xbuf[L:L + SUBLANES, :]
    conv = _silu(conv)
    q_all = conv[:, :width].astype(BF16)
    k_all = conv[:, width:] * (HEAD_W ** -0.5)
    v_all = padded(v_ref).astype(BF16)
    o_all = padded(o_ref)

    g = padded(gt_ref) + gb_ref[...]
    lane = lax.broadcasted_iota(I32, (L, LANES), 1)
    rowi = lax.broadcasted_iota(I32, (L, LANES), 0)
    is_f = (lane >= N_HEADS) & (lane < 2 * N_HEADS)
    logsig = jnp.minimum(g, 0.0) - jnp.log1p(jnp.exp(-jnp.abs(g)))
    a = jnp.where(is_f, logsig, g)
    if n_valid < L:
        a = jnp.where(rowi < n_valid, a, jnp.where(is_f, 0.0, NEG))
    tri = (lax.broadcasted_iota(I32, (L, L), 1) <= lax.broadcasted_iota(I32, (L, L), 0))
    tri_b = tri.astype(BF16)
    cum = functools.reduce(jnp.add, [_dot(tri_b, part) for part in _split3(jnp.where(is_f, a, 0.0))])
    a_t = a.T
    cum_t = cum.T

    m_all = m_ref[...]
    n_all = n_ref[...]
    m_rows, n_rows = [], []
    for h in range(N_HEADS):
        sl = slice(h * HEAD_W, (h + 1) * HEAD_W)
        b_col = cum[:, N_HEADS + h:N_HEADS + h + 1]
        b_row = cum_t[N_HEADS + h:N_HEADS + h + 1, :]
        i_col = a[:, h:h + 1]
        i_row = a_t[h:h + 1, :]
        m_prev = m_all[h:h + 1, 0:1]
        d = jnp.where(tri, (b_col - b_row) + i_row, NEG)
        inter = b_col + m_prev
        mt = jnp.maximum(inter, jnp.max(d, axis=-1, keepdims=True))
        w_intra = jnp.exp(d - mt)
        w_inter = jnp.exp(inter - mt)
        qh, kh, vh = q_all[:, sl], k_all[:, sl], v_all[:, sl]
        C_h = C_ref[h]
        n_h = n_all[h:h + 1, :]
        s = _dot_nt(qh, kh.astype(BF16)) * w_intra
        num = _dot(s.astype(BF16), vh) + w_inter * _dot(qh, C_h.astype(BF16))
        den = (jnp.sum(s, axis=-1, keepdims=True)
               + w_inter * jnp.sum(qh.astype(F32) * n_h.astype(BF16).astype(F32), axis=-1, keepdims=True))
        hh = num / jnp.maximum(jnp.abs(den), jnp.exp(-mt))
        m_last = mt[L - 1:L, :]
        b_last = b_col[L - 1:L, :]
        gk = jnp.exp((b_last - b_col) + i_col - m_last) * kh
        decay = jnp.exp(b_last + m_prev - m_last)
        C_ref[h] = decay * C_h + _dot_tn(gk.astype(BF16), vh)
        n_rows.append(decay * n_h + jnp.sum(gk, axis=0, keepdims=True))
        m_rows.append(jnp.broadcast_to(m_last, (1, LANES)))
        hn = hh * lax.rsqrt(jnp.mean(hh * hh, axis=-1, keepdims=True) + EPS) * og_ref[...]
        out = hn * jax.nn.sigmoid(o_all[:, sl])
        h_ref[:, sl] = out[:rows_in].astype(BF16)
    n_ref[...] = jnp.concatenate(n_rows, axis=0)
    m_ref[...] = jnp.concatenate(m_rows, axis=0)


def _mlstm(proj, gates, conv_w, conv_b, gate_b, og, C0, n0, m0, cv0, nb, rows_in, nchunks, n_valid):
    L = MLSTM_L
    w = N_HEADS * HEAD_W
    row = lambda cb: (lambda b, c: (b * nchunks + c, cb))
    st3 = lambda b, c: (b, 0, 0)
    kern = functools.partial(_mlstm_kernel, L=L, rows_in=rows_in, n_valid=n_valid)
    return pl.pallas_call(
        kern,
        grid=(nb, nchunks),
        in_specs=[pl.BlockSpec((rows_in, 2 * w), lambda b, c: (b * nchunks + c, 0)),
                  pl.BlockSpec((rows_in, w), row(2)), pl.BlockSpec((rows_in, w), row(3)),
                  pl.BlockSpec((rows_in, LANES), lambda b, c: (b * nchunks + c, 0)),
                  pl.BlockSpec((SUBLANES, 2 * w), lambda b, c: (0, 0)),
                  pl.BlockSpec((1, 2 * w), lambda b, c: (0, 0)),
                  pl.BlockSpec((1, LANES), lambda b, c: (0, 0)),
                  pl.BlockSpec((1, HEAD_W), lambda b, c: (0, 0)),
                  pl.BlockSpec((None, N_HEADS, HEAD_W, HEAD_W), lambda b, c: (b, 0, 0, 0)),
                  pl.BlockSpec((None, N_HEADS, HEAD_W), st3),
                  pl.BlockSpec((None, N_HEADS, LANES), st3),
                  pl.BlockSpec((None, SUBLANES, 2 * w), st3)],
        out_specs=[pl.BlockSpec((rows_in, w), lambda b, c: (b * nchunks + c, 0)),
                   pl.BlockSpec((None, N_HEADS, HEAD_W, HEAD_W), lambda b, c: (b, 0, 0, 0)),
                   pl.BlockSpec((None, N_HEADS, HEAD_W), st3),
                   pl.BlockSpec((None, N_HEADS, LANES), st3),
                   pl.BlockSpec((None, SUBLANES, 2 * w), st3)],
        out_shape=[jax.ShapeDtypeStruct((nb * nchunks * rows_in, w), BF16),
                   jax.ShapeDtypeStruct((nb, N_HEADS, HEAD_W, HEAD_W), F32),
                   jax.ShapeDtypeStruct((nb, N_HEADS, HEAD_W), F32),
                   jax.ShapeDtypeStruct((nb, N_HEADS, LANES), F32),
                   jax.ShapeDtypeStruct((nb, SUBLANES, 2 * w), F32)],
        scratch_shapes=[pltpu.VMEM((SUBLANES + L, 2 * w), F32)],
        compiler_params=_cparams(("arbitrary", "arbitrary"), 40 * 2**20),
        name="mlstm_chunkwise",
    )(proj, proj, proj, gates, conv_w, conv_b, gate_b, og, C0, n0, m0, cv0)


def _outproj_kernel(hm_ref, ha_ref, x_ref, g1_ref, sc_ref, sh_ref, ng_ref, wo_ref, wr_ref, br_ref,
                    x1_ref, hn_ref, ids_ref, gates_ref):
    half = hm_ref.shape[1]
    mix = _dot(hm_ref[...], wo_ref[0:half, :]) + _dot(ha_ref[...], wo_ref[half:2 * half, :])
    x1 = x_ref[...] + g1_ref[...] * mix
    x1_ref[...] = x1
    y = x1 * lax.rsqrt(jnp.mean(x1 * x1, axis=-1, keepdims=True) + EPS) * ng_ref[...]
    hn = y * (1.0 + sc_ref[...]) + sh_ref[...]
    hn_ref[...] = _pack_bf16_pairs(hn)
    logits = _dot(hn.astype(BF16), wr_ref[...]) + br_ref[...]

    lane = lax.broadcasted_iota(I32, logits.shape, 1).astype(F32)
    big = 1000.0
    is_g = lane < N_GROUPS
    gl = jnp.where(is_g, logits, NEG)
    gmax = jnp.max(gl, axis=-1, keepdims=True)
    g_idx = jnp.min(jnp.where(gl == gmax, lane, big), axis=-1, keepdims=True)
    g_p = 1.0 / jnp.sum(jnp.where(is_g, jnp.exp(gl - gmax), 0.0), axis=-1, keepdims=True)
    e_lo = N_GROUPS + EXPERTS_PER_GROUP * g_idx
    in_grp = (lane >= e_lo) & (lane < e_lo + EXPERTS_PER_GROUP)
    el = jnp.where(in_grp, logits, NEG)
    ex = jnp.where(in_grp, jnp.exp(el - jnp.max(el, axis=-1, keepdims=True)), 0.0)
    p = jnp.where(in_grp, ex / jnp.sum(ex, axis=-1, keepdims=True), -1.0)
    top1 = jnp.max(p, axis=-1, keepdims=True)
    idx1 = jnp.min(jnp.where(p == top1, lane, big), axis=-1, keepdims=True)
    p2 = jnp.where(lane == idx1, -1.0, p)
    top2 = jnp.max(p2, axis=-1, keepdims=True)
    idx2 = jnp.min(jnp.where(p2 == top2, lane, big), axis=-1, keepdims=True)
    tsum = top1 + top2
    ids_ref[...] = jnp.where(lane == 0.0, idx1 - N_GROUPS,
                             jnp.where(lane == 1.0, idx2 - N_GROUPS, -1.0)).astype(I32)
    gates_ref[...] = jnp.where(lane == 0.0, g_p * top1 / tsum, jnp.where(lane == 1.0, g_p * top2 / tsum, 0.0))


def _outproj(hm, ha, x, mod_specs, mods, ng, wo, wr, br, tm):
    m, d = x.shape
    half = hm.shape[1]
    row = lambda wdt: pl.BlockSpec((tm, wdt), lambda i: (i, 0))
    const = lambda shp: pl.BlockSpec(shp, lambda i: (0, 0))
    return pl.pallas_call(
        _outproj_kernel,
        grid=(m // tm,),
        in_specs=[row(half), row(half), row(d), *mod_specs, const((1, d)), const((d, d)),
                  const((d, LANES)), const((1, LANES))],
        out_specs=[row(d), row(d // 2), row(LANES), row(LANES)],
        out_shape=[jax.ShapeDtypeStruct((m, d), F32), jax.ShapeDtypeStruct((m, d // 2), jnp.uint32),
                   jax.ShapeDtypeStruct((m, LANES), I32), jax.ShapeDtypeStruct((m, LANES), F32)],
        compiler_params=_cparams(("arbitrary",), 2 * d * d * 2 + 8 * tm * d * 4 + 8 * 2**20),
        name="outproj_norm2_route",
    )(hm, ha, x, *mods, ng.reshape(1, d), wo, wr, br)


def _rank_kernel(ids_ref, dest_ref, cnt_ref, carry, pstart):
    pss = pl.program_id(0)
    i = pl.program_id(1)
    ids = ids_ref[...]
    tm = ids.shape[0]
    lane = lax.broadcasted_iota(I32, ids.shape, 1)
    o0 = lane == ids[:, 0:1]
    o1 = lane == ids[:, 1:2]
    onehot = jnp.where(o0 | o1, 1.0, 0.0)
    col_counts = jnp.sum(onehot, axis=0, keepdims=True)

    @pl.when((pss == 0) & (i == 0))
    def _():
        carry[...] = jnp.zeros(carry.shape, F32)

    @pl.when(pss == 0)
    def _():
        carry[...] = carry[...] + col_counts

    @pl.when((pss == 1) & (i == 0))
    def _():
        cnt = carry[...]
        cnt_ref[...] = cnt
        blocks = jnp.floor((cnt + (MOE_ROWS - 1)) * (1.0 / MOE_ROWS))
        earlier = lax.broadcasted_iota(I32, (LANES, LANES), 0) < lax.broadcasted_iota(I32, (LANES, LANES), 1)
        pstart[...] = _dot(blocks.astype(BF16), earlier.astype(BF16)) * MOE_ROWS
        carry[...] = jnp.zeros(carry.shape, F32)

    @pl.when(pss == 1)
    def _():
        strict = lax.broadcasted_iota(I32, (tm, tm), 1) < lax.broadcasted_iota(I32, (tm, tm), 0)
        before = _dot(strict.astype(BF16), onehot.astype(BF16)) + carry[0:1, :] + pstart[0:1, :]
        r0 = jnp.sum(jnp.where(o0, before, 0.0), axis=-1, keepdims=True)
        r1 = jnp.sum(jnp.where(o1, before, 0.0), axis=-1, keepdims=True)
        dest_ref[...] = jnp.where(lane == 0, r0, jnp.where(lane == 1, r1, 0.0)).astype(I32)
        carry[...] = carry[...] + col_counts


def _ranks(ids, tm):
    m = ids.shape[0]
    return pl.pallas_call(
        _rank_kernel,
        grid=(2, m // tm),
        in_specs=[pl.BlockSpec((tm, LANES), lambda p, i: (i, 0))],
        out_specs=[pl.BlockSpec((tm, LANES), lambda p, i: (i * p, 0)),
                   pl.BlockSpec((SUBLANES, LANES), lambda p, i: (0, 0))],
        out_shape=[jax.ShapeDtypeStruct((m, LANES), I32), jax.ShapeDtypeStruct((SUBLANES, LANES), F32)],
        scratch_shapes=[pltpu.VMEM((SUBLANES, LANES), F32), pltpu.VMEM((SUBLANES, LANES), F32)],
        compiler_params=_cparams(("arbitrary", "arbitrary"), 16 * 2**20),
        name="moe_ranks",
    )(ids)


def _row_copies(idx_ref, t0, tm, copy):
    def body(t, _):
        for k in range(2):
            copy(t, k, idx_ref[2 * (t0 + t) + k]).start()
        return 0
    lax.fori_loop(0, tm, body, 0, unroll=8)


def _dispatch_kernel(dest_ref, h_ref, xin_ref, xs_ref, sem, *, tm, base):
    del xin_ref
    t0 = base + pl.program_id(0) * tm
    _row_copies(dest_ref, t0, tm,
                lambda t, k, d: pltpu.make_async_copy(h_ref.at[pl.ds(t, 1)], xs_ref.at[pl.ds(d, 1)], sem))
    for _ in range(2):
        pltpu.make_async_copy(h_ref, xs_ref.at[pl.ds(0, tm)], sem).wait()


def _dispatch(dest_flat, hn, xs, tm, base):
    m, d = hn.shape
    any_spec = pl.BlockSpec(memory_space=pl.ANY)
    grid_spec = pltpu.PrefetchScalarGridSpec(
        num_scalar_prefetch=1, grid=(m // tm,),
        in_specs=[pl.BlockSpec((tm, d), lambda i, dr: (i, 0)), any_spec], out_specs=any_spec,
        scratch_shapes=[pltpu.SemaphoreType.DMA])
    return pl.pallas_call(
        functools.partial(_dispatch_kernel, tm=tm, base=base),
        grid_spec=grid_spec,
        out_shape=jax.ShapeDtypeStruct(xs.shape, xs.dtype),
        input_output_aliases={2: 0},
        compiler_params=_cparams(("arbitrary",), 4 * tm * d * 4 + 8 * 2**20),
        name="moe_dispatch",
    )(dest_flat, hn, xs)


def _expert_weights(b, be_ref, bf_ref, nx_ref, w_hbm, w_stage, w_bf16, sem):
    def copies(e):
        return [pltpu.make_async_copy(src.at[0, e], dst, sem.at[i])
                for i, (src, dst) in enumerate(zip(w_hbm, w_stage))]

    @pl.when(b == 0)
    def _():
        for c in copies(be_ref[0]):
            c.start()

    @pl.when(bf_ref[b] == 1)
    def _():
        for c in copies(be_ref[b]):
            c.wait()
        for stage, wb in zip(w_stage, w_bf16):
            wb[...] = stage[...].astype(BF16)

        @pl.when(nx_ref[b] >= 0)
        def _():
            for c in copies(nx_ref[b]):
                c.start()


def _block_rows(br_ref, b, out_ref, compute):
    nrows = br_ref[b]
    half = out_ref.shape[0] // 2
    top, bottom = pl.ds(0, half), pl.ds(half, half)

    @pl.when(nrows > half)
    def _():
        out_ref[...] = compute(pl.ds(0, 2 * half))

    @pl.when(jnp.logical_and(nrows > 0, nrows <= half))
    def _():
        out_ref[top, :] = compute(top)
        out_ref[bottom, :] = jnp.zeros((half, out_ref.shape[1]), out_ref.dtype)

    @pl.when(nrows == 0)
    def _():
        out_ref[...] = jnp.zeros(out_ref.shape, out_ref.dtype)


def _expert_up_kernel(be_ref, bf_ref, nx_ref, br_ref, x_ref, w1_hbm, w3_hbm, h_ref, w1f, w3f, w1b, w3b, sem):
    b = pl.program_id(0)
    _expert_weights(b, be_ref, bf_ref, nx_ref, (w1_hbm, w3_hbm), (w1f, w3f), (w1b, w3b), sem)

    def swiglu(rows):
        xl, xr = _unpack_bf16_pairs(x_ref[rows, :])
        half = xl.shape[1]
        up = lambda wb: _dot(xl, wb[0:half, :]) + _dot(xr, wb[half:2 * half, :])
        return (_silu(up(w1b)) * up(w3b)).astype(BF16)

    _block_rows(br_ref, b, h_ref, swiglu)


def _expert_up(tabs, xs, w1, w3):
    rows, dp = xs.shape
    d, f = w1.shape[-2:]
    assert d == 2 * dp
    nb = rows // MOE_ROWS
    any_spec = pl.BlockSpec(memory_space=pl.ANY)
    grid_spec = pltpu.PrefetchScalarGridSpec(
        num_scalar_prefetch=4, grid=(nb,),
        in_specs=[pl.BlockSpec((MOE_ROWS, dp), lambda b, *_: (b, 0)), any_spec, any_spec],
        out_specs=pl.BlockSpec((MOE_ROWS, f), lambda b, *_: (b, 0)),
        scratch_shapes=[pltpu.VMEM((d, f), F32), pltpu.VMEM((d, f), F32),
                        pltpu.VMEM((d, f), BF16), pltpu.VMEM((d, f), BF16), pltpu.SemaphoreType.DMA((2,))])
    return pl.pallas_call(
        _expert_up_kernel,
        grid_spec=grid_spec,
        out_shape=jax.ShapeDtypeStruct((rows, f), BF16),
        compiler_params=_cparams(("arbitrary",), 2 * d * f * 6 + 4 * MOE_ROWS * d * 4 + 8 * 2**20),
        name="moe_expert_up",
    )(*tabs, xs, w1, w3)


def _expert_down_kernel(be_ref, bf_ref, nx_ref, br_ref, h_ref, w2_hbm, y_ref, w2f, w2b, sem):
    b = pl.program_id(0)
    _expert_weights(b, be_ref, bf_ref, nx_ref, (w2_hbm,), (w2f,), (w2b,), sem)
    _block_rows(br_ref, b, y_ref, lambda rows: _dot(h_ref[rows, :], w2b[...]))


def _expert_down(tabs, hs, w2):
    rows, f = hs.shape
    d = w2.shape[-1]
    nb = rows // MOE_ROWS
    grid_spec = pltpu.PrefetchScalarGridSpec(
        num_scalar_prefetch=4, grid=(nb,),
        in_specs=[pl.BlockSpec((MOE_ROWS, f), lambda b, *_: (b, 0)), pl.BlockSpec(memory_space=pl.ANY)],
        out_specs=pl.BlockSpec((MOE_ROWS, d), lambda b, *_: (b, 0)),
        scratch_shapes=[pltpu.VMEM((f, d), F32), pltpu.VMEM((f, d), BF16), pltpu.SemaphoreType.DMA((1,))])
    return pl.pallas_call(
        _expert_down_kernel,
        grid_spec=grid_spec,
        out_shape=jax.ShapeDtypeStruct((rows, d), F32),
        compiler_params=_cparams(("arbitrary",), f * d * 6 + 6 * MOE_ROWS * d * 4 + 8 * 2**20),
        name="moe_expert_down",
    )(*tabs, hs, w2)


def _combine_kernel(dest_ref, x1_ref, g2_ref, gates_ref, y_ref, o_ref, ybuf, sem, *, tm, base):
    i = pl.program_id(0)
    slot = i % 2

    def gather(step, s):
        _row_copies(dest_ref, base + step * tm, tm,
                    lambda t, k, d: pltpu.make_async_copy(y_ref.at[pl.ds(d, 1)], ybuf.at[s, k, pl.ds(t, 1)],
                                                          sem.at[s]))

    @pl.when(i == 0)
    def _():
        gather(0, 0)

    @pl.when(i + 1 < pl.num_programs(0))
    def _():
        gather(i + 1, 1 - slot)

    for k in range(2):
        pltpu.make_async_copy(y_ref.at[pl.ds(0, tm)], ybuf.at[slot, k], sem.at[slot]).wait()
    gates = gates_ref[...]
    ff = gates[:, 0:1] * ybuf[slot, 0] + gates[:, 1:2] * ybuf[slot, 1]
    o_ref[...] = x1_ref[...] + g2_ref[...] * ff


def _combine(dest_flat, x1, g2_spec, g2, gates, y, tm, base):
    m, d = x1.shape
    grid_spec = pltpu.PrefetchScalarGridSpec(
        num_scalar_prefetch=1, grid=(m // tm,),
        in_specs=[pl.BlockSpec((tm, d), lambda i, dr: (i, 0)), g2_spec,
                  pl.BlockSpec((tm, LANES), lambda i, dr: (i, 0)),
                  pl.BlockSpec(memory_space=pl.ANY)],
        out_specs=pl.BlockSpec((tm, d), lambda i, dr: (i, 0)),
        scratch_shapes=[pltpu.VMEM((2, 2, tm, d), F32), pltpu.SemaphoreType.DMA((2,))])
    return pl.pallas_call(
        functools.partial(_combine_kernel, tm=tm, base=base),
        grid_spec=grid_spec,
        out_shape=jax.ShapeDtypeStruct((m, d), F32),
        compiler_params=_cparams(("arbitrary",), 8 * tm * d * 4 + 8 * 2**20),
        name="moe_combine",
    )(dest_flat, x1, g2, gates, y)


def kernel(x_prompt, x_sample, cache_k, cache_v, page_table, state_mlstm_C, state_mlstm_n, state_mlstm_m, state_conv,
           c_prompt, c_sample, w_ada, b_ada, norm1_g, norm2_g, w_in, conv_w, conv_b, b_igate, b_fgate, mlstm_out_g,
           q_norm_g, k_norm_g, lam_q1, lam_k1, lam_q2, lam_k2, diff_out_g, w_out, w_grp, b_grp, w_rt, b_rt, w1, w3, w2):
    B, T, D = x_prompt.shape
    DB, TS, _ = x_sample.shape
    depth = w_ada.shape[0]
    assert depth == 1, "single-layer trunk"
    n_pages, page = page_table.shape[1], cache_k.shape[2]
    past = n_pages * page
    W = N_HEADS * HEAD_W
    lam_init = 0.8 - 0.6 * math.exp(-0.3 * 0)
    NP, NS = B * T, DB * TS
    SP = SAMPLE_PAD

    c_all = jnp.concatenate([c_prompt, c_sample, jnp.zeros((16 - B - DB, D), F32)], axis=0)
    w_main, w_gate = _regroup_w_in(w_in[0].T, 4 * W, 2 * N_HEADS)
    gate_b = jnp.pad(jnp.concatenate([b_igate[0], b_fgate[0]]), (0, LANES - 2 * N_HEADS)).reshape(1, LANES)
    cw = jnp.pad(conv_w[0], ((0, SUBLANES - CONV_W), (0, 0)))
    cb = conv_b[0].reshape(1, 2 * W)
    og_m = mlstm_out_g[0].reshape(1, HEAD_W)
    wo = w_out[0].astype(BF16)
    w_route = jnp.concatenate([w_grp[0], w_rt[0].transpose(1, 0, 2).reshape(D, N_EXPERTS)], axis=1)
    w_route = jnp.pad(w_route, ((0, 0), (0, LANES - N_GROUPS - N_EXPERTS))).astype(BF16)
    b_route = jnp.pad(jnp.concatenate([b_grp[0], b_rt[0].reshape(-1)]), (0, LANES - N_GROUPS - N_EXPERTS)).reshape(1, LANES)
    lams = (lam_q1[0], lam_k1[0], lam_q2[0], lam_k2[0])

    mod = _ada(c_all, w_ada[0], b_ada[0])
    mod3 = mod.reshape(16, 1, 6 * D)
    mod_s = jnp.repeat(mod[B:B + DB], SP, axis=0)

    def p_mod(chunk, tm):
        return pl.BlockSpec((None, 1, D), lambda i, *_: ((i * tm) // T, 0, chunk))

    def s_mod(chunk, tm):
        return pl.BlockSpec((tm, D), lambda i, *_: (i, chunk))

    xp = x_prompt.reshape(NP, D)
    tm_in = 1024
    proj_p, gates_p = _inproj(xp, (p_mod(1, tm_in), p_mod(0, tm_in)), (mod3, mod3), norm1_g[0], w_main, w_gate, tm_in)
    cos_p, sin_p = _rope_tables(jnp.arange(T, dtype=I32))
    knew_p, kb_p, vnew_p, qt_p, vt_p = _qkrope(proj_p, 4, cos_p, sin_p, q_norm_g[0], k_norm_g[0], 256, B)
    ha_p = _dattn_prompt(qt_p, kb_p, vt_p, lams, diff_out_g[0], lam_init)
    zeros = lambda *s: jnp.zeros(s, F32)
    hm_p, C_p, n_p, m_p, cv_p = _mlstm(
        proj_p, gates_p, cw, cb, gate_b, og_m,
        zeros(B, N_HEADS, HEAD_W, HEAD_W), zeros(B, N_HEADS, HEAD_W), zeros(B, N_HEADS, LANES),
        zeros(B, SUBLANES, 2 * W), B, MLSTM_L, T // MLSTM_L, MLSTM_L)
    tm_o = 256
    x1_p, hn2_p, ids_p, gts_p = _outproj(
        hm_p, ha_p, xp, (p_mod(2, tm_o), p_mod(4, tm_o), p_mod(3, tm_o)), (mod3, mod3, mod3),
        norm2_g[0], wo, w_route, b_route, tm_o)

    MS = DB * SP
    xs_pad = jnp.pad(x_sample, ((0, 0), (0, SP - TS), (0, 0))).reshape(MS, D)
    proj_s, gates_s = _inproj(xs_pad, (s_mod(1, MS), s_mod(0, MS)), (mod_s, mod_s), norm1_g[0], w_main, w_gate, MS)
    pos_s = past + (jnp.arange(MS, dtype=I32) % SP)
    cos_s, sin_s = _rope_tables(pos_s)
    knew_s, kb_s, vnew_s, q_s, vb_s = _qkrope(proj_s, 4, cos_s, sin_s, q_norm_g[0], k_norm_g[0], MS)
    qr = q_s.reshape(DB, SP, N_HEADS, 2, DQK)[:, :TS].transpose(0, 2, 3, 1, 4)
    zq = jnp.zeros_like(qr[:, :, 0])
    qall = jnp.stack([jnp.concatenate([qr[:, :, 0], zq], axis=-1),
                      jnp.concatenate([zq, qr[:, :, 1]], axis=-1)], axis=2).reshape(DB, N_HEADS * 2 * TS, HEAD_W)
    new_rows = page * N_HEADS
    pad_new = lambda a: jnp.pad(a.reshape(DB, SP * N_HEADS, HEAD_W), ((0, 0), (0, new_rows - SP * N_HEADS), (0, 0)))
    ha_s = _dattn_paged(qall, cache_k[0].reshape(-1, new_rows, HEAD_W), cache_v[0].reshape(-1, new_rows, HEAD_W),
                        page_table, pad_new(kb_s), pad_new(vb_s), lams, diff_out_g[0], TS, lam_init)
    m0_s = jnp.broadcast_to(state_mlstm_m[0][:, :, None], (DB, N_HEADS, LANES))
    cv0_s = jnp.pad(state_conv[0], ((0, 0), (SUBLANES - (CONV_W - 1), 0), (0, 0)))
    hm_s, C_s, n_s, m_s, cv_s = _mlstm(proj_s, gates_s, cw, cb, gate_b, og_m, state_mlstm_C[0], state_mlstm_n[0],
                                       m0_s, cv0_s, DB, SP, 1, TS)
    x1_s, hn2_s, ids_s, gts_s = _outproj(
        hm_s, ha_s, xs_pad, (s_mod(2, MS), s_mod(4, MS), s_mod(3, MS)), (mod_s, mod_s, mod_s),
        norm2_g[0], wo, w_route, b_route, MS)
    real = lambda a: a.reshape(DB, SP, -1)[:, :TS].reshape(NS, -1)
    x1_s, hn2_s, ids_s, gts_s = real(x1_s), real(hn2_s), real(ids_s), real(gts_s)

    NT = NP + NS
    tm_r = 256
    n_rank = -(-NT // tm_r) * tm_r
    ids_all = jnp.concatenate([ids_p, ids_s, jnp.full((n_rank - NT, LANES), -1, I32)], axis=0)
    dest2, cnt = _ranks(ids_all, tm_r)
    dest = dest2[:NT, :2].reshape(-1)
    counts = cnt[0, :N_EXPERTS].astype(I32)
    e_blocks = (counts + MOE_ROWS - 1) // MOE_ROWS
    pend = jnp.cumsum(e_blocks)
    n_blocks = (2 * NT + N_EXPERTS * (MOE_ROWS - 1)) // MOE_ROWS
    n_valid = pend[-1:]
    blk_id = jnp.arange(n_blocks, dtype=I32)
    blk = jnp.minimum(blk_id, n_valid[0] - 1)
    blk_e = jnp.sum((blk[:, None] >= pend[None, :]).astype(I32), axis=1)
    in_expert = blk_id - (pend - e_blocks)[blk_e]
    blk_rows = jnp.where(blk_id < n_valid[0], jnp.clip(counts[blk_e] - in_expert * MOE_ROWS, 0, MOE_ROWS), 0).astype(I32)
    blk_first = jnp.concatenate([jnp.ones((1,), I32), (blk_e[1:] != blk_e[:-1]).astype(I32)])
    first_pos = jnp.where(blk_first == 1, jnp.arange(n_blocks, dtype=I32), n_blocks)
    next_first = jnp.concatenate([lax.cummin(first_pos, reverse=True)[1:], jnp.full((1,), n_blocks, I32)])
    blk_next = jnp.where(next_first < n_blocks, blk_e[jnp.minimum(next_first, n_blocks - 1)], -1).astype(I32)
    tabs = (blk_e, blk_first, blk_next, blk_rows)
    xs_sorted = _dispatch(dest, hn2_p, jnp.zeros((n_blocks * MOE_ROWS, D // 2), jnp.uint32), 256, 0)
    xs_sorted = _dispatch(dest, hn2_s, xs_sorted, NS, NP)
    h_sorted = _expert_up(tabs, xs_sorted, w1, w3)
    y_sorted = _expert_down(tabs, h_sorted, w2)
    tm_c = 128
    y_p = _combine(dest, x1_p, pl.BlockSpec((None, 1, D), lambda i, dr: ((i * tm_c) // T, 0, 5)), mod3, gts_p,
                   y_sorted, tm_c, 0)
    g2_s = jnp.repeat(mod[B:B + DB, 5 * D:], TS, axis=0)
    y_s = _combine(dest, x1_s, pl.BlockSpec((NS, D), lambda i, dr: (0, 0)), g2_s, gts_s, y_sorted, NS, NP)

    st = lambda a: a[None]
    k5 = lambda a, nb, t: a.reshape(1, nb, t, N_HEADS, HEAD_W)
    tok_s = lambda a: a.reshape(DB, SP, -1)[:, :TS]
    conv_out = lambda cv: cv[None, :, SUBLANES - (CONV_W - 1):, :]
    return (y_p.reshape(B, T, D), y_s.reshape(DB, TS, D),
            k5(knew_p, B, T), k5(vnew_p, B, T), st(C_p), st(n_p), st(m_p[:, :, 0]), conv_out(cv_p),
            k5(tok_s(knew_s), DB, TS), k5(tok_s(vnew_s), DB, TS), st(C_s), st(n_s), st(m_s[:, :, 0]), conv_out(cv_s))
```

```python
import functools
import math

import jax
import jax.numpy as jnp
from jax import lax
from jax.experimental import pallas as pl
from jax.experimental.pallas import tpu as pltpu

F32 = jnp.float32
BF16 = jnp.bfloat16
I32 = jnp.int32

EPS = 1e-6
ROPE_THETA = 10000.0
NEG = -1e30

LANES = 128
SUBLANES = 8
VMEM_LIMIT_CAP = 56 * 1024 * 1024

N_HEADS = 8
HEAD_W = 128
DQK = 64
N_GROUPS = 4
EXPERTS_PER_GROUP = 8
N_EXPERTS = N_GROUPS * EXPERTS_PER_GROUP
CONV_W = 4
MOE_ROWS = 256
SAMPLE_PAD = 8
MLSTM_L = 128
PAGES_PER_STEP = 16


def _cparams(sem, vmem_bytes):
    return pltpu.CompilerParams(dimension_semantics=sem,
                                vmem_limit_bytes=int(min(max(vmem_bytes, 16 * 2**20), VMEM_LIMIT_CAP)))


def _silu(x):
    return x * jax.nn.sigmoid(x)


def _dot(a, b):
    return jnp.dot(a, b, preferred_element_type=F32)


def _pack_bf16_pairs(x):
    n = x.shape[1] // 2
    bits = lax.bitcast_convert_type(x.astype(BF16).astype(F32), jnp.uint32)
    return bits[:, :n] | (bits[:, n:] >> 16)


def _unpack_bf16_pairs(w):
    left = lax.bitcast_convert_type(w & jnp.uint32(0xFFFF0000), F32)
    right = lax.bitcast_convert_type(w << 16, F32)
    return left.astype(BF16), right.astype(BF16)


def _dot_nt(a, b):
    return lax.dot_general(a, b, (((1,), (1,)), ((), ())), preferred_element_type=F32)


def _dot_tn(a, b):
    return lax.dot_general(a, b, (((0,), (0,)), ((), ())), preferred_element_type=F32)


def _ada_kernel(c_ref, w_ref, b_ref, o_ref):
    s = _silu(c_ref[...]).astype(BF16)
    o_ref[...] = _dot(s, w_ref[...].astype(BF16)) + b_ref[...]


def _ada(c_all, w_ada, b_ada):
    rows, d = c_all.shape
    n = w_ada.shape[1]
    tn = 1024
    return pl.pallas_call(
        _ada_kernel,
        grid=(n // tn,),
        in_specs=[pl.BlockSpec((rows, d), lambda j: (0, 0)),
                  pl.BlockSpec((d, tn), lambda j: (0, j)),
                  pl.BlockSpec((1, tn), lambda j: (0, j))],
        out_specs=pl.BlockSpec((rows, tn), lambda j: (0, j)),
        out_shape=jax.ShapeDtypeStruct((rows, n), F32),
        compiler_params=_cparams(("arbitrary",), 2 * d * tn * 4 + 3 * d * tn * 2 + 2**22),
        name="ada_mod",
    )(c_all, w_ada, b_ada.reshape(1, n))


def _regroup_kernel(w_hbm, main_ref, gate_ref, buf, gbuf, sem, *, lo, ng, tr):
    j = pl.program_id(0)
    slot = j % 2

    def block_copy(jj, s):
        start = pl.multiple_of(jnp.where(jj * tr < lo, jj * tr, jj * tr + ng), SUBLANES)
        return pltpu.make_async_copy(w_hbm.at[pl.ds(start, tr)], buf.at[s], sem.at[s])

    gate_copy = pltpu.make_async_copy(w_hbm.at[pl.ds(lo, ng)], gbuf, sem.at[2])

    @pl.when(j == 0)
    def _():
        block_copy(0, 0).start()
        gate_copy.start()

    @pl.when(j + 1 < pl.num_programs(0))
    def _():
        block_copy(j + 1, 1 - slot).start()

    block_copy(j, slot).wait()
    main_ref[...] = buf[slot].astype(BF16)

    @pl.when(j == 0)
    def _():
        gate_copy.wait()
        pad = jnp.zeros((LANES - ng, gbuf.shape[1]), F32)
        gate_ref[...] = jnp.concatenate([gbuf[...], pad], axis=0).astype(BF16)


def _regroup_w_in(w_t, lo, ng):
    n, d = w_t.shape
    tr = 1024
    assert lo % tr == 0 and (n - ng) % tr == 0 and ng % SUBLANES == 0
    return pl.pallas_call(
        functools.partial(_regroup_kernel, lo=lo, ng=ng, tr=tr),
        grid=((n - ng) // tr,),
        in_specs=[pl.BlockSpec(memory_space=pl.ANY)],
        out_specs=[pl.BlockSpec((tr, d), lambda j: (j, 0)), pl.BlockSpec((LANES, d), lambda j: (0, 0))],
        out_shape=[jax.ShapeDtypeStruct((n - ng, d), BF16), jax.ShapeDtypeStruct((LANES, d), BF16)],
        scratch_shapes=[pltpu.VMEM((2, tr, d), F32), pltpu.VMEM((ng, d), F32), pltpu.SemaphoreType.DMA((3,))],
        compiler_params=_cparams(("arbitrary",), 2 * tr * d * 4 + 4 * tr * d * 2 + 8 * 2**20),
        name="w_in_layout",
    )(w_t)


def _inproj_kernel(x_ref, sc_ref, sh_ref, g_ref, w_ref, wg_ref, o_ref, og_ref, hn_ref):
    @pl.when(pl.program_id(1) == 0)
    def _():
        rows = min(x_ref.shape[0], 256)

        def chunk(c, _):
            r = pl.ds(pl.multiple_of(c * rows, rows), rows)
            x = x_ref[r, :]
            y = x * lax.rsqrt(jnp.mean(x * x, axis=-1, keepdims=True) + EPS) * g_ref[...]
            sc = sc_ref[...] if sc_ref.shape[0] == 1 else sc_ref[r, :]
            sh = sh_ref[...] if sh_ref.shape[0] == 1 else sh_ref[r, :]
            hb = (y * (1.0 + sc) + sh).astype(BF16)
            hn_ref[r, :] = hb
            og_ref[r, :] = _dot_nt(hb, wg_ref[...])
            return 0
        lax.fori_loop(0, x_ref.shape[0] // rows, chunk, 0)

    o_ref[...] = _dot_nt(hn_ref[...], w_ref[...])


def _inproj(x, mod_specs, mods, g, w, wg, tm):
    m, d = x.shape
    n = w.shape[0]
    tn = 1024
    sc_spec, sh_spec = mod_specs
    return pl.pallas_call(
        _inproj_kernel,
        grid=(m // tm, n // tn),
        in_specs=[pl.BlockSpec((tm, d), lambda i, j: (i, 0)), sc_spec, sh_spec,
                  pl.BlockSpec((1, d), lambda i, j: (0, 0)),
                  pl.BlockSpec((tn, d), lambda i, j: (j, 0)),
                  pl.BlockSpec((LANES, d), lambda i, j: (0, 0))],
        out_specs=[pl.BlockSpec((tm, tn), lambda i, j: (i, j)),
                   pl.BlockSpec((tm, LANES), lambda i, j: (i, 0))],
        out_shape=[jax.ShapeDtypeStruct((m, n), F32), jax.ShapeDtypeStruct((m, LANES), F32)],
        scratch_shapes=[pltpu.VMEM((tm, d), BF16)],
        compiler_params=_cparams(("arbitrary", "arbitrary"),
                                 2 * tm * d * 4 + 2 * d * tn * 2 + 2 * tm * tn * 4 + tm * d * 2 + 12 * 2**20),
        name="norm1_inproj",
    )(x, mods[0], mods[1], g.reshape(1, d), w, wg)


def _qkrope_kernel(q_ref, k_ref, v_ref, cos_ref, sin_ref, qg_ref, kg_ref, ko_ref, kb_ref, vo_ref, qx_ref, vx_ref,
                   *, transposed):
    cos = cos_ref[...]
    sin = sin_ref[...]
    lane = lax.broadcasted_iota(I32, (1, LANES), 1)
    first_map = lane < DQK
    first_half = (lane % DQK) < (DQK // 2)

    group_mean = jnp.where(lax.broadcasted_iota(I32, (LANES, LANES), 0) // DQK
                           == lax.broadcasted_iota(I32, (LANES, LANES), 1) // DQK, 1.0 / DQK, 0.0).astype(BF16)

    def norm_rope(x, g):
        x2 = x * x
        if transposed:
            ms = functools.reduce(jnp.add, [_dot(part, group_mean) for part in _split3(x2)])
            inv = lax.rsqrt(ms + EPS)
        else:
            s1 = jnp.sum(jnp.where(first_map, x2, 0.0), axis=-1, keepdims=True)
            s2 = jnp.sum(jnp.where(first_map, 0.0, x2), axis=-1, keepdims=True)
            inv = jnp.where(first_map, lax.rsqrt(s1 / DQK + EPS), lax.rsqrt(s2 / DQK + EPS))
        y = x * inv * g
        partner = jnp.where(first_half, pltpu.roll(y, LANES - DQK // 2, 1), pltpu.roll(y, DQK // 2, 1))
        return y * cos + partner * sin

    for h in range(N_HEADS):
        sl = slice(h * HEAD_W, (h + 1) * HEAD_W)
        qh = norm_rope(q_ref[:, sl], qg_ref[...]) * (DQK ** -0.5)
        kh = norm_rope(k_ref[:, sl], kg_ref[...])
        vh = v_ref[:, sl]
        ko_ref[:, sl] = kh
        kb_ref[:, sl] = kh.astype(BF16)
        vo_ref[:, sl] = vh
        if transposed:
            qx_ref[sl, :] = qh.T.astype(BF16)
            vx_ref[sl, :] = vh.T.astype(BF16)
        else:
            qx_ref[:, sl] = qh.astype(BF16)
            vx_ref[:, sl] = vh.astype(BF16)


def _qkrope(proj, col0, cos_t, sin_t, qg, kg, tm, transposed_batches=0):
    m = proj.shape[0]
    w = N_HEADS * HEAD_W
    nt = cos_t.shape[0] // tm
    blk = lambda c: pl.BlockSpec((tm, w), lambda i, c=c: (i, c))
    tab = pl.BlockSpec((tm, LANES), lambda i: (i % nt, 0))
    gsp = pl.BlockSpec((1, LANES), lambda i: (0, 0))
    osp = pl.BlockSpec((tm, w), lambda i: (i, 0))
    g2 = lambda g: jnp.tile(g, 2).reshape(1, LANES)
    out_specs = [osp] * 3
    out_shape = [jax.ShapeDtypeStruct((m, w), F32), jax.ShapeDtypeStruct((m, w), BF16),
                 jax.ShapeDtypeStruct((m, w), F32)]
    if transposed_batches:
        out_specs += [pl.BlockSpec((None, None, w, tm), lambda i: (i // nt, i % nt, 0, 0))] * 2
        out_shape += [jax.ShapeDtypeStruct((transposed_batches, nt, w, tm), BF16)] * 2
    else:
        out_specs += [osp] * 2
        out_shape += [jax.ShapeDtypeStruct((m, w), BF16)] * 2
    return pl.pallas_call(
        functools.partial(_qkrope_kernel, transposed=bool(transposed_batches)),
        grid=(m // tm,),
        in_specs=[blk(col0), blk(col0 + 1), blk(col0 + 2), tab, tab, gsp, gsp],
        out_specs=out_specs,
        out_shape=out_shape,
        compiler_params=_cparams(("arbitrary",), 2 * tm * w * (3 * 4 + 2 * 4 + 5 * 2) + 2**22),
        name="qknorm_rope",
    )(proj, proj, proj, cos_t, sin_t, g2(qg), g2(kg))


def _rope_tables(pos):
    half = DQK // 2
    inv = ROPE_THETA ** (-jnp.arange(half, dtype=F32) / half)
    ang = pos.astype(F32)[:, None] * inv[None, :]
    cos = jnp.tile(jnp.cos(ang), (1, LANES // half))
    sin = jnp.sin(ang)
    sin = jnp.tile(jnp.concatenate([-sin, sin], axis=1), (1, LANES // DQK))
    return cos, sin


def _lam_value(l1, l2, l3, l4, lam_init):
    a = jnp.sum(l1[...] * l2[...], axis=-1, keepdims=True)
    b = jnp.sum(l3[...] * l4[...], axis=-1, keepdims=True)
    return jnp.exp(a) - jnp.exp(b) + lam_init


def _dattn_kernel(qt_ref, k_ref, vt_ref, l1_ref, l2_ref, l3_ref, l4_ref, og_ref, o_ref,
                  s_a, s_b, p_a, p_b, cm_ref, stat_ref, acc_ref, *, tq, lam_init):
    qi = pl.program_id(2)
    tk = tq // 2
    s_bufs, p_bufs = (s_a, s_b), (p_a, p_b)
    qt = jnp.concatenate([qt_ref[0], qt_ref[1]], axis=1)
    feat = lax.broadcasted_iota(I32, (HEAD_W, 1), 0)
    zero = jnp.zeros_like(qt)
    qcat = jnp.concatenate([jnp.where(feat < DQK, qt, zero), jnp.where(feat < DQK, zero, qt)], axis=1)
    key = lax.broadcasted_iota(I32, (tk, 2 * tq), 0)
    qry = lax.broadcasted_iota(I32, (tk, 2 * tq), 1) % tq

    def set_scores(slot, s, mask):
        if mask is not None:
            s = jnp.where(mask, s, NEG)
        s_bufs[slot][...] = s
        cm_ref[slot:slot + 1, :] = jnp.max(s, axis=0, keepdims=True)

    def scores_into(j, slot, mask=None):
        off = pl.multiple_of(j * tk, tk)
        set_scores(slot, _dot(k_ref[pl.ds(off, tk), :], qcat), mask)

    def softmax_pv(j, slot):
        pv = _dot(vt_ref[jnp.maximum(j - 1, 0)], p_bufs[1 - slot][...])
        m = stat_ref[0:1, :]
        m_new = jnp.maximum(m, cm_ref[slot:slot + 1, :])
        alpha = jnp.exp(m - m_new)
        p = jnp.exp(s_bufs[slot][...] - m_new)
        stat_ref[0:1, :] = m_new
        stat_ref[1:2, :] = alpha * stat_ref[1:2, :] + jnp.sum(p, axis=0, keepdims=True)
        p_bufs[slot][...] = p.astype(BF16)
        acc_ref[...] = alpha * (acc_ref[...] + pv)

    stat_ref[0:1, :] = jnp.full((1, 2 * tq), NEG, F32)
    stat_ref[1:2, :] = jnp.zeros((1, 2 * tq), F32)
    acc_ref[...] = jnp.zeros(acc_ref.shape, F32)
    p_b[...] = jnp.zeros(p_b.shape, BF16)
    scores_into(0, 0)

    def body(t, _):
        scores_into(2 * t + 1, 1)
        softmax_pv(2 * t, 0)
        scores_into(2 * t + 2, 0)
        softmax_pv(2 * t + 1, 1)
        return 0

    lax.fori_loop(0, qi, body, 0)
    set_scores(0, s_a[...], key <= qry)
    scores_into(2 * qi + 1, 1, key + tk <= qry)
    softmax_pv(2 * qi, 0)
    softmax_pv(2 * qi + 1, 1)
    acc = (acc_ref[...] + _dot(vt_ref[2 * qi + 1], p_b[...])) / stat_ref[1:2, :]
    lam = _lam_value(l1_ref, l2_ref, l3_ref, l4_ref, lam_init)
    o = (acc[:, :tq] - lam * acc[:, tq:]).T
    o = o * lax.rsqrt(jnp.mean(o * o, axis=-1, keepdims=True) + EPS) * og_ref[...]
    o_ref[...] = (o * (1.0 - lam_init)).astype(BF16)


def _dattn_prompt(qt, k, vt, lams, og, lam_init):
    nb, nk, _, tk = qt.shape
    t = nk * tk
    tq = 2 * tk
    nq = t // tq
    lsp = pl.BlockSpec((1, DQK), lambda b, h, i: (0, 0))
    return pl.pallas_call(
        functools.partial(_dattn_kernel, tq=tq, lam_init=lam_init),
        grid=(nb, N_HEADS, nq),
        in_specs=[pl.BlockSpec((None, 2, HEAD_W, tk), lambda b, h, i: (b, i, h, 0)),
                  pl.BlockSpec((t, HEAD_W), lambda b, h, i: (b, h)),
                  pl.BlockSpec((None, nk, HEAD_W, tk), lambda b, h, i: (b, 0, h, 0)),
                  lsp, lsp, lsp, lsp,
                  pl.BlockSpec((1, HEAD_W), lambda b, h, i: (0, 0))],
        out_specs=pl.BlockSpec((tq, HEAD_W), lambda b, h, i: (b * nq + i, h)),
        out_shape=jax.ShapeDtypeStruct(k.shape, BF16),
        scratch_shapes=[pltpu.VMEM((tk, 2 * tq), F32), pltpu.VMEM((tk, 2 * tq), F32),
                        pltpu.VMEM((tk, 2 * tq), BF16), pltpu.VMEM((tk, 2 * tq), BF16),
                        pltpu.VMEM((SUBLANES, 2 * tq), F32), pltpu.VMEM((SUBLANES, 2 * tq), F32),
                        pltpu.VMEM((HEAD_W, 2 * tq), F32)],
        compiler_params=_cparams(("arbitrary",) * 3, 32 * 2**20),
        name="diff_attn_prompt",
    )(qt, k, vt, *[x.reshape(1, DQK) for x in lams], og.reshape(1, HEAD_W))


def _dattn_paged_kernel(pt_ref, q_ref, *refs, pps, n_new, n_pages, lam_init):
    k_refs = refs[:pps]
    v_refs = refs[pps:2 * pps]
    kn_ref, vn_ref, l1_ref, l2_ref, l3_ref, l4_ref, og_ref, o_ref, s_scr, m_sc, l_sc, acc_sc = refs[2 * pps:]
    phase = pl.program_id(1)
    step_id = pl.program_id(2)
    nsteps = pl.num_programs(2)
    prow = s_scr.shape[2]
    nchunk = prow // LANES
    lane = lax.broadcasted_iota(I32, (SUBLANES, LANES), 1)
    lane_head = lane % N_HEADS
    page_head = lax.broadcasted_iota(I32, (SUBLANES, prow), 1) % N_HEADS
    row = lax.broadcasted_iota(I32, (SUBLANES, prow), 0)
    tile = lambda x: jnp.concatenate([x] * nchunk, axis=1)

    def lane_class_reduce(x, op):
        for sh in (8, 16, 32, 64):
            x = op(x, pltpu.roll(x, sh, 1))
        return x

    def score_pages(ks, first, valid):
        q = q_ref[...]
        parts = []
        for pi, kp in enumerate(ks):
            sf = _dot_nt(q, kp.astype(BF16))
            page = []
            for c in range(nchunk):
                blk = sf[:, c * LANES:(c + 1) * LANES]
                s = jnp.zeros((SUBLANES, LANES), F32)
                for h in range(N_HEADS):
                    s = jnp.where(lane_head == h, blk[h * SUBLANES:(h + 1) * SUBLANES, :], s)
                page.append(s)
            s_page = jnp.concatenate(page, axis=1)
            if valid is not None:
                s_page = jnp.where(valid, s_page, NEG)
                page = [s_page[:, c * LANES:(c + 1) * LANES] for c in range(nchunk)]
            s_scr[first + pi] = s_page
            parts += page
        m_old = m_sc[...]
        m_new = jnp.maximum(m_old, functools.reduce(jnp.maximum, parts))
        l_sc[...] = jnp.exp(m_old - m_new) * l_sc[...] + functools.reduce(jnp.add, [jnp.exp(s - m_new) for s in parts])
        m_sc[...] = m_new

    def value_pages(vs, first):
        lam = _lam_value(l1_ref, l2_ref, l3_ref, l4_ref, lam_init)
        m_full = tile(m_sc[...])
        l_full = tile(l_sc[...])
        pv = jnp.zeros(acc_sc.shape, F32)
        for pi, vp in enumerate(vs):
            e = jnp.exp(s_scr[first + pi] - m_full) / l_full
            a = jnp.where(row < n_new, e - lam * pltpu.roll(e, SUBLANES - n_new, 0), 0.0)
            pexp = jnp.concatenate([jnp.where(page_head == h, a, 0.0) for h in range(N_HEADS)], axis=0)
            pv = pv + _dot(pexp.astype(BF16), vp.astype(BF16))
        acc_sc[...] = acc_sc[...] + pv

    @pl.when(phase == 0)
    def _():
        @pl.when(step_id == 0)
        def _():
            m_sc[...] = jnp.full(m_sc.shape, NEG, F32)
            l_sc[...] = jnp.zeros(l_sc.shape, F32)

        score_pages([r[...] for r in k_refs], step_id * pps, None)

        @pl.when(step_id == nsteps - 1)
        def _():
            tok = lax.broadcasted_iota(I32, (SUBLANES, prow), 1) // N_HEADS
            score_pages([kn_ref[...]], n_pages, tok <= row % n_new)
            m_lane = m_sc[...]
            m_head = lane_class_reduce(m_lane, jnp.maximum)
            l_sc[...] = lane_class_reduce(l_sc[...] * jnp.exp(m_lane - m_head), jnp.add)
            m_sc[...] = m_head

    @pl.when(phase == 1)
    def _():
        @pl.when(step_id == 0)
        def _():
            acc_sc[...] = jnp.zeros(acc_sc.shape, F32)

        value_pages([r[...] for r in v_refs], step_id * pps)

        @pl.when(step_id == nsteps - 1)
        def _():
            value_pages([vn_ref[...]], n_pages)
            acc = acc_sc[...]
            outs = []
            for h in range(N_HEADS):
                o = acc[h * SUBLANES:(h + 1) * SUBLANES]
                o = o * lax.rsqrt(jnp.mean(o * o, axis=-1, keepdims=True) + EPS) * og_ref[...]
                outs.append(o * (1.0 - lam_init))
            o_ref[...] = jnp.concatenate(outs, axis=1).astype(BF16)


def _dattn_paged(qall, cache_k, cache_v, page_table, k_new, v_new, lams, og, n_new, lam_init):
    nb, n_pages = page_table.shape
    pps = PAGES_PER_STEP
    prow = cache_k.shape[1]
    nsteps = n_pages // pps
    kspec = lambda j: pl.BlockSpec(
        (None, prow, HEAD_W),
        lambda b, ph, s, pt, j=j: (pt[b * n_pages + (s * (1 - ph) + (nsteps - 1) * ph) * pps + j], 0, 0))
    vspec = lambda j: pl.BlockSpec(
        (None, prow, HEAD_W), lambda b, ph, s, pt, j=j: (pt[b * n_pages + s * ph * pps + j], 0, 0))
    lsp = pl.BlockSpec((1, DQK), lambda b, ph, s, pt: (0, 0))
    nspec = pl.BlockSpec((None, k_new.shape[1], HEAD_W), lambda b, ph, s, pt: (b, 0, 0))
    grid_spec = pltpu.PrefetchScalarGridSpec(
        num_scalar_prefetch=1,
        grid=(nb, 2, nsteps),
        in_specs=[pl.BlockSpec((None, N_HEADS * SUBLANES, HEAD_W), lambda b, ph, s, pt: (b, 0, 0))]
                 + [kspec(j) for j in range(pps)] + [vspec(j) for j in range(pps)]
                 + [nspec, nspec, lsp, lsp, lsp, lsp, pl.BlockSpec((1, HEAD_W), lambda b, ph, s, pt: (0, 0))],
        out_specs=pl.BlockSpec((SUBLANES, N_HEADS * HEAD_W), lambda b, ph, s, pt: (b, 0)),
        scratch_shapes=[pltpu.VMEM((n_pages + 1, SUBLANES, prow), F32),
                        pltpu.VMEM((SUBLANES, LANES), F32), pltpu.VMEM((SUBLANES, LANES), F32),
                        pltpu.VMEM((N_HEADS * SUBLANES, HEAD_W), F32)])
    return pl.pallas_call(
        functools.partial(_dattn_paged_kernel, pps=pps, n_new=n_new, n_pages=n_pages, lam_init=lam_init),
        grid_spec=grid_spec,
        out_shape=jax.ShapeDtypeStruct((nb * SUBLANES, N_HEADS * HEAD_W), BF16),
        compiler_params=_cparams(("arbitrary",) * 3,
                                 4 * pps * prow * HEAD_W * 4 + (n_pages + 1) * SUBLANES * prow * 4 + 24 * 2**20),
        name="diff_attn_paged",
    )(page_table.reshape(-1), qall, *([cache_k] * pps), *([cache_v] * pps), k_new, v_new,
      *[x.reshape(1, DQK) for x in lams], og.reshape(1, HEAD_W))


def _split3(a):
    hi = a.astype(BF16)
    r1 = a - hi.astype(F32)
    mid = r1.astype(BF16)
    lo = (r1 - mid.astype(F32)).astype(BF16)
    return hi, mid, lo


def _mlstm_kernel(qk_ref, v_ref, o_ref, gt_ref, cw_ref, cb_ref, gb_ref, og_ref,
                  C0_ref, n0_ref, m0_ref, cv0_ref,
                  h_ref, C_ref, n_ref, m_ref, cv_ref, xbuf, *, L, rows_in, n_valid):
    c = pl.program_id(1)
    width = N_HEADS * HEAD_W

    @pl.when(c == 0)
    def _():
        C_ref[...] = C0_ref[...]
        n_ref[...] = n0_ref[...]
        m_ref[...] = m0_ref[...]
        xbuf[0:SUBLANES, :] = cv0_ref[...]

    def padded(ref):
        x = ref[...]
        if rows_in < L:
            x = jnp.concatenate([x, jnp.zeros((L - rows_in, x.shape[1]), x.dtype)], axis=0)
        return x

    xbuf[SUBLANES:SUBLANES + L, :] = padded(qk_ref)
    taps = [xbuf[SUBLANES - (CONV_W - 1) + j:SUBLANES - (CONV_W - 1) + j + L, :] * cw_ref[j:j + 1, :]
            for j in range(CONV_W)]
    conv = cb_ref[...] + functools.reduce(jnp.add, taps)
    cv_ref[...] = xbuf[n_valid:n_valid + SUBLANES, :]
    xbuf[0:SUBLANES, :] = xbuf[L:L + SUBLANES, :]
    conv = _silu(conv)
    q_all = conv[:, :width].astype(BF16)
    k_all = conv[:, width:] * (HEAD_W ** -0.5)
    v_all = padded(v_ref).astype(BF16)
    o_all = padded(o_ref)

    g = padded(gt_ref) + gb_ref[...]
    lane = lax.broadcasted_iota(I32, (L, LANES), 1)
    rowi = lax.broadcasted_iota(I32, (L, LANES), 0)
    is_f = (lane >= N_HEADS) & (lane < 2 * N_HEADS)
    logsig = jnp.minimum(g, 0.0) - jnp.log1p(jnp.exp(-jnp.abs(g)))
    a = jnp.where(is_f, logsig, g)
    if n_valid < L:
        a = jnp.where(rowi < n_valid, a, jnp.where(is_f, 0.0, NEG))
    tri = (lax.broadcasted_iota(I32, (L, L), 1) <= lax.broadcasted_iota(I32, (L, L), 0))
    tri_b = tri.astype(BF16)
    cum = functools.reduce(jnp.add, [_dot(tri_b, part) for part in _split3(jnp.where(is_f, a, 0.0))])
    a_t = a.T
    cum_t = cum.T

    m_all = m_ref[...]
    n_all = n_ref[...]
    m_rows, n_rows = [], []
    for h in range(N_HEADS):
        sl = slice(h * HEAD_W, (h + 1) * HEAD_W)
        b_col = cum[:, N_HEADS + h:N_HEADS + h + 1]
        b_row = cum_t[N_HEADS + h:N_HEADS + h + 1, :]
        i_col = a[:, h:h + 1]
        i_row = a_t[h:h + 1, :]
        m_prev = m_all[h:h + 1, 0:1]
        d = jnp.where(tri, (b_col - b_row) + i_row, NEG)
        inter = b_col + m_prev
        mt = jnp.maximum(inter, jnp.max(d, axis=-1, keepdims=True))
        w_intra = jnp.exp(d - mt)
        w_inter = jnp.exp(inter - mt)
        qh, kh, vh = q_all[:, sl], k_all[:, sl], v_all[:, sl]
        C_h = C_ref[h]
        n_h = n_all[h:h + 1, :]
        s = _dot_nt(qh, kh.astype(BF16)) * w_intra
        num = _dot(s.astype(BF16), vh) + w_inter * _dot(qh, C_h.astype(BF16))
        den = (jnp.sum(s, axis=-1, keepdims=True)
               + w_inter * jnp.sum(qh.astype(F32) * n_h.astype(BF16).astype(F32), axis=-1, keepdims=True))
        hh = num / jnp.maximum(jnp.abs(den), jnp.exp(-mt))
        m_last = mt[L - 1:L, :]
        b_last = b_col[L - 1:L, :]
        gk = jnp.exp((b_last - b_col) + i_col - m_last) * kh
        decay = jnp.exp(b_last + m_prev - m_last)
        C_ref[h] = decay * C_h + _dot_tn(gk.astype(BF16), vh)
        n_rows.append(decay * n_h + jnp.sum(gk, axis=0, keepdims=True))
        m_rows.append(jnp.broadcast_to(m_last, (1, LANES)))
        hn = hh * lax.rsqrt(jnp.mean(hh * hh, axis=-1, keepdims=True) + EPS) * og_ref[...]
        out = hn * jax.nn.sigmoid(o_all[:, sl])
        h_ref[:, sl] = out[:rows_in].astype(BF16)
    n_ref[...] = jnp.concatenate(n_rows, axis=0)
    m_ref[...] = jnp.concatenate(m_rows, axis=0)


def _mlstm(proj, gates, conv_w, conv_b, gate_b, og, C0, n0, m0, cv0, nb, rows_in, nchunks, n_valid):
    L = MLSTM_L
    w = N_HEADS * HEAD_W
    row = lambda cb: (lambda b, c: (b * nchunks + c, cb))
    st3 = lambda b, c: (b, 0, 0)
    kern = functools.partial(_mlstm_kernel, L=L, rows_in=rows_in, n_valid=n_valid)
    return pl.pallas_call(
        kern,
        grid=(nb, nchunks),
        in_specs=[pl.BlockSpec((rows_in, 2 * w), lambda b, c: (b * nchunks + c, 0)),
                  pl.BlockSpec((rows_in, w), row(2)), pl.BlockSpec((rows_in, w), row(3)),
                  pl.BlockSpec((rows_in, LANES), lambda b, c: (b * nchunks + c, 0)),
                  pl.BlockSpec((SUBLANES, 2 * w), lambda b, c: (0, 0)),
                  pl.BlockSpec((1, 2 * w), lambda b, c: (0, 0)),
                  pl.BlockSpec((1, LANES), lambda b, c: (0, 0)),
                  pl.BlockSpec((1, HEAD_W), lambda b, c: (0, 0)),
                  pl.BlockSpec((None, N_HEADS, HEAD_W, HEAD_W), lambda b, c: (b, 0, 0, 0)),
                  pl.BlockSpec((None, N_HEADS, HEAD_W), st3),
                  pl.BlockSpec((None, N_HEADS, LANES), st3),
                  pl.BlockSpec((None, SUBLANES, 2 * w), st3)],
        out_specs=[pl.BlockSpec((rows_in, w), lambda b, c: (b * nchunks + c, 0)),
                   pl.BlockSpec((None, N_HEADS, HEAD_W, HEAD_W), lambda b, c: (b, 0, 0, 0)),
                   pl.BlockSpec((None, N_HEADS, HEAD_W), st3),
                   pl.BlockSpec((None, N_HEADS, LANES), st3),
                   pl.BlockSpec((None, SUBLANES, 2 * w), st3)],
        out_shape=[jax.ShapeDtypeStruct((nb * nchunks * rows_in, w), BF16),
                   jax.ShapeDtypeStruct((nb, N_HEADS, HEAD_W, HEAD_W), F32),
                   jax.ShapeDtypeStruct((nb, N_HEADS, HEAD_W), F32),
                   jax.ShapeDtypeStruct((nb, N_HEADS, LANES), F32),
                   jax.ShapeDtypeStruct((nb, SUBLANES, 2 * w), F32)],
        scratch_shapes=[pltpu.VMEM((SUBLANES + L, 2 * w), F32)],
        compiler_params=_cparams(("arbitrary", "arbitrary"), 40 * 2**20),
        name="mlstm_chunkwise",
    )(proj, proj, proj, gates, conv_w, conv_b, gate_b, og, C0, n0, m0, cv0)


def _outproj_kernel(hm_ref, ha_ref, x_ref, g1_ref, sc_ref, sh_ref, ng_ref, wo_ref, wr_ref, br_ref,
                    x1_ref, hn_ref, ids_ref, gates_ref):
    half = hm_ref.shape[1]
    mix = _dot(hm_ref[...], wo_ref[0:half, :]) + _dot(ha_ref[...], wo_ref[half:2 * half, :])
    x1 = x_ref[...] + g1_ref[...] * mix
    x1_ref[...] = x1
    y = x1 * lax.rsqrt(jnp.mean(x1 * x1, axis=-1, keepdims=True) + EPS) * ng_ref[...]
    hn = y * (1.0 + sc_ref[...]) + sh_ref[...]
    hn_ref[...] = _pack_bf16_pairs(hn)
    logits = _dot(hn.astype(BF16), wr_ref[...]) + br_ref[...]

    lane = lax.broadcasted_iota(I32, logits.shape, 1).astype(F32)
    big = 1000.0
    is_g = lane < N_GROUPS
    gl = jnp.where(is_g, logits, NEG)
    gmax = jnp.max(gl, axis=-1, keepdims=True)
    g_idx = jnp.min(jnp.where(gl == gmax, lane, big), axis=-1, keepdims=True)
    g_p = 1.0 / jnp.sum(jnp.where(is_g, jnp.exp(gl - gmax), 0.0), axis=-1, keepdims=True)
    e_lo = N_GROUPS + EXPERTS_PER_GROUP * g_idx
    in_grp = (lane >= e_lo) & (lane < e_lo + EXPERTS_PER_GROUP)
    el = jnp.where(in_grp, logits, NEG)
    ex = jnp.where(in_grp, jnp.exp(el - jnp.max(el, axis=-1, keepdims=True)), 0.0)
    p = jnp.where(in_grp, ex / jnp.sum(ex, axis=-1, keepdims=True), -1.0)
    top1 = jnp.max(p, axis=-1, keepdims=True)
    idx1 = jnp.min(jnp.where(p == top1, lane, big), axis=-1, keepdims=True)
    p2 = jnp.where(lane == idx1, -1.0, p)
    top2 = jnp.max(p2, axis=-1, keepdims=True)
    idx2 = jnp.min(jnp.where(p2 == top2, lane, big), axis=-1, keepdims=True)
    tsum = top1 + top2
    ids_ref[...] = jnp.where(lane == 0.0, idx1 - N_GROUPS,
                             jnp.where(lane == 1.0, idx2 - N_GROUPS, -1.0)).astype(I32)
    gates_ref[...] = jnp.where(lane == 0.0, g_p * top1 / tsum, jnp.where(lane == 1.0, g_p * top2 / tsum, 0.0))


def _outproj(hm, ha, x, mod_specs, mods, ng, wo, wr, br, tm):
    m, d = x.shape
    half = hm.shape[1]
    row = lambda wdt: pl.BlockSpec((tm, wdt), lambda i: (i, 0))
    const = lambda shp: pl.BlockSpec(shp, lambda i: (0, 0))
    return pl.pallas_call(
        _outproj_kernel,
        grid=(m // tm,),
        in_specs=[row(half), row(half), row(d), *mod_specs, const((1, d)), const((d, d)),
                  const((d, LANES)), const((1, LANES))],
        out_specs=[row(d), row(d // 2), row(LANES), row(LANES)],
        out_shape=[jax.ShapeDtypeStruct((m, d), F32), jax.ShapeDtypeStruct((m, d // 2), jnp.uint32),
                   jax.ShapeDtypeStruct((m, LANES), I32), jax.ShapeDtypeStruct((m, LANES), F32)],
        compiler_params=_cparams(("arbitrary",), 2 * d * d * 2 + 8 * tm * d * 4 + 8 * 2**20),
        name="outproj_norm2_route",
    )(hm, ha, x, *mods, ng.reshape(1, d), wo, wr, br)


def _rank_kernel(ids_ref, dest_ref, cnt_ref, carry, pstart):
    pss = pl.program_id(0)
    i = pl.program_id(1)
    ids = ids_ref[...]
    tm = ids.shape[0]
    lane = lax.broadcasted_iota(I32, ids.shape, 1)
    o0 = lane == ids[:, 0:1]
    o1 = lane == ids[:, 1:2]
    onehot = jnp.where(o0 | o1, 1.0, 0.0)
    col_counts = jnp.sum(onehot, axis=0, keepdims=True)

    @pl.when((pss == 0) & (i == 0))
    def _():
        carry[...] = jnp.zeros(carry.shape, F32)

    @pl.when(pss == 0)
    def _():
        carry[...] = carry[...] + col_counts

    @pl.when((pss == 1) & (i == 0))
    def _():
        cnt = carry[...]
        cnt_ref[...] = cnt
        blocks = jnp.floor((cnt + (MOE_ROWS - 1)) * (1.0 / MOE_ROWS))
        earlier = lax.broadcasted_iota(I32, (LANES, LANES), 0) < lax.broadcasted_iota(I32, (LANES, LANES), 1)
        pstart[...] = _dot(blocks.astype(BF16), earlier.astype(BF16)) * MOE_ROWS
        carry[...] = jnp.zeros(carry.shape, F32)

    @pl.when(pss == 1)
    def _():
        strict = lax.broadcasted_iota(I32, (tm, tm), 1) < lax.broadcasted_iota(I32, (tm, tm), 0)
        before = _dot(strict.astype(BF16), onehot.astype(BF16)) + carry[0:1, :] + pstart[0:1, :]
        r0 = jnp.sum(jnp.where(o0, before, 0.0), axis=-1, keepdims=True)
        r1 = jnp.sum(jnp.where(o1, before, 0.0), axis=-1, keepdims=True)
        dest_ref[...] = jnp.where(lane == 0, r0, jnp.where(lane == 1, r1, 0.0)).astype(I32)
        carry[...] = carry[...] + col_counts


def _ranks(ids, tm):
    m = ids.shape[0]
    return pl.pallas_call(
        _rank_kernel,
        grid=(2, m // tm),
        in_specs=[pl.BlockSpec((tm, LANES), lambda p, i: (i, 0))],
        out_specs=[pl.BlockSpec((tm, LANES), lambda p, i: (i * p, 0)),
                   pl.BlockSpec((SUBLANES, LANES), lambda p, i: (0, 0))],
        out_shape=[jax.ShapeDtypeStruct((m, LANES), I32), jax.ShapeDtypeStruct((SUBLANES, LANES), F32)],
        scratch_shapes=[pltpu.VMEM((SUBLANES, LANES), F32), pltpu.VMEM((SUBLANES, LANES), F32)],
        compiler_params=_cparams(("arbitrary", "arbitrary"), 16 * 2**20),
        name="moe_ranks",
    )(ids)


def _row_copies(idx_ref, t0, tm, copy):
    def body(t, _):
        for k in range(2):
            copy(t, k, idx_ref[2 * (t0 + t) + k]).start(priority=k)
        return 0
    lax.fori_loop(0, tm, body, 0, unroll=8)


def _dispatch_kernel(dest_ref, h_ref, xin_ref, xs_ref, sem, *, tm, base):
    del xin_ref
    t0 = base + pl.program_id(0) * tm
    _row_copies(dest_ref, t0, tm,
                lambda t, k, d: pltpu.make_async_copy(h_ref.at[pl.ds(t, 1)], xs_ref.at[pl.ds(d, 1)], sem))
    for _ in range(2):
        pltpu.make_async_copy(h_ref, xs_ref.at[pl.ds(0, tm)], sem).wait()


def _dispatch(dest_flat, hn, xs, tm, base):
    m, d = hn.shape
    any_spec = pl.BlockSpec(memory_space=pl.ANY)
    grid_spec = pltpu.PrefetchScalarGridSpec(
        num_scalar_prefetch=1, grid=(m // tm,),
        in_specs=[pl.BlockSpec((tm, d), lambda i, dr: (i, 0)), any_spec], out_specs=any_spec,
        scratch_shapes=[pltpu.SemaphoreType.DMA])
    return pl.pallas_call(
        functools.partial(_dispatch_kernel, tm=tm, base=base),
        grid_spec=grid_spec,
        out_shape=jax.ShapeDtypeStruct(xs.shape, xs.dtype),
        input_output_aliases={2: 0},
        compiler_params=_cparams(("arbitrary",), 4 * tm * d * 4 + 8 * 2**20),
        name="moe_dispatch",
    )(dest_flat, hn, xs)


def _expert_weights(b, be_ref, bf_ref, nx_ref, w_hbm, w_stage, w_bf16, sem):
    def copies(e):
        return [pltpu.make_async_copy(src.at[0, e], dst, sem.at[i])
                for i, (src, dst) in enumerate(zip(w_hbm, w_stage))]

    @pl.when(b == 0)
    def _():
        for c in copies(be_ref[0]):
            c.start()

    @pl.when(bf_ref[b] == 1)
    def _():
        for c in copies(be_ref[b]):
            c.wait()
        for stage, wb in zip(w_stage, w_bf16):
            wb[...] = stage[...].astype(BF16)

        @pl.when(nx_ref[b] >= 0)
        def _():
            for c in copies(nx_ref[b]):
                c.start()


def _block_rows(br_ref, b, out_ref, compute):
    nrows = br_ref[b]
    half = out_ref.shape[0] // 2
    top, bottom = pl.ds(0, half), pl.ds(half, half)

    @pl.when(nrows > half)
    def _():
        out_ref[...] = compute(pl.ds(0, 2 * half))

    @pl.when(jnp.logical_and(nrows > 0, nrows <= half))
    def _():
        out_ref[top, :] = compute(top)
        out_ref[bottom, :] = jnp.zeros((half, out_ref.shape[1]), out_ref.dtype)

    @pl.when(nrows == 0)
    def _():
        out_ref[...] = jnp.zeros(out_ref.shape, out_ref.dtype)


def _expert_up_kernel(be_ref, bf_ref, nx_ref, br_ref, x_ref, w1_hbm, w3_hbm, h_ref, w1f, w3f, w1b, w3b, sem):
    b = pl.program_id(0)
    _expert_weights(b, be_ref, bf_ref, nx_ref, (w1_hbm, w3_hbm), (w1f, w3f), (w1b, w3b), sem)

    def swiglu(rows):
        xl, xr = _unpack_bf16_pairs(x_ref[rows, :])
        half = xl.shape[1]
        up = lambda wb: _dot(xl, wb[0:half, :]) + _dot(xr, wb[half:2 * half, :])
        return (_silu(up(w1b)) * up(w3b)).astype(BF16)

    _block_rows(br_ref, b, h_ref, swiglu)


def _expert_up(tabs, xs, w1, w3):
    rows, dp = xs.shape
    d, f = w1.shape[-2:]
    assert d == 2 * dp
    nb = rows // MOE_ROWS
    any_spec = pl.BlockSpec(memory_space=pl.ANY)
    grid_spec = pltpu.PrefetchScalarGridSpec(
        num_scalar_prefetch=4, grid=(nb,),
        in_specs=[pl.BlockSpec((MOE_ROWS, dp), lambda b, *_: (b, 0)), any_spec, any_spec],
        out_specs=pl.BlockSpec((MOE_ROWS, f), lambda b, *_: (b, 0)),
        scratch_shapes=[pltpu.VMEM((d, f), F32), pltpu.VMEM((d, f), F32),
                        pltpu.VMEM((d, f), BF16), pltpu.VMEM((d, f), BF16), pltpu.SemaphoreType.DMA((2,))])
    return pl.pallas_call(
        _expert_up_kernel,
        grid_spec=grid_spec,
        out_shape=jax.ShapeDtypeStruct((rows, f), BF16),
        compiler_params=_cparams(("arbitrary",), 2 * d * f * 6 + 4 * MOE_ROWS * d * 4 + 8 * 2**20),
        name="moe_expert_up",
    )(*tabs, xs, w1, w3)


def _expert_down_kernel(be_ref, bf_ref, nx_ref, br_ref, h_ref, w2_hbm, y_ref, w2f, w2b, sem):
    b = pl.program_id(0)
    _expert_weights(b, be_ref, bf_ref, nx_ref, (w2_hbm,), (w2f,), (w2b,), sem)
    _block_rows(br_ref, b, y_ref, lambda rows: _dot(h_ref[rows, :], w2b[...]))


def _expert_down(tabs, hs, w2):
    rows, f = hs.shape
    d = w2.shape[-1]
    nb = rows // MOE_ROWS
    grid_spec = pltpu.PrefetchScalarGridSpec(
        num_scalar_prefetch=4, grid=(nb,),
        in_specs=[pl.BlockSpec((MOE_ROWS, f), lambda b, *_: (b, 0)), pl.BlockSpec(memory_space=pl.ANY)],
        out_specs=pl.BlockSpec((MOE_ROWS, d), lambda b, *_: (b, 0)),
        scratch_shapes=[pltpu.VMEM((f, d), F32), pltpu.VMEM((f, d), BF16), pltpu.SemaphoreType.DMA((1,))])
    return pl.pallas_call(
        _expert_down_kernel,
        grid_spec=grid_spec,
        out_shape=jax.ShapeDtypeStruct((rows, d), F32),
        compiler_params=_cparams(("arbitrary",), f * d * 6 + 6 * MOE_ROWS * d * 4 + 8 * 2**20),
        name="moe_expert_down",
    )(*tabs, hs, w2)


def _combine_kernel(dest_ref, x1_ref, g2_ref, gates_ref, y_ref, o_ref, ybuf, sem, *, tm, base):
    i = pl.program_id(0)
    slot = i % 2

    def gather(step, s):
        _row_copies(dest_ref, base + step * tm, tm,
                    lambda t, k, d: pltpu.make_async_copy(y_ref.at[pl.ds(d, 1)], ybuf.at[s, k, pl.ds(t, 1)],
                                                          sem.at[s]))

    @pl.when(i == 0)
    def _():
        gather(0, 0)

    @pl.when(i + 1 < pl.num_programs(0))
    def _():
        gather(i + 1, 1 - slot)

    for k in range(2):
        pltpu.make_async_copy(y_ref.at[pl.ds(0, tm)], ybuf.at[slot, k], sem.at[slot]).wait()
    gates = gates_ref[...]
    ff = gates[:, 0:1] * ybuf[slot, 0] + gates[:, 1:2] * ybuf[slot, 1]
    o_ref[...] = x1_ref[...] + g2_ref[...] * ff


def _combine(dest_flat, x1, g2_spec, g2, gates, y, tm, base):
    m, d = x1.shape
    grid_spec = pltpu.PrefetchScalarGridSpec(
        num_scalar_prefetch=1, grid=(m // tm,),
        in_specs=[pl.BlockSpec((tm, d), lambda i, dr: (i, 0)), g2_spec,
                  pl.BlockSpec((tm, LANES), lambda i, dr: (i, 0)),
                  pl.BlockSpec(memory_space=pl.ANY)],
        out_specs=pl.BlockSpec((tm, d), lambda i, dr: (i, 0)),
        scratch_shapes=[pltpu.VMEM((2, 2, tm, d), F32), pltpu.SemaphoreType.DMA((2,))])
    return pl.pallas_call(
        functools.partial(_combine_kernel, tm=tm, base=base),
        grid_spec=grid_spec,
        out_shape=jax.ShapeDtypeStruct((m, d), F32),
        compiler_params=_cparams(("arbitrary",), 8 * tm * d * 4 + 8 * 2**20),
        name="moe_combine",
    )(dest_flat, x1, g2, gates, y)


def kernel(x_prompt, x_sample, cache_k, cache_v, page_table, state_mlstm_C, state_mlstm_n, state_mlstm_m, state_conv,
           c_prompt, c_sample, w_ada, b_ada, norm1_g, norm2_g, w_in, conv_w, conv_b, b_igate, b_fgate, mlstm_out_g,
           q_norm_g, k_norm_g, lam_q1, lam_k1, lam_q2, lam_k2, diff_out_g, w_out, w_grp, b_grp, w_rt, b_rt, w1, w3, w2):
    B, T, D = x_prompt.shape
    DB, TS, _ = x_sample.shape
    depth = w_ada.shape[0]
    assert depth == 1, "single-layer trunk"
    n_pages, page = page_table.shape[1], cache_k.shape[2]
    past = n_pages * page
    W = N_HEADS * HEAD_W
    lam_init = 0.8 - 0.6 * math.exp(-0.3 * 0)
    NP, NS = B * T, DB * TS
    SP = SAMPLE_PAD

    c_all = jnp.concatenate([c_prompt, c_sample, jnp.zeros((16 - B - DB, D), F32)], axis=0)
    w_main, w_gate = _regroup_w_in(w_in[0].T, 4 * W, 2 * N_HEADS)
    gate_b = jnp.pad(jnp.concatenate([b_igate[0], b_fgate[0]]), (0, LANES - 2 * N_HEADS)).reshape(1, LANES)
    cw = jnp.pad(conv_w[0], ((0, SUBLANES - CONV_W), (0, 0)))
    cb = conv_b[0].reshape(1, 2 * W)
    og_m = mlstm_out_g[0].reshape(1, HEAD_W)
    wo = w_out[0].astype(BF16)
    w_route = jnp.concatenate([w_grp[0], w_rt[0].transpose(1, 0, 2).reshape(D, N_EXPERTS)], axis=1)
    w_route = jnp.pad(w_route, ((0, 0), (0, LANES - N_GROUPS - N_EXPERTS))).astype(BF16)
    b_route = jnp.pad(jnp.concatenate([b_grp[0], b_rt[0].reshape(-1)]), (0, LANES - N_GROUPS - N_EXPERTS)).reshape(1, LANES)
    lams = (lam_q1[0], lam_k1[0], lam_q2[0], lam_k2[0])

    mod = _ada(c_all, w_ada[0], b_ada[0])
    mod3 = mod.reshape(16, 1, 6 * D)
    mod_s = jnp.repeat(mod[B:B + DB], SP, axis=0)

    def p_mod(chunk, tm):
        return pl.BlockSpec((None, 1, D), lambda i, *_: ((i * tm) // T, 0, chunk))

    def s_mod(chunk, tm):
        return pl.BlockSpec((tm, D), lambda i, *_: (i, chunk))

    xp = x_prompt.reshape(NP, D)
    tm_in = 1024
    proj_p, gates_p = _inproj(xp, (p_mod(1, tm_in), p_mod(0, tm_in)), (mod3, mod3), norm1_g[0], w_main, w_gate, tm_in)
    cos_p, sin_p = _rope_tables(jnp.arange(T, dtype=I32))
    knew_p, kb_p, vnew_p, qt_p, vt_p = _qkrope(proj_p, 4, cos_p, sin_p, q_norm_g[0], k_norm_g[0], 256, B)
    ha_p = _dattn_prompt(qt_p, kb_p, vt_p, lams, diff_out_g[0], lam_init)
    zeros = lambda *s: jnp.zeros(s, F32)
    hm_p, C_p, n_p, m_p, cv_p = _mlstm(
        proj_p, gates_p, cw, cb, gate_b, og_m,
        zeros(B, N_HEADS, HEAD_W, HEAD_W), zeros(B, N_HEADS, HEAD_W), zeros(B, N_HEADS, LANES),
        zeros(B, SUBLANES, 2 * W), B, MLSTM_L, T // MLSTM_L, MLSTM_L)
    tm_o = 256
    x1_p, hn2_p, ids_p, gts_p = _outproj(
        hm_p, ha_p, xp, (p_mod(2, tm_o), p_mod(4, tm_o), p_mod(3, tm_o)), (mod3, mod3, mod3),
        norm2_g[0], wo, w_route, b_route, tm_o)

    MS = DB * SP
    xs_pad = jnp.pad(x_sample, ((0, 0), (0, SP - TS), (0, 0))).reshape(MS, D)
    proj_s, gates_s = _inproj(xs_pad, (s_mod(1, MS), s_mod(0, MS)), (mod_s, mod_s), norm1_g[0], w_main, w_gate, MS)
    pos_s = past + (jnp.arange(MS, dtype=I32) % SP)
    cos_s, sin_s = _rope_tables(pos_s)
    knew_s, kb_s, vnew_s, q_s, vb_s = _qkrope(proj_s, 4, cos_s, sin_s, q_norm_g[0], k_norm_g[0], MS)
    qr = q_s.reshape(DB, SP, N_HEADS, 2, DQK)[:, :TS].transpose(0, 2, 3, 1, 4)
    zq = jnp.zeros_like(qr[:, :, 0])
    qall = jnp.stack([jnp.concatenate([qr[:, :, 0], zq], axis=-1),
                      jnp.concatenate([zq, qr[:, :, 1]], axis=-1)], axis=2).reshape(DB, N_HEADS * 2 * TS, HEAD_W)
    new_rows = page * N_HEADS
    pad_new = lambda a: jnp.pad(a.reshape(DB, SP * N_HEADS, HEAD_W), ((0, 0), (0, new_rows - SP * N_HEADS), (0, 0)))
    ha_s = _dattn_paged(qall, cache_k[0].reshape(-1, new_rows, HEAD_W), cache_v[0].reshape(-1, new_rows, HEAD_W),
                        page_table, pad_new(kb_s), pad_new(vb_s), lams, diff_out_g[0], TS, lam_init)
    m0_s = jnp.broadcast_to(state_mlstm_m[0][:, :, None], (DB, N_HEADS, LANES))
    cv0_s = jnp.pad(state_conv[0], ((0, 0), (SUBLANES - (CONV_W - 1), 0), (0, 0)))
    hm_s, C_s, n_s, m_s, cv_s = _mlstm(proj_s, gates_s, cw, cb, gate_b, og_m, state_mlstm_C[0], state_mlstm_n[0],
                                       m0_s, cv0_s, DB, SP, 1, TS)
    x1_s, hn2_s, ids_s, gts_s = _outproj(
        hm_s, ha_s, xs_pad, (s_mod(2, MS), s_mod(4, MS), s_mod(3, MS)), (mod_s, mod_s, mod_s),
        norm2_g[0], wo, w_route, b_route, MS)
    real = lambda a: a.reshape(DB, SP, -1)[:, :TS].reshape(NS, -1)
    x1_s, hn2_s, ids_s, gts_s = real(x1_s), real(hn2_s), real(ids_s), real(gts_s)

    NT = NP + NS
    tm_r = 256
    n_rank = -(-NT // tm_r) * tm_r
    ids_all = jnp.concatenate([ids_p, ids_s, jnp.full((n_rank - NT, LANES), -1, I32)], axis=0)
    dest2, cnt = _ranks(ids_all, tm_r)
    dest = dest2[:NT, :2].reshape(-1)
    counts = cnt[0, :N_EXPERTS].astype(I32)
    e_blocks = (counts + MOE_ROWS - 1) // MOE_ROWS
    pend = jnp.cumsum(e_blocks)
    n_blocks = (2 * NT + N_EXPERTS * (MOE_ROWS - 1)) // MOE_ROWS
    n_valid = pend[-1:]
    blk_id = jnp.arange(n_blocks, dtype=I32)
    blk = jnp.minimum(blk_id, n_valid[0] - 1)
    blk_e = jnp.sum((blk[:, None] >= pend[None, :]).astype(I32), axis=1)
    in_expert = blk_id - (pend - e_blocks)[blk_e]
    blk_rows = jnp.where(blk_id < n_valid[0], jnp.clip(counts[blk_e] - in_expert * MOE_ROWS, 0, MOE_ROWS), 0).astype(I32)
    blk_first = jnp.concatenate([jnp.ones((1,), I32), (blk_e[1:] != blk_e[:-1]).astype(I32)])
    first_pos = jnp.where(blk_first == 1, jnp.arange(n_blocks, dtype=I32), n_blocks)
    next_first = jnp.concatenate([lax.cummin(first_pos, reverse=True)[1:], jnp.full((1,), n_blocks, I32)])
    blk_next = jnp.where(next_first < n_blocks, blk_e[jnp.minimum(next_first, n_blocks - 1)], -1).astype(I32)
    tabs = (blk_e, blk_first, blk_next, blk_rows)
    xs_sorted = _dispatch(dest, hn2_p, jnp.zeros((n_blocks * MOE_ROWS, D // 2), jnp.uint32), 256, 0)
    xs_sorted = _dispatch(dest, hn2_s, xs_sorted, NS, NP)
    h_sorted = _expert_up(tabs, xs_sorted, w1, w3)
    y_sorted = _expert_down(tabs, h_sorted, w2)
    tm_c = 128
    y_p = _combine(dest, x1_p, pl.BlockSpec((None, 1, D), lambda i, dr: ((i * tm_c) // T, 0, 5)), mod3, gts_p,
                   y_sorted, tm_c, 0)
    g2_s = jnp.repeat(mod[B:B + DB, 5 * D:], TS, axis=0)
    y_s = _combine(dest, x1_s, pl.BlockSpec((NS, D), lambda i, dr: (0, 0)), g2_s, gts_s, y_sorted, NS, NP)

    st = lambda a: a[None]
    k5 = lambda a, nb, t: a.reshape(1, nb, t, N_HEADS, HEAD_W)
    tok_s = lambda a: a.reshape(DB, SP, -1)[:, :TS]
    conv_out = lambda cv: cv[None, :, SUBLANES - (CONV_W - 1):, :]
    return (y_p.reshape(B, T, D), y_s.reshape(DB, TS, D),
            k5(knew_p, B, T), k5(vnew_p, B, T), st(C_p), st(n_p), st(m_p[:, :, 0]), conv_out(cv_p),
            k5(tok_s(knew_s), DB, TS), k5(tok_s(vnew_s), DB, TS), st(C_s), st(n_s), st(m_s[:, :, 0]), conv_out(cv_s))
```
